```python
import math
import jax, jax.numpy as jnp
from jax import lax
import numpy as np

D_MODEL = 1024
BATCH = 8
SEQ = 2048
DEPTH = 1
DEC_BATCH = 128
DEC_SEQ = 4
PAST_LEN = 16384
PAGE_SIZE = 128

S5_WIDTH = D_MODEL // 2
S5_GROUP = 16
S5_GROUPS = S5_WIDTH // S5_GROUP
S5_STATE = 64
RET_HEADS = 4
RET_DK = D_MODEL // 8
RET_DV = D_MODEL // 8
RET_WIDTH = RET_HEADS * RET_DV
RET_CHUNK = 128
MIX_WIDTH = S5_WIDTH + RET_WIDTH
IN_WIDTH = S5_WIDTH + RET_HEADS * (2 * RET_DK + 2 * RET_DV)
IN_SPLITS = (S5_WIDTH,
             S5_WIDTH + RET_HEADS * RET_DK,
             S5_WIDTH + 2 * RET_HEADS * RET_DK,
             S5_WIDTH + 2 * RET_HEADS * RET_DK + RET_WIDTH)
D_FF = ((8 * D_MODEL // 3 + 255) // 256) * 256
N_MOD = 9
ROPE_BASE = 10000.0
EPS = 1e-6

kernel_name = 'hymba_s5_retnet_macaron_adaln_step'


def rmsnorm(x, g):
    xf = x.astype(jnp.float32)
    y = xf * lax.rsqrt(jnp.mean(xf * xf, axis=-1, keepdims=True) + EPS) * g.astype(jnp.float32)
    return y.astype(x.dtype)


def modulate(h, shift, scale):
    return h * (1.0 + scale) + shift


def swiglu(h, w_in, w_out):
    a, b = jnp.split(h @ w_in, 2, axis=-1)
    return (jax.nn.silu(a) * b) @ w_out


def head_norm(o):
    mu = jnp.mean(o, axis=-1, keepdims=True)
    var = jnp.mean(jnp.square(o - mu), axis=-1, keepdims=True)
    return (o - mu) * lax.rsqrt(var + EPS)


def rotary(x, pos):
    half = x.shape[-1] // 2
    inv = ROPE_BASE ** (-jnp.arange(half, dtype=jnp.float32) / half)
    ang = pos[:, None] * inv[None, :]
    cos = jnp.cos(ang)[None, :, None, :]
    sin = jnp.sin(ang)[None, :, None, :]
    x1, x2 = x[..., :half], x[..., half:]
    return jnp.concatenate([x1 * cos - x2 * sin, x1 * sin + x2 * cos], axis=-1)


def _ssm_combine(e1, e2):
    a1, b1 = e1
    a2, b2 = e2
    return a1 * a2, a2 * b1 + b2


def s5_scan(u, lam_re, lam_im, log_dt, b_re, b_im, c_re, c_im, d_skip, h0_re, h0_im):
    f = jnp.float32
    Bsz, L, _ = u.shape
    lam = lax.complex(lam_re.astype(f), lam_im.astype(f))
    dt = jnp.exp(log_dt.astype(f))[:, None]
    lam_bar = jnp.exp(lam * dt)
    b_bar = ((lam_bar - 1.0) / lam)[:, :, None] * lax.complex(b_re.astype(f), b_im.astype(f))
    cmat = lax.complex(c_re.astype(f), c_im.astype(f))
    ug = u.astype(f).reshape(Bsz, L, S5_GROUPS, S5_GROUP)
    bu = jnp.einsum('blgc,gnc->blgn', ug.astype(jnp.complex64), b_bar)
    a = jnp.broadcast_to(lam_bar, bu.shape)
    _, xs = lax.associative_scan(_ssm_combine, (a, bu), axis=1)
    steps = jnp.arange(1, L + 1, dtype=f)[:, None, None]
    decay_h0 = jnp.exp(lam[None] * (dt[None] * steps))
    h0 = lax.complex(h0_re.astype(f), h0_im.astype(f))
    xs = xs + decay_h0[None] * h0[:, None]
    y = jnp.einsum('gcn,blgn->blgc', cmat, xs).real + d_skip.astype(f).reshape(S5_GROUPS, S5_GROUP) * ug
    h_last = xs[:, -1]
    return y.reshape(Bsz, L, S5_WIDTH), h_last.real, h_last.imag


def retention_chunked(q, k, v, s0):
    Bsz, L, H, _ = q.shape
    dv = v.shape[-1]
    chunk = RET_CHUNK if L % RET_CHUNK == 0 else L
    n_chunks = L // chunk
    lg = jnp.log1p(-jnp.exp2(-5.0 - jnp.arange(H, dtype=jnp.float32)))
    idx = jnp.arange(chunk, dtype=jnp.float32)
    diff = idx[:, None] - idx[None, :]
    intra = jnp.where(diff >= 0, jnp.exp(lg[:, None, None] * jnp.maximum(diff, 0.0)), 0.0)
    q_dec = jnp.exp(lg[None, :] * (idx[:, None] + 1.0))
    k_dec = jnp.exp(lg[None, :] * (chunk - 1.0 - idx)[:, None])
    c_dec = jnp.exp(lg * chunk)

    def to_chunks(t):
        return jnp.moveaxis(t.reshape(Bsz, n_chunks, chunk, H, t.shape[-1]), 1, 0)

    def step(S, blk):
        qc, kc, vc = blk
        scores = jnp.einsum('bihd,bjhd->bhij', qc, kc) * intra[None]
        inner = jnp.einsum('bhij,bjhv->bihv', scores, vc)
        cross = jnp.einsum('bihd,bhdv->bihv', qc, S) * q_dec[None, :, :, None]
        S_new = S * c_dec[None, :, None, None] + jnp.einsum('bjhd,bjhv->bhdv', kc * k_dec[None, :, :, None], vc)
        return S_new, inner + cross

    s_last, out = lax.scan(step, s0, (to_chunks(q), to_chunks(k), to_chunks(v)))
    out = jnp.moveaxis(out, 0, 1).reshape(Bsz, L, H, dv)
    return out, s_last


def decoder_layer(x, c, pos, h0_re, h0_im, s0, lw):
    (w_ada, b_ada, norm_ffn1, ffn1_w_in, ffn1_w_out, norm_mix, w_in_mix,
     s5_lambda_re, s5_lambda_im, s5_log_dt, s5_b_re, s5_b_im, s5_c_re, s5_c_im, s5_d, s5_w_glu,
     w_out_mix, norm_ffn2, ffn2_w_in, ffn2_w_out) = lw
    f = jnp.float32
    mod = jax.nn.silu(c) @ w_ada + b_ada
    sh1, sc1, g1, sh2, sc2, g2, sh3, sc3, g3 = jnp.split(mod[:, None, :], N_MOD, axis=-1)
    h = modulate(rmsnorm(x, norm_ffn1), sh1, sc1)
    x = x + 0.5 * g1 * swiglu(h, ffn1_w_in, ffn1_w_out)
    h = modulate(rmsnorm(x, norm_mix), sh2, sc2)
    Bsz, L, _ = h.shape
    u, q, k, v, g = jnp.split(h @ w_in_mix, IN_SPLITS, axis=-1)
    y_s5, h_re, h_im = s5_scan(u, s5_lambda_re, s5_lambda_im, s5_log_dt, s5_b_re, s5_b_im,
                               s5_c_re, s5_c_im, s5_d, h0_re, h0_im)
    y_s5 = jax.nn.gelu(y_s5).astype(x.dtype)
    glu_a, glu_b = jnp.split(y_s5 @ s5_w_glu, 2, axis=-1)
    s5_out = glu_a * jax.nn.sigmoid(glu_b)
    qh = rotary(q.astype(f).reshape(Bsz, L, RET_HEADS, RET_DK), pos)
    kh = rotary(k.astype(f).reshape(Bsz, L, RET_HEADS, RET_DK), pos) * (RET_DK ** -0.5)
    vh = v.astype(f).reshape(Bsz, L, RET_HEADS, RET_DV)
    ret, s_new = retention_chunked(qh, kh, vh, s0.astype(f))
    ret_out = jax.nn.silu(g) * head_norm(ret).reshape(Bsz, L, RET_WIDTH).astype(x.dtype)
    x = x + g2 * (jnp.concatenate([s5_out, ret_out], axis=-1) @ w_out_mix)
    h = modulate(rmsnorm(x, norm_ffn2), sh3, sc3)
    x = x + 0.5 * g3 * swiglu(h, ffn2_w_in, ffn2_w_out)
    return x, h_re, h_im, s_new


def final_norm(x, c, w_ada_final, b_ada_final, norm_final):
    shift, scale = jnp.split((jax.nn.silu(c) @ w_ada_final + b_ada_final)[:, None, :], 2, axis=-1)
    return modulate(rmsnorm(x, norm_final), shift, scale)


def setup_inputs(seed: int = 0) -> dict:
    key = jax.random.key(seed)
    ks = jax.random.split(key, 40)
    f = jnp.float32

    def nrm(k, shape, scale):
        return jax.random.normal(k, shape, f) * scale

    lam_im_base = math.pi * jnp.arange(S5_STATE, dtype=f)
    return {
        'x_prompt': nrm(ks[0], (BATCH, SEQ, D_MODEL), 1.0),
        'x_sample': nrm(ks[1], (DEC_BATCH, DEC_SEQ, D_MODEL), 1.0),
        'state_ssm_re': nrm(ks[2], (DEPTH, DEC_BATCH, S5_GROUPS, S5_STATE), 0.3),
        'state_ssm_im': nrm(ks[3], (DEPTH, DEC_BATCH, S5_GROUPS, S5_STATE), 0.3),
        'state_ret': nrm(ks[4], (DEPTH, DEC_BATCH, RET_HEADS, RET_DK, RET_DV), 1.0),
        'c_prompt': nrm(ks[5], (BATCH, D_MODEL), 1.0),
        'c_sample': nrm(ks[6], (DEC_BATCH, D_MODEL), 1.0),
        'w_ada': nrm(ks[7], (DEPTH, D_MODEL, N_MOD * D_MODEL), 0.5 * D_MODEL ** -0.5),
        'b_ada': nrm(ks[8], (DEPTH, N_MOD * D_MODEL), 0.02),
        'norm_ffn1': 1.0 + nrm(ks[9], (DEPTH, D_MODEL), 0.02),
        'ffn1_w_in': nrm(ks[10], (DEPTH, D_MODEL, 2 * D_FF), D_MODEL ** -0.5),
        'ffn1_w_out': nrm(ks[11], (DEPTH, D_FF, D_MODEL), D_FF ** -0.5),
        'norm_mix': 1.0 + nrm(ks[12], (DEPTH, D_MODEL), 0.02),
        'w_in_mix': nrm(ks[13], (DEPTH, D_MODEL, IN_WIDTH), D_MODEL ** -0.5),
        's5_lambda_re': -0.5 + nrm(ks[14], (DEPTH, S5_GROUPS, S5_STATE), 0.01),
        's5_lambda_im': lam_im_base + nrm(ks[15], (DEPTH, S5_GROUPS, S5_STATE), 0.01),
        's5_log_dt': jax.random.uniform(ks[16], (DEPTH, S5_GROUPS), f, math.log(0.001), math.log(0.1)),
        's5_b_re': nrm(ks[17], (DEPTH, S5_GROUPS, S5_STATE, S5_GROUP), (2.0 * S5_GROUP) ** -0.5),
        's5_b_im': nrm(ks[18], (DEPTH, S5_GROUPS, S5_STATE, S5_GROUP), (2.0 * S5_GROUP) ** -0.5),
        's5_c_re': nrm(ks[19], (DEPTH, S5_GROUPS, S5_GROUP, S5_STATE), (2.0 * S5_STATE) ** -0.5),
        's5_c_im': nrm(ks[20], (DEPTH, S5_GROUPS, S5_GROUP, S5_STATE), (2.0 * S5_STATE) ** -0.5),
        's5_d': nrm(ks[21], (DEPTH, S5_WIDTH), 1.0),
        's5_w_glu': nrm(ks[22], (DEPTH, S5_WIDTH, 2 * S5_WIDTH), S5_WIDTH ** -0.5),
        'w_out_mix': nrm(ks[23], (DEPTH, MIX_WIDTH, D_MODEL), MIX_WIDTH ** -0.5),
        'norm_ffn2': 1.0 + nrm(ks[24], (DEPTH, D_MODEL), 0.02),
        'ffn2_w_in': nrm(ks[25], (DEPTH, D_MODEL, 2 * D_FF), D_MODEL ** -0.5),
        'ffn2_w_out': nrm(ks[26], (DEPTH, D_FF, D_MODEL), D_FF ** -0.5),
        'w_ada_final': nrm(ks[27], (D_MODEL, 2 * D_MODEL), 0.5 * D_MODEL ** -0.5),
        'b_ada_final': nrm(ks[28], (2 * D_MODEL,), 0.02),
        'norm_final': 1.0 + nrm(ks[29], (D_MODEL,), 0.02),
    }


def reference(x_prompt, x_sample, state_ssm_re, state_ssm_im, state_ret, c_prompt, c_sample,
              w_ada, b_ada, norm_ffn1, ffn1_w_in, ffn1_w_out, norm_mix, w_in_mix,
              s5_lambda_re, s5_lambda_im, s5_log_dt, s5_b_re, s5_b_im, s5_c_re, s5_c_im, s5_d, s5_w_glu,
              w_out_mix, norm_ffn2, ffn2_w_in, ffn2_w_out, w_ada_final, b_ada_final, norm_final):
    f = jnp.float32
    bp = x_prompt.shape[0]
    pos_prompt = jnp.arange(x_prompt.shape[1], dtype=f)
    pos_sample = PAST_LEN + jnp.arange(x_sample.shape[1], dtype=f)
    xp, xs = x_prompt, x_sample
    p_re, p_im, p_ret, s_re, s_im, s_ret = [], [], [], [], [], []
    for l in range(DEPTH):
        lw = (w_ada[l], b_ada[l], norm_ffn1[l], ffn1_w_in[l], ffn1_w_out[l], norm_mix[l], w_in_mix[l],
              s5_lambda_re[l], s5_lambda_im[l], s5_log_dt[l], s5_b_re[l], s5_b_im[l], s5_c_re[l], s5_c_im[l],
              s5_d[l], s5_w_glu[l], w_out_mix[l], norm_ffn2[l], ffn2_w_in[l], ffn2_w_out[l])
        zero_ssm = jnp.zeros((bp, S5_GROUPS, S5_STATE), f)
        zero_ret = jnp.zeros((bp, RET_HEADS, RET_DK, RET_DV), f)
        xp, hr, hi, sr = decoder_layer(xp, c_prompt, pos_prompt, zero_ssm, zero_ssm, zero_ret, lw)
        p_re.append(hr); p_im.append(hi); p_ret.append(sr)
        xs, hr, hi, sr = decoder_layer(xs, c_sample, pos_sample, state_ssm_re[l], state_ssm_im[l], state_ret[l], lw)
        s_re.append(hr); s_im.append(hi); s_ret.append(sr)
    y_prompt = final_norm(xp, c_prompt, w_ada_final, b_ada_final, norm_final)
    y_sample = final_norm(xs, c_sample, w_ada_final, b_ada_final, norm_final)
    return (y_prompt, y_sample, jnp.stack(p_re), jnp.stack(p_im), jnp.stack(p_ret),
            jnp.stack(s_re), jnp.stack(s_im), jnp.stack(s_ret))
```

```python
import functools
import math

import jax
import jax.numpy as jnp
from jax import lax
from jax.experimental import pallas as pl
from jax.experimental.pallas import tpu as pltpu

F32 = jnp.float32
BF16 = jnp.bfloat16

D_MODEL = 1024
D_FF = 2816
N_MOD = 9
S5_WIDTH = 512
S5_GROUP = 16
S5_GROUPS = 32
S5_STATE = 64
RET_HEADS = 4
RET_DK = 128
RET_DV = 128
RET_WIDTH = 512
RET_CHUNK = 128
PAST_LEN = 16384
ROPE_BASE = 10000.0
EPS = 1e-6

LANES = 128
SUBLANES = 8
S5_PAIRS = S5_GROUPS // 2
S5_LANE_TILES = S5_WIDTH // LANES
VMEM_LIMIT_BYTES = 56 * 1024 * 1024


def _dot(a, b):
    return jnp.dot(a, b, preferred_element_type=F32)


def _params(semantics):
    return pltpu.CompilerParams(dimension_semantics=semantics, vmem_limit_bytes=VMEM_LIMIT_BYTES)


def _resident(shape):
    nd = len(shape)
    return pl.BlockSpec(shape, lambda *_: (0,) * nd, pipeline_mode=pl.Buffered(1))


def _rms_mod(x, norm_w, shift, scale):
    xn = x * lax.rsqrt(jnp.mean(x * x, axis=-1, keepdims=True) + EPS) * norm_w
    return xn * (1.0 + scale) + shift


def _ada_kernel(c_ref, w_ref, b_ref, o_ref):
    c = c_ref[...]
    s = (c * jax.nn.sigmoid(c)).astype(BF16)
    o_ref[...] = _dot(s, w_ref[...].astype(BF16)) + b_ref[...]


def _ada(c, w, b, tn=1024):
    m, k = c.shape
    n = w.shape[1]
    return pl.pallas_call(
        _ada_kernel,
        out_shape=jax.ShapeDtypeStruct((m, n), F32),
        grid=(n // tn,),
        in_specs=[pl.BlockSpec((m, k), lambda j: (0, 0)),
                  pl.BlockSpec((k, tn), lambda j: (0, j)),
                  pl.BlockSpec((1, tn), lambda j: (0, j))],
        out_specs=pl.BlockSpec((m, tn), lambda j: (0, j)),
        compiler_params=_params(("arbitrary",)),
        name="ada_mod",
    )(c, w, b.reshape(1, n))


FF_SPLITS = ((0, 1536), (1536, 2816))


def _ffn_kernel(*refs, final):
    if final:
        x_ref, mod_ref, nw_ref, win_ref, wout_ref, fin_ref, nf_ref, o_ref = refs
    else:
        x_ref, mod_ref, nw_ref, win_ref, wout_ref, o_ref = refs
    x = x_ref[...]
    a_dim, r_dim, _ = x.shape
    m = mod_ref[...]
    shift, scale, gate = m[..., :D_MODEL], m[..., D_MODEL:2 * D_MODEL], m[..., 2 * D_MODEL:]
    h = _rms_mod(x, nw_ref[...], shift, scale)
    hb = h.reshape(a_dim * r_dim, D_MODEL).astype(BF16)
    o = None
    for lo, hi in FF_SPLITS:
        a = _dot(hb, win_ref[:, lo:hi])
        b = _dot(hb, win_ref[:, D_FF + lo:D_FF + hi])
        act = (a * jax.nn.sigmoid(a) * b).astype(BF16)
        part = _dot(act, wout_ref[lo:hi, :])
        o = part if o is None else o + part
    y = x + (0.5 * gate) * o.reshape(a_dim, r_dim, D_MODEL)
    if final:
        f = fin_ref[...]
        y = _rms_mod(y, nf_ref[...], f[..., :D_MODEL], f[..., D_MODEL:])
    o_ref[...] = y


def _ffn(x3, mod3, sub, norm_w, w_in, w_out, blk, fin3=None, norm_f=None):
    at, rt, _ = x3.shape
    a_blk, r_blk = blk
    grid = (at // a_blk, rt // r_blk)
    bm, rm, _ = mod3.shape
    per_batch = bm > 1
    mod_idx = (lambda i, j: (i, 0, sub)) if per_batch else (lambda i, j: (0, 0, sub))
    fin_idx = (lambda i, j: (i, 0, 0)) if per_batch else (lambda i, j: (0, 0, 0))
    final = fin3 is not None
    in_specs = [pl.BlockSpec((a_blk, r_blk, D_MODEL), lambda i, j: (i, j, 0)),
                pl.BlockSpec((1, rm, 3 * D_MODEL), mod_idx),
                _resident((1, D_MODEL)),
                _resident((D_MODEL, 2 * D_FF)),
                _resident((D_FF, D_MODEL))]
    args = [x3, mod3, norm_w.reshape(1, D_MODEL), w_in, w_out]
    if final:
        in_specs += [pl.BlockSpec((1, rm, 2 * D_MODEL), fin_idx), _resident((1, D_MODEL))]
        args += [fin3, norm_f.reshape(1, D_MODEL)]
    return pl.pallas_call(
        functools.partial(_ffn_kernel, final=final),
        out_shape=jax.ShapeDtypeStruct(x3.shape, F32),
        grid=grid,
        in_specs=in_specs,
        out_specs=pl.BlockSpec((a_blk, r_blk, D_MODEL), lambda i, j: (i, j, 0)),
        compiler_params=_params(("arbitrary", "arbitrary")),
        name="ffn_final" if final else "ffn",
    )(*args)


def _s5_tables(lam_re, lam_im, log_dt, b_re, b_im, c_re, c_im):
    lam = lax.complex(lam_re.astype(F32), lam_im.astype(F32))
    dt = jnp.exp(log_dt.astype(F32))[:, None]
    lam_bar = jnp.exp(lam * dt)
    b_bar = ((lam_bar - 1.0) / lam)[:, :, None] * lax.complex(b_re.astype(F32), b_im.astype(F32))

    def lanes(t):
        return t.reshape(S5_PAIRS, 2 * S5_STATE)

    lam_t = jnp.stack([lanes(lam_bar.real), lanes(lam_bar.imag)])
    lam_t = jnp.broadcast_to(lam_t[:, :, None, :], (2, S5_PAIRS, SUBLANES, LANES))

    groups_per_tile = LANES // S5_GROUP
    bt = jnp.transpose(b_bar, (0, 2, 1))
    sel_in = (jnp.arange(groups_per_tile)[None, :, None]
              == (2 * (jnp.arange(S5_PAIRS) % 4)[:, None, None] + jnp.arange(2)[None, None, :]))
    bte = bt.reshape(S5_PAIRS, 2, S5_GROUP, S5_STATE)

    def build_bw(part):
        full = jnp.einsum('jqe,jecn->jqcen', sel_in.astype(F32), part)
        return full.reshape(S5_PAIRS, LANES, 2 * S5_STATE)

    bw = jnp.concatenate([build_bw(bte.real), build_bw(bte.imag)], axis=-1).astype(BF16)

    ce_re = jnp.transpose(c_re.astype(F32), (0, 2, 1)).reshape(S5_PAIRS, 2, S5_STATE, S5_GROUP)
    ce_im = jnp.transpose(c_im.astype(F32), (0, 2, 1)).reshape(S5_PAIRS, 2, S5_STATE, S5_GROUP)

    def build_cw(part):
        full = jnp.einsum('jqe,jenc->jenqc', sel_in.astype(F32), part)
        return full.reshape(S5_PAIRS, 2 * S5_STATE, LANES)

    cw = jnp.concatenate([build_cw(ce_re), build_cw(-ce_im)], axis=1).astype(BF16)
    return lam_t, bw, cw


def _s5_input(u_tile, bw_ref, j):
    bu = _dot(u_tile.astype(BF16), bw_ref[j])
    return bu[:, :LANES], bu[:, LANES:]


def _s5_readout(u_tiles, xre, xim, cw_ref, d_ref, wglu_ref):
    ys = []
    for jt in range(S5_LANE_TILES):
        acc = d_ref[:, jt * LANES:(jt + 1) * LANES] * u_tiles(jt)
        for jj in range(4):
            j = 4 * jt + jj
            xc = jnp.concatenate([xre(j), xim(j)], axis=-1).astype(BF16)
            acc = acc + _dot(xc, cw_ref[j])
        ys.append(jax.nn.gelu(acc).astype(BF16))
    glu = _dot(jnp.concatenate(ys, axis=-1), wglu_ref[...])
    return glu[:, :S5_WIDTH] * jax.nn.sigmoid(glu[:, S5_WIDTH:])


def _s5_prompt_kernel(x_ref, mod_ref, nw_ref, wu_ref, bw_ref, lam_ref, cw_ref, d_ref, wglu_ref,
                      o_ref, hre_ref, him_ref,
                      u_slab, xre, xim, st_re, st_im, y_slab, *, tl, nb):
    i = pl.program_id(0)

    @pl.when(i == 0)
    def _():
        st_re[...] = jnp.zeros_like(st_re)
        st_im[...] = jnp.zeros_like(st_im)

    x = x_ref[...]
    m = mod_ref[...]
    h = _rms_mod(x, nw_ref[...], m[..., :D_MODEL], m[..., D_MODEL:2 * D_MODEL])
    u = _dot(h.reshape(nb * tl, D_MODEL).astype(BF16), wu_ref[...])

    for s in range(S5_LANE_TILES):
        for b in range(nb):
            u_slab[s, pl.ds(b, tl, stride=nb), :] = u[b * tl:(b + 1) * tl, s * LANES:(s + 1) * LANES]

    for j in range(S5_PAIRS):
        re, im = _s5_input(u_slab[j // 4], bw_ref, j)
        xre[j] = re
        xim[j] = im

    pairs_per_pass = 8
    for j0 in range(0, S5_PAIRS, pairs_per_pass):
        js = range(j0, j0 + pairs_per_pass)
        lr = [lam_ref[0, j] for j in js]
        li = [lam_ref[1, j] for j in js]

        def step(t, carry, js=js, lr=lr, li=li):
            r0 = pl.multiple_of(t * nb, SUBLANES)
            out = []
            for n, j in enumerate(js):
                re, im = carry[2 * n], carry[2 * n + 1]
                nre = lr[n] * re - li[n] * im + xre[j, pl.ds(r0, nb), :]
                nim = lr[n] * im + li[n] * re + xim[j, pl.ds(r0, nb), :]
                xre[j, pl.ds(r0, nb), :] = nre
                xim[j, pl.ds(r0, nb), :] = nim
                out += [nre, nim]
            return tuple(out)

        carry0 = []
        for j in js:
            carry0 += [st_re[j], st_im[j]]
        carry = lax.fori_loop(0, tl, step, tuple(carry0), unroll=2)
        for n, j in enumerate(js):
            st_re[j] = carry[2 * n]
            st_im[j] = carry[2 * n + 1]

    s5o = _s5_readout(lambda jt: u_slab[jt], lambda j: xre[j], lambda j: xim[j],
                      cw_ref, d_ref, wglu_ref)
    for s in range(S5_LANE_TILES):
        y_slab[s] = s5o[:, s * LANES:(s + 1) * LANES]
    for b in range(nb):
        for s in range(S5_LANE_TILES):
            c0 = b * S5_WIDTH + s * LANES
            o_ref[:, c0:c0 + LANES] = y_slab[s, pl.ds(b, tl, stride=nb), :].astype(BF16)

    @pl.when(i == pl.num_programs(0) - 1)
    def _():
        hre_ref[...] = jnp.concatenate([st_re[j] for j in range(S5_PAIRS)], axis=-1)
        him_ref[...] = jnp.concatenate([st_im[j] for j in range(S5_PAIRS)], axis=-1)


def _s5_prompt(x1, mod3, norm_w, w_u, bw, lam_t, cw, d_skip, w_glu, tl=64):
    nb, seq, _ = x1.shape
    rows = nb * tl
    return pl.pallas_call(
        functools.partial(_s5_prompt_kernel, tl=tl, nb=nb),
        out_shape=(jax.ShapeDtypeStruct((seq, nb * S5_WIDTH), BF16),
                   jax.ShapeDtypeStruct((nb, S5_GROUPS * S5_STATE), F32),
                   jax.ShapeDtypeStruct((nb, S5_GROUPS * S5_STATE), F32)),
        grid=(seq // tl,),
        in_specs=[pl.BlockSpec((nb, tl, D_MODEL), lambda i: (0, i, 0)),
                  pl.BlockSpec((nb, 1, 3 * D_MODEL), lambda i: (0, 0, 1)),
                  _resident((1, D_MODEL)),
                  _resident((D_MODEL, S5_WIDTH)),
                  _resident((S5_PAIRS, LANES, 2 * LANES)),
                  _resident((2, S5_PAIRS, SUBLANES, LANES)),
                  _resident((S5_PAIRS, 2 * LANES, LANES)),
                  _resident((1, S5_WIDTH)),
                  _resident((S5_WIDTH, 2 * S5_WIDTH))],
        out_specs=(pl.BlockSpec((tl, nb * S5_WIDTH), lambda i: (i, 0)),
                   pl.BlockSpec((nb, S5_GROUPS * S5_STATE), lambda i: (0, 0)),
                   pl.BlockSpec((nb, S5_GROUPS * S5_STATE), lambda i: (0, 0))),
        scratch_shapes=[pltpu.VMEM((S5_LANE_TILES, rows, LANES), F32),
                        pltpu.VMEM((S5_PAIRS, rows, LANES), F32),
                        pltpu.VMEM((S5_PAIRS, rows, LANES), F32),
                        pltpu.VMEM((S5_PAIRS, nb, LANES), F32),
                        pltpu.VMEM((S5_PAIRS, nb, LANES), F32),
                        pltpu.VMEM((S5_LANE_TILES, rows, LANES), F32)],
        compiler_params=_params(("arbitrary",)),
        name="s5_prompt",
    )(x1, mod3, norm_w.reshape(1, D_MODEL), w_u, bw, lam_t, cw, d_skip.reshape(1, S5_WIDTH), w_glu)


def _rotary_tables(pos):
    half = RET_DK // 2
    inv = ROPE_BASE ** (-jnp.arange(half, dtype=F32) / half)
    ang = pos[:, None] * inv[None, :]
    cos, sin = jnp.cos(ang), jnp.sin(ang)
    return jnp.concatenate([cos, cos], axis=-1), jnp.concatenate([-sin, sin], axis=-1)


def _decay_tables(chunk, rows):
    lg = jnp.log1p(-jnp.exp2(-5.0 - jnp.arange(RET_HEADS, dtype=F32)))
    idx = jnp.arange(rows, dtype=F32)
    valid = idx < chunk
    diff = idx[:, None] - idx[None, :]
    intra = jnp.where((diff >= 0) & valid[:, None] & valid[None, :],
                      jnp.exp(lg[:, None, None] * jnp.maximum(diff, 0.0)), 0.0)
    q_dec = jnp.where(valid[None, :], jnp.exp(lg[:, None] * (idx[None, :] + 1.0)), 0.0)
    k_dec = jnp.where(valid[None, :], jnp.exp(lg[:, None] * (chunk - 1.0 - idx)[None, :]), 0.0)
    c_dec = jnp.exp(lg * chunk)
    q_dec = jnp.broadcast_to(q_dec[:, :, None], (RET_HEADS, rows, RET_DV))
    k_dec = jnp.broadcast_to(k_dec[:, :, None], (RET_HEADS, rows, RET_DK))
    c_dec = jnp.broadcast_to(c_dec[:, None, None], (RET_HEADS, 1, RET_DV))
    return intra, q_dec, k_dec, c_dec


def _rotary(x, cs, sn):
    return x * cs + pltpu.roll(x, RET_DK // 2, axis=1) * sn


def _retention_chunk(q, k, v, s, intra, q_dec, k_dec, c_dec):
    qb, kb, vb = q.astype(BF16), k.astype(BF16), v.astype(BF16)
    scores = lax.dot_general(qb, kb, (((1,), (1,)), ((), ())), preferred_element_type=F32) * intra
    inner = _dot(scores.astype(BF16), vb)
    cross = _dot(qb, s.astype(BF16)) * q_dec
    kd_t = jnp.transpose(k * k_dec).astype(BF16)
    s_new = s * c_dec + _dot(kd_t, vb)
    return inner + cross, s_new


def _head_norm_gate(ret, g):
    mu = jnp.mean(ret, axis=-1, keepdims=True)
    cen = ret - mu
    var = jnp.mean(cen * cen, axis=-1, keepdims=True)
    return (g * jax.nn.sigmoid(g)) * (cen * lax.rsqrt(var + EPS))


def _ret_prompt_kernel(x_ref, mod_ref, nw_ref, wq_ref, cs_ref, sn_ref, intra_ref, qd_ref, kd_ref, cd_ref,
                       s5o_ref, wout_ref, o_ref, sout_ref, s_scr, *, tm):
    t = pl.program_id(1)

    @pl.when(t == 0)
    def _():
        s_scr[...] = jnp.zeros_like(s_scr)

    x = x_ref[0]
    m = mod_ref[0]
    h = _rms_mod(x, nw_ref[...], m[:, :D_MODEL], m[:, D_MODEL:2 * D_MODEL])
    proj = _dot(h.astype(BF16), wq_ref[...])
    cs, sn = cs_ref[...], sn_ref[...]
    parts = [s5o_ref[...]]
    for hd in range(RET_HEADS):
        lo = hd * RET_DK
        q = _rotary(proj[:, lo:lo + RET_DK], cs, sn)
        k = _rotary(proj[:, RET_WIDTH + lo:RET_WIDTH + lo + RET_DK], cs, sn) * (RET_DK ** -0.5)
        v = proj[:, 2 * RET_WIDTH + lo:2 * RET_WIDTH + lo + RET_DV]
        g = proj[:, 3 * RET_WIDTH + lo:3 * RET_WIDTH + lo + RET_DV]
        s = s_scr[hd]
        outs = []
        for c in range(tm // RET_CHUNK):
            r0 = c * RET_CHUNK
            o_c, s = _retention_chunk(q[r0:r0 + RET_CHUNK], k[r0:r0 + RET_CHUNK], v[r0:r0 + RET_CHUNK], s,
                                      intra_ref[hd], qd_ref[hd], kd_ref[hd], cd_ref[hd])
            outs.append(o_c)
        s_scr[hd] = s
        parts.append(_head_norm_gate(jnp.concatenate(outs, axis=0), g).astype(BF16))
    mix = jnp.concatenate(parts, axis=-1)
    o_ref[0] = x + m[:, 2 * D_MODEL:] * _dot(mix, wout_ref[...])

    @pl.when(t == pl.num_programs(1) - 1)
    def _():
        sout_ref[0] = s_scr[...]


def _ret_prompt(x1, mod3, norm_w, w_qkvg, cs, sn, decay, s5o, w_out, tm=512):
    nb, seq, _ = x1.shape
    intra, q_dec, k_dec, c_dec = decay
    return pl.pallas_call(
        functools.partial(_ret_prompt_kernel, tm=tm),
        out_shape=(jax.ShapeDtypeStruct(x1.shape, F32),
                   jax.ShapeDtypeStruct((nb, RET_HEADS, RET_DK, RET_DV), F32)),
        grid=(nb, seq // tm),
        in_specs=[pl.BlockSpec((1, tm, D_MODEL), lambda b, t: (b, t, 0)),
                  pl.BlockSpec((1, 1, 3 * D_MODEL), lambda b, t: (b, 0, 1)),
                  _resident((1, D_MODEL)),
                  _resident((D_MODEL, 4 * RET_WIDTH)),
                  pl.BlockSpec((tm, RET_DK), lambda b, t: (t, 0)),
                  pl.BlockSpec((tm, RET_DK), lambda b, t: (t, 0)),
                  _resident(intra.shape), _resident(q_dec.shape), _resident(k_dec.shape), _resident(c_dec.shape),
                  pl.BlockSpec((tm, S5_WIDTH), lambda b, t: (t, b)),
                  _resident((D_MODEL, D_MODEL))],
        out_specs=(pl.BlockSpec((1, tm, D_MODEL), lambda b, t: (b, t, 0)),
                   pl.BlockSpec((1, RET_HEADS, RET_DK, RET_DV), lambda b, t: (b, 0, 0, 0))),
        scratch_shapes=[pltpu.VMEM((RET_HEADS, RET_DK, RET_DV), F32)],
        compiler_params=_params(("arbitrary", "arbitrary")),
        name="ret_prompt",
    )(x1, mod3, norm_w.reshape(1, D_MODEL), w_qkvg, cs, sn, intra, q_dec, k_dec, c_dec, s5o, w_out)


def _mix_in_sample_kernel(x_ref, mod_ref, nw_ref, win_ref, bw_ref, lam_ref, cw_ref, d_ref, wglu_ref,
                          h0re_ref, h0im_ref, qkvg_ref, s5o_ref, hre_ref, him_ref, xre, xim, *, steps, nb):
    x = x_ref[...]
    m = mod_ref[...]
    h = _rms_mod(x, nw_ref[...], m[..., :D_MODEL], m[..., D_MODEL:2 * D_MODEL])
    proj = _dot(h.reshape(steps * nb, D_MODEL).astype(BF16), win_ref[...])
    qkvg_ref[...] = proj[:, S5_WIDTH:]
    u = proj[:, :S5_WIDTH]
    for j in range(S5_PAIRS):
        jt = j // 4
        re, im = _s5_input(u[:, jt * LANES:(jt + 1) * LANES], bw_ref, j)
        lr = lam_ref[0, j][:1]
        li = lam_ref[1, j][:1]
        sre = h0re_ref[:, j * LANES:(j + 1) * LANES]
        sim = h0im_ref[:, j * LANES:(j + 1) * LANES]
        for t in range(steps):
            nre = lr * sre - li * sim + re[t * nb:(t + 1) * nb]
            nim = lr * sim + li * sre + im[t * nb:(t + 1) * nb]
            xre[j, t * nb:(t + 1) * nb, :] = nre
            xim[j, t * nb:(t + 1) * nb, :] = nim
            sre, sim = nre, nim
        hre_ref[:, j * LANES:(j + 1) * LANES] = sre
        him_ref[:, j * LANES:(j + 1) * LANES] = sim
    s5o = _s5_readout(lambda jt: u[:, jt * LANES:(jt + 1) * LANES], lambda j: xre[j], lambda j: xim[j],
                      cw_ref, d_ref, wglu_ref)
    s5o_ref[...] = s5o.astype(BF16)


def _mix_in_sample(x1, mod3, norm_w, w_in, bw, lam_t, cw, d_skip, w_glu, h0_re, h0_im):
    steps, nb, _ = x1.shape
    rows = steps * nb
    n_state = S5_GROUPS * S5_STATE
    in_width = w_in.shape[1]
    args = (x1, mod3, norm_w.reshape(1, D_MODEL), w_in, bw, lam_t, cw, d_skip.reshape(1, S5_WIDTH), w_glu,
            h0_re, h0_im)
    in_specs = [pl.BlockSpec((steps, nb, D_MODEL), lambda i: (0, 0, 0)),
                pl.BlockSpec((1, nb, 3 * D_MODEL), lambda i: (0, 0, 1))]
    in_specs += [_resident(a.shape) for a in args[2:]]
    return pl.pallas_call(
        functools.partial(_mix_in_sample_kernel, steps=steps, nb=nb),
        out_shape=(jax.ShapeDtypeStruct((rows, in_width - S5_WIDTH), F32),
                   jax.ShapeDtypeStruct((rows, S5_WIDTH), BF16),
                   jax.ShapeDtypeStruct((nb, n_state), F32),
                   jax.ShapeDtypeStruct((nb, n_state), F32)),
        grid=(1,),
        in_specs=in_specs,
        out_specs=(pl.BlockSpec((rows, in_width - S5_WIDTH), lambda i: (0, 0)),
                   pl.BlockSpec((rows, S5_WIDTH), lambda i: (0, 0)),
                   pl.BlockSpec((nb, n_state), lambda i: (0, 0)),
                   pl.BlockSpec((nb, n_state), lambda i: (0, 0))),
        scratch_shapes=[pltpu.VMEM((S5_PAIRS, rows, LANES), F32),
                        pltpu.VMEM((S5_PAIRS, rows, LANES), F32)],
        compiler_params=_params(("arbitrary",)),
        name="mix_in_sample",
    )(*args)


def _ret_sample_kernel(q_ref, k_ref, v_ref, g_ref, s_ref, cs_ref, sn_ref, intra_ref, qd_ref, kd_ref, cd_ref,
                       o_ref, sout_ref, *, bb):
    cs, sn = cs_ref[...], sn_ref[...]
    for b in range(bb):
        for hd in range(RET_HEADS):
            q = _rotary(q_ref[b, hd], cs, sn)
            k = _rotary(k_ref[b, hd], cs, sn) * (RET_DK ** -0.5)
            o, s_new = _retention_chunk(q, k, v_ref[b, hd], s_ref[b, hd],
                                        intra_ref[hd], qd_ref[hd], kd_ref[hd], cd_ref[hd])
            sout_ref[b, hd] = s_new
            o_ref[b, hd] = _head_norm_gate(o, g_ref[b, hd])


def _ret_sample(q, k, v, g, s0, cs, sn, decay, bb=8):
    nb, _, rows, _ = q.shape
    intra, q_dec, k_dec, c_dec = decay
    tok = pl.BlockSpec((bb, RET_HEADS, rows, RET_DK), lambda i: (i, 0, 0, 0))
    st = pl.BlockSpec((bb, RET_HEADS, RET_DK, RET_DV), lambda i: (i, 0, 0, 0))
    return pl.pallas_call(
        functools.partial(_ret_sample_kernel, bb=bb),
        out_shape=(jax.ShapeDtypeStruct(q.shape, F32), jax.ShapeDtypeStruct(s0.shape, F32)),
        grid=(nb // bb,),
        in_specs=[tok, tok, tok, tok, st, _resident(cs.shape), _resident(sn.shape),
                  _resident(intra.shape), _resident(q_dec.shape), _resident(k_dec.shape), _resident(c_dec.shape)],
        out_specs=(tok, st),
        compiler_params=_params(("arbitrary",)),
        name="ret_sample",
    )(q, k, v, g, s0, cs, sn, intra, q_dec, k_dec, c_dec)


def _mix_out_sample_kernel(x_ref, mod_ref, s5o_ref, ret_ref, wout_ref, o_ref, *, steps, nb):
    x = x_ref[...]
    gate = mod_ref[...][..., 2 * D_MODEL:]
    mix = jnp.concatenate([s5o_ref[...], ret_ref[...].astype(BF16)], axis=-1)
    o_ref[...] = x + gate * _dot(mix, wout_ref[...]).reshape(steps, nb, D_MODEL)


def _mix_out_sample(x1, mod3, s5o, ret, w_out):
    steps, nb, _ = x1.shape
    rows = steps * nb
    return pl.pallas_call(
        functools.partial(_mix_out_sample_kernel, steps=steps, nb=nb),
        out_shape=jax.ShapeDtypeStruct(x1.shape, F32),
        grid=(1,),
        in_specs=[pl.BlockSpec((steps, nb, D_MODEL), lambda i: (0, 0, 0)),
                  pl.BlockSpec((1, nb, 3 * D_MODEL), lambda i: (0, 0, 1)),
                  pl.BlockSpec((rows, S5_WIDTH), lambda i: (0, 0)),
                  pl.BlockSpec((rows, RET_WIDTH), lambda i: (0, 0)),
                  _resident((D_MODEL, D_MODEL))],
        out_specs=pl.BlockSpec((steps, nb, D_MODEL), lambda i: (0, 0, 0)),
        compiler_params=_params(("arbitrary",)),
        name="mix_out_sample",
    )(x1, mod3, s5o, ret, w_out)


def kernel(x_prompt, x_sample, state_ssm_re, state_ssm_im, state_ret, c_prompt, c_sample,
           w_ada, b_ada, norm_ffn1, ffn1_w_in, ffn1_w_out, norm_mix, w_in_mix,
           s5_lambda_re, s5_lambda_im, s5_log_dt, s5_b_re, s5_b_im, s5_c_re, s5_c_im, s5_d, s5_w_glu,
           w_out_mix, norm_ffn2, ffn2_w_in, ffn2_w_out, w_ada_final, b_ada_final, norm_final):
    depth = w_ada.shape[0]
    bp, seq, _ = x_prompt.shape
    bs, steps, _ = x_sample.shape

    c_rows = bp + bs
    c_pad = -c_rows % 16
    c_all = jnp.concatenate([c_prompt, c_sample, jnp.zeros((c_pad, D_MODEL), F32)], axis=0)

    pos_p = jnp.arange(seq, dtype=F32)
    pos_s = PAST_LEN + jnp.arange(steps, dtype=F32)
    cs_p, sn_p = _rotary_tables(pos_p)
    chunk_p = RET_CHUNK if seq % RET_CHUNK == 0 else seq
    decay_p = _decay_tables(chunk_p, chunk_p)
    rows_s = -(-steps // SUBLANES) * SUBLANES
    cs_s, sn_s = _rotary_tables(jnp.concatenate([pos_s, jnp.zeros((rows_s - steps,), F32)]))
    decay_s = _decay_tables(steps, rows_s)

    xp = x_prompt
    xs = jnp.transpose(x_sample, (1, 0, 2))
    outs = {k: [] for k in ("p_re", "p_im", "p_ret", "s_re", "s_im", "s_ret")}
    for l in range(depth):
        mod = _ada(c_all, w_ada[l], b_ada[l])
        mod_p = mod[:bp][:, None, :]
        mod_s = mod[bp:c_rows][None]
        w1_in, w1_out = ffn1_w_in[l].astype(BF16), ffn1_w_out[l].astype(BF16)
        w2_in, w2_out = ffn2_w_in[l].astype(BF16), ffn2_w_out[l].astype(BF16)
        w_mix = w_in_mix[l].astype(BF16)
        w_glu = s5_w_glu[l].astype(BF16)
        w_out = w_out_mix[l].astype(BF16)
        lam_t, bw, cw = _s5_tables(s5_lambda_re[l], s5_lambda_im[l], s5_log_dt[l],
                                   s5_b_re[l], s5_b_im[l], s5_c_re[l], s5_c_im[l])
        last = l == depth - 1
        if last:
            fin = _ada(c_all, w_ada_final, b_ada_final)
            fin_p, fin_s = fin[:bp][:, None, :], fin[bp:c_rows][None]
        else:
            fin_p = fin_s = None

        xp = _ffn(xp, mod_p, 0, norm_ffn1[l], w1_in, w1_out, (1, 512))
        s5o, hre, him = _s5_prompt(xp, mod_p, norm_mix[l], w_mix[:, :S5_WIDTH], bw, lam_t, cw, s5_d[l], w_glu)
        xp, sret = _ret_prompt(xp, mod_p, norm_mix[l], w_mix[:, S5_WIDTH:], cs_p, sn_p, decay_p, s5o, w_out)
        xp = _ffn(xp, mod_p, 2, norm_ffn2[l], w2_in, w2_out, (1, 512),
                  fin3=fin_p, norm_f=norm_final if last else None)
        outs["p_re"].append(hre.reshape(bp, S5_GROUPS, S5_STATE))
        outs["p_im"].append(him.reshape(bp, S5_GROUPS, S5_STATE))
        outs["p_ret"].append(sret)

        xs = _ffn(xs, mod_s, 0, norm_ffn1[l], w1_in, w1_out, (steps, bs))
        qkvg, s5o_s, hre_s, him_s = _mix_in_sample(
            xs, mod_s, norm_mix[l], w_mix, bw, lam_t, cw, s5_d[l], w_glu,
            state_ssm_re[l].reshape(bs, -1), state_ssm_im[l].reshape(bs, -1))
        qkvg = jnp.transpose(qkvg.reshape(steps, bs, 4, RET_HEADS, RET_DK), (2, 1, 3, 0, 4))
        qkvg = jnp.pad(qkvg, ((0, 0), (0, 0), (0, 0), (0, rows_s - steps), (0, 0)))
        ret_s, sret_s = _ret_sample(qkvg[0], qkvg[1], qkvg[2], qkvg[3], state_ret[l], cs_s, sn_s, decay_s)
        ret_s = jnp.transpose(ret_s[:, :, :steps, :], (2, 0, 1, 3)).reshape(steps * bs, RET_WIDTH)
        xs = _mix_out_sample(xs, mod_s, s5o_s, ret_s, w_out)
        xs = _ffn(xs, mod_s, 2, norm_ffn2[l], w2_in, w2_out, (steps, bs),
                  fin3=fin_s, norm_f=norm_final if last else None)
        outs["s_re"].append(hre_s.reshape(bs, S5_GROUPS, S5_STATE))
        outs["s_im"].append(him_s.reshape(bs, S5_GROUPS, S5_STATE))
        outs["s_ret"].append(sret_s)

    y_prompt = xp
    y_sample = jnp.transpose(xs, (1, 0, 2))
    return (y_prompt, y_sample, jnp.stack(outs["p_re"]), jnp.stack(outs["p_im"]), jnp.stack(outs["p_ret"]),
            jnp.stack(outs["s_re"]), jnp.stack(outs["s_im"]), jnp.stack(outs["s_ret"]))
```

```python
import functools

import numpy as np
import jax
import jax.numpy as jnp
from jax import lax
from jax.experimental import pallas as pl
from jax.experimental.pallas import tpu as pltpu

F32 = jnp.float32
BF16 = jnp.bfloat16

D_MODEL = 1024
D_FF = 2816
N_MOD = 9
S5_WIDTH = 512
S5_GROUP = 16
S5_GROUPS = 32
S5_STATE = 64
RET_HEADS = 4
RET_DK = 128
RET_DV = 128
RET_WIDTH = 512
RET_CHUNK = 128
PAST_LEN = 16384
ROPE_BASE = 10000.0
EPS = 1e-6

LANES = 128
SUBLANES = 8
S5_PAIRS = S5_GROUPS // 2
S5_LANE_TILES = S5_WIDTH // LANES
VMEM_LIMIT_BYTES = 56 * 1024 * 1024


def _dot(a, b):
    return jnp.dot(a, b, preferred_element_type=F32)


def _params(semantics):
    return pltpu.CompilerParams(dimension_semantics=semantics, vmem_limit_bytes=VMEM_LIMIT_BYTES)


def _resident(shape):
    nd = len(shape)
    return pl.BlockSpec(shape, lambda *_: (0,) * nd, pipeline_mode=pl.Buffered(1))


def _rms_mod(x, norm_w, shift, scale):
    xn = x * lax.rsqrt(jnp.mean(x * x, axis=-1, keepdims=True) + EPS) * norm_w
    return xn * (1.0 + scale) + shift


def _ada_kernel(c_ref, w_ref, b_ref, o_ref):
    c = c_ref[...]
    s = (c * jax.nn.sigmoid(c)).astype(BF16)
    o_ref[...] = _dot(s, w_ref[...].astype(BF16)) + b_ref[...]


def _ada(c, w, b, tn=1024):
    m, k = c.shape
    n = w.shape[1]
    return pl.pallas_call(
        _ada_kernel,
        out_shape=jax.ShapeDtypeStruct((m, n), F32),
        grid=(n // tn,),
        in_specs=[pl.BlockSpec((m, k), lambda j: (0, 0)),
                  pl.BlockSpec((k, tn), lambda j: (0, j)),
                  pl.BlockSpec((1, tn), lambda j: (0, j))],
        out_specs=pl.BlockSpec((m, tn), lambda j: (0, j)),
        compiler_params=_params(("arbitrary",)),
        name="ada_mod",
    )(c, w, b.reshape(1, n))


FF_SPLITS = ((0, 1024), (1024, 2048), (2048, 2816))
FFN_ROWS = 1024


def _ffn_kernel(*refs, final):
    if final:
        x_ref, mod_ref, nw_ref, win_ref, wout_ref, fin_ref, nf_ref, o_ref = refs
    else:
        x_ref, mod_ref, nw_ref, win_ref, wout_ref, o_ref = refs
    x = x_ref[...]
    a_dim, r_dim, _ = x.shape
    m = mod_ref[...]
    shift, scale, gate = m[..., :D_MODEL], m[..., D_MODEL:2 * D_MODEL], m[..., 2 * D_MODEL:]
    h = _rms_mod(x, nw_ref[...], shift, scale)
    hb = h.reshape(a_dim * r_dim, D_MODEL).astype(BF16)
    o = None
    for lo, hi in FF_SPLITS:
        a = _dot(hb, win_ref[:, lo:hi])
        b = _dot(hb, win_ref[:, D_FF + lo:D_FF + hi])
        act = (a * jax.nn.sigmoid(a) * b).astype(BF16)
        part = _dot(act, wout_ref[lo:hi, :])
        o = part if o is None else o + part
    y = x + (0.5 * gate) * o.reshape(a_dim, r_dim, D_MODEL)
    if final:
        f = fin_ref[...]
        y = _rms_mod(y, nf_ref[...], f[..., :D_MODEL], f[..., D_MODEL:])
    o_ref[...] = y


def _ffn(x3, mod3, sub, norm_w, w_in, w_out, blk, fin3=None, norm_f=None):
    at, rt, _ = x3.shape
    a_blk, r_blk = blk
    grid = (at // a_blk, rt // r_blk)
    bm, rm, _ = mod3.shape
    per_batch = bm > 1
    mod_idx = (lambda i, j: (i, 0, sub)) if per_batch else (lambda i, j: (0, 0, sub))
    fin_idx = (lambda i, j: (i, 0, 0)) if per_batch else (lambda i, j: (0, 0, 0))
    final = fin3 is not None
    in_specs = [pl.BlockSpec((a_blk, r_blk, D_MODEL), lambda i, j: (i, j, 0)),
                pl.BlockSpec((1, rm, 3 * D_MODEL), mod_idx),
                _resident((1, D_MODEL)),
                _resident((D_MODEL, 2 * D_FF)),
                _resident((D_FF, D_MODEL))]
    args = [x3, mod3, norm_w.reshape(1, D_MODEL), w_in, w_out]
    if final:
        in_specs += [pl.BlockSpec((1, rm, 2 * D_MODEL), fin_idx), _resident((1, D_MODEL))]
        args += [fin3, norm_f.reshape(1, D_MODEL)]
    return pl.pallas_call(
        functools.partial(_ffn_kernel, final=final),
        out_shape=jax.ShapeDtypeStruct(x3.shape, F32),
        grid=grid,
        in_specs=in_specs,
        out_specs=pl.BlockSpec((a_blk, r_blk, D_MODEL), lambda i, j: (i, j, 0)),
        compiler_params=_params(("arbitrary", "arbitrary")),
        name="ffn_final" if final else "ffn",
    )(*args)


def _s5_tables(lam_re, lam_im, log_dt, b_re, b_im, c_re, c_im):
    lr, li = lam_re.astype(F32), lam_im.astype(F32)
    dt = jnp.exp(log_dt.astype(F32))[:, None]
    mag = jnp.exp(lr * dt)
    ar, ai = mag * jnp.cos(li * dt), mag * jnp.sin(li * dt)
    den = lr * lr + li * li
    cr = ((ar - 1.0) * lr + ai * li) / den
    ci = (ai * lr - (ar - 1.0) * li) / den
    bb_re = cr[:, :, None] * b_re.astype(F32) - ci[:, :, None] * b_im.astype(F32)
    bb_im = cr[:, :, None] * b_im.astype(F32) + ci[:, :, None] * b_re.astype(F32)

    def lanes(t):
        return t.reshape(S5_PAIRS, 2 * S5_STATE)

    lam_t = jnp.stack([lanes(ar), lanes(ai)])
    lam_t = jnp.broadcast_to(lam_t[:, :, None, :], (2, S5_PAIRS, SUBLANES, LANES))

    groups_per_tile = LANES // S5_GROUP
    sel_in = (jnp.arange(groups_per_tile)[None, :, None]
              == (2 * (jnp.arange(S5_PAIRS) % 4)[:, None, None] + jnp.arange(2)[None, None, :]))

    def build_bw(part):
        part = jnp.transpose(part, (0, 2, 1)).reshape(S5_PAIRS, 2, S5_GROUP, S5_STATE)
        full = jnp.einsum('jqe,jecn->jqcen', sel_in.astype(F32), part)
        return full.reshape(S5_PAIRS, LANES, 2 * S5_STATE)

    bw = jnp.concatenate([build_bw(bb_re), build_bw(bb_im)], axis=-1).astype(BF16)

    ce_re = jnp.transpose(c_re.astype(F32), (0, 2, 1)).reshape(S5_PAIRS, 2, S5_STATE, S5_GROUP)
    ce_im = jnp.transpose(c_im.astype(F32), (0, 2, 1)).reshape(S5_PAIRS, 2, S5_STATE, S5_GROUP)

    def build_cw(part):
        full = jnp.einsum('jqe,jenc->jenqc', sel_in.astype(F32), part)
        return full.reshape(S5_PAIRS, 2 * S5_STATE, LANES)

    cw = jnp.concatenate([build_cw(ce_re), build_cw(-ce_im)], axis=1).astype(BF16)
    return lam_t, bw, cw


def _s5_input(u_tile, bw_ref, j):
    bu = _dot(u_tile.astype(BF16), bw_ref[j])
    return bu[:, :LANES], bu[:, LANES:]


def _s5_readout(u_tiles, xre, xim, cw_ref, d_ref, wglu_ref):
    ys = []
    for jt in range(S5_LANE_TILES):
        acc = d_ref[:, jt * LANES:(jt + 1) * LANES] * u_tiles(jt)
        for jj in range(4):
            j = 4 * jt + jj
            xc = jnp.concatenate([xre(j), xim(j)], axis=-1).astype(BF16)
            acc = acc + _dot(xc, cw_ref[j])
        ys.append(jax.nn.gelu(acc).astype(BF16))
    glu = _dot(jnp.concatenate(ys, axis=-1), wglu_ref[...])
    return glu[:, :S5_WIDTH] * jax.nn.sigmoid(glu[:, S5_WIDTH:])


def _s5_prompt_kernel(x_ref, mod_ref, nw_ref, wu_ref, bw_ref, lam_ref, cw_ref, d_ref, wglu_ref,
                      o_ref, hre_ref, him_ref,
                      u_slab, xre, xim, st_re, st_im, y_slab, *, tl, nb):
    i = pl.program_id(0)

    @pl.when(i == 0)
    def _():
        st_re[...] = jnp.zeros_like(st_re)
        st_im[...] = jnp.zeros_like(st_im)

    x = x_ref[...]
    m = mod_ref[...]
    h = _rms_mod(x, nw_ref[...], m[..., :D_MODEL], m[..., D_MODEL:2 * D_MODEL])
    u = _dot(h.reshape(nb * tl, D_MODEL).astype(BF16), wu_ref[...])

    for s in range(S5_LANE_TILES):
        for b in range(nb):
            u_slab[s, pl.ds(b, tl, stride=nb), :] = u[b * tl:(b + 1) * tl, s * LANES:(s + 1) * LANES]

    for j in range(S5_PAIRS):
        re, im = _s5_input(u_slab[j // 4], bw_ref, j)
        xre[j] = re
        xim[j] = im

    pairs_per_pass = 8
    for j0 in range(0, S5_PAIRS, pairs_per_pass):
        js = range(j0, j0 + pairs_per_pass)
        lr = [lam_ref[0, j] for j in js]
        li = [lam_ref[1, j] for j in js]

        carry = []
        for j in js:
            carry += [st_re[j], st_im[j]]
        for t in range(tl):
            r0 = t * nb
            for n, j in enumerate(js):
                re, im = carry[2 * n], carry[2 * n + 1]
                nre = lr[n] * re - li[n] * im + xre[j, r0:r0 + nb, :]
                nim = lr[n] * im + li[n] * re + xim[j, r0:r0 + nb, :]
                xre[j, r0:r0 + nb, :] = nre
                xim[j, r0:r0 + nb, :] = nim
                carry[2 * n], carry[2 * n + 1] = nre, nim
        for n, j in enumerate(js):
            st_re[j] = carry[2 * n]
            st_im[j] = carry[2 * n + 1]

    s5o = _s5_readout(lambda jt: u_slab[jt], lambda j: xre[j], lambda j: xim[j],
                      cw_ref, d_ref, wglu_ref)
    for s in range(S5_LANE_TILES):
        y_slab[s] = s5o[:, s * LANES:(s + 1) * LANES]
    for b in range(nb):
        for s in range(S5_LANE_TILES):
            c0 = b * S5_WIDTH + s * LANES
            o_ref[:, c0:c0 + LANES] = y_slab[s, pl.ds(b, tl, stride=nb), :].astype(BF16)

    @pl.when(i == pl.num_programs(0) - 1)
    def _():
        hre_ref[...] = jnp.concatenate([st_re[j] for j in range(S5_PAIRS)], axis=-1)
        him_ref[...] = jnp.concatenate([st_im[j] for j in range(S5_PAIRS)], axis=-1)


def _s5_prompt(x1, mod3, norm_w, w_u, bw, lam_t, cw, d_skip, w_glu, tl=64):
    nb, seq, _ = x1.shape
    rows = nb * tl
    return pl.pallas_call(
        functools.partial(_s5_prompt_kernel, tl=tl, nb=nb),
        out_shape=(jax.ShapeDtypeStruct((seq, nb * S5_WIDTH), BF16),
                   jax.ShapeDtypeStruct((nb, S5_GROUPS * S5_STATE), F32),
                   jax.ShapeDtypeStruct((nb, S5_GROUPS * S5_STATE), F32)),
        grid=(seq // tl,),
        in_specs=[pl.BlockSpec((nb, tl, D_MODEL), lambda i: (0, i, 0)),
                  pl.BlockSpec((nb, 1, 3 * D_MODEL), lambda i: (0, 0, 1)),
                  _resident((1, D_MODEL)),
                  _resident((D_MODEL, S5_WIDTH)),
                  _resident((S5_PAIRS, LANES, 2 * LANES)),
                  _resident((2, S5_PAIRS, SUBLANES, LANES)),
                  _resident((S5_PAIRS, 2 * LANES, LANES)),
                  _resident((1, S5_WIDTH)),
                  _resident((S5_WIDTH, 2 * S5_WIDTH))],
        out_specs=(pl.BlockSpec((tl, nb * S5_WIDTH), lambda i: (i, 0)),
                   pl.BlockSpec((nb, S5_GROUPS * S5_STATE), lambda i: (0, 0)),
                   pl.BlockSpec((nb, S5_GROUPS * S5_STATE), lambda i: (0, 0))),
        scratch_shapes=[pltpu.VMEM((S5_LANE_TILES, rows, LANES), F32),
                        pltpu.VMEM((S5_PAIRS, rows, LANES), F32),
                        pltpu.VMEM((S5_PAIRS, rows, LANES), F32),
                        pltpu.VMEM((S5_PAIRS, nb, LANES), F32),
                        pltpu.VMEM((S5_PAIRS, nb, LANES), F32),
                        pltpu.VMEM((S5_LANE_TILES, rows, LANES), F32)],
        compiler_params=_params(("arbitrary",)),
        name="s5_prompt",
    )(x1, mod3, norm_w.reshape(1, D_MODEL), w_u, bw, lam_t, cw, d_skip.reshape(1, S5_WIDTH), w_glu)


def _rotary_tables(pos):
    half = RET_DK // 2
    inv = ROPE_BASE ** (-np.arange(half, dtype=np.float64) / half)
    ang = np.asarray(pos, np.float64)[:, None] * inv[None, :]
    cos, sin = np.cos(ang), np.sin(ang)
    return (jnp.asarray(np.concatenate([cos, cos], axis=-1), F32),
            jnp.asarray(np.concatenate([-sin, sin], axis=-1), F32))


def _decay_tables(chunk, rows):
    lg = np.log1p(-np.exp2(-5.0 - np.arange(RET_HEADS, dtype=np.float64)))
    idx = np.arange(rows, dtype=np.float64)
    valid = idx < chunk
    diff = idx[:, None] - idx[None, :]
    intra = np.where((diff >= 0) & valid[:, None] & valid[None, :],
                     np.exp(lg[:, None, None] * np.maximum(diff, 0.0)), 0.0)
    q_dec = np.where(valid[None, :], np.exp(lg[:, None] * (idx[None, :] + 1.0)), 0.0)
    k_dec = np.where(valid[None, :], np.exp(lg[:, None] * (chunk - 1.0 - idx)[None, :]), 0.0)
    c_dec = np.exp(lg * chunk)
    q_dec = np.broadcast_to(q_dec[:, :, None], (RET_HEADS, rows, RET_DV))
    k_dec = np.broadcast_to(k_dec[:, :, None], (RET_HEADS, rows, RET_DK))
    c_dec = np.broadcast_to(c_dec[:, None, None], (RET_HEADS, 1, RET_DV))
    return tuple(jnp.asarray(t, F32) for t in (intra, q_dec, k_dec, c_dec))


def _rotary(x, cs, sn):
    return x * cs + pltpu.roll(x, RET_DK // 2, axis=1) * sn


def _retention_chunk(q, k, v, s, intra, q_dec, k_dec, c_dec):
    qb, kb, vb = q.astype(BF16), k.astype(BF16), v.astype(BF16)
    scores = lax.dot_general(qb, kb, (((1,), (1,)), ((), ())), preferred_element_type=F32) * intra
    inner = _dot(scores.astype(BF16), vb)
    cross = _dot(qb, s.astype(BF16)) * q_dec
    kd_t = jnp.transpose(k * k_dec).astype(BF16)
    s_new = s * c_dec + _dot(kd_t, vb)
    return inner + cross, s_new


def _head_norm_gate(ret, g):
    mu = jnp.mean(ret, axis=-1, keepdims=True)
    cen = ret - mu
    var = jnp.mean(cen * cen, axis=-1, keepdims=True)
    return (g * jax.nn.sigmoid(g)) * (cen * lax.rsqrt(var + EPS))


def _ret_prompt_kernel(x_ref, mod_ref, nw_ref, wq_ref, cs_ref, sn_ref, intra_ref, qd_ref, kd_ref, cd_ref,
                       s5o_ref, wout_ref, o_ref, sout_ref, s_scr, *, tm):
    t = pl.program_id(1)

    @pl.when(t == 0)
    def _():
        s_scr[...] = jnp.zeros_like(s_scr)

    x = x_ref[0]
    m = mod_ref[0]
    h = _rms_mod(x, nw_ref[...], m[:, :D_MODEL], m[:, D_MODEL:2 * D_MODEL])
    proj = _dot(h.astype(BF16), wq_ref[...])
    cs, sn = cs_ref[...], sn_ref[...]
    parts = [s5o_ref[...]]
    for hd in range(RET_HEADS):
        lo = hd * RET_DK
        q = _rotary(proj[:, lo:lo + RET_DK], cs, sn)
        k = _rotary(proj[:, RET_WIDTH + lo:RET_WIDTH + lo + RET_DK], cs, sn) * (RET_DK ** -0.5)
        v = proj[:, 2 * RET_WIDTH + lo:2 * RET_WIDTH + lo + RET_DV]
        g = proj[:, 3 * RET_WIDTH + lo:3 * RET_WIDTH + lo + RET_DV]
        s = s_scr[hd]
        outs = []
        for c in range(tm // RET_CHUNK):
            r0 = c * RET_CHUNK
            o_c, s = _retention_chunk(q[r0:r0 + RET_CHUNK], k[r0:r0 + RET_CHUNK], v[r0:r0 + RET_CHUNK], s,
                                      intra_ref[hd], qd_ref[hd], kd_ref[hd], cd_ref[hd])
            outs.append(o_c)
        s_scr[hd] = s
        parts.append(_head_norm_gate(jnp.concatenate(outs, axis=0), g).astype(BF16))
    mix = jnp.concatenate(parts, axis=-1)
    o_ref[0] = x + m[:, 2 * D_MODEL:] * _dot(mix, wout_ref[...])

    @pl.when(t == pl.num_programs(1) - 1)
    def _():
        sout_ref[0] = s_scr[...]


def _ret_prompt(x1, mod3, norm_w, w_qkvg, cs, sn, decay, s5o, w_out, tm=512):
    nb, seq, _ = x1.shape
    intra, q_dec, k_dec, c_dec = decay
    return pl.pallas_call(
        functools.partial(_ret_prompt_kernel, tm=tm),
        out_shape=(jax.ShapeDtypeStruct(x1.shape, F32),
                   jax.ShapeDtypeStruct((nb, RET_HEADS, RET_DK, RET_DV), F32)),
        grid=(nb, seq // tm),
        in_specs=[pl.BlockSpec((1, tm, D_MODEL), lambda b, t: (b, t, 0)),
                  pl.BlockSpec((1, 1, 3 * D_MODEL), lambda b, t: (b, 0, 1)),
                  _resident((1, D_MODEL)),
                  _resident((D_MODEL, 4 * RET_WIDTH)),
                  pl.BlockSpec((tm, RET_DK), lambda b, t: (t, 0)),
                  pl.BlockSpec((tm, RET_DK), lambda b, t: (t, 0)),
                  _resident(intra.shape), _resident(q_dec.shape), _resident(k_dec.shape), _resident(c_dec.shape),
                  pl.BlockSpec((tm, S5_WIDTH), lambda b, t: (t, b)),
                  _resident((D_MODEL, D_MODEL))],
        out_specs=(pl.BlockSpec((1, tm, D_MODEL), lambda b, t: (b, t, 0)),
                   pl.BlockSpec((1, RET_HEADS, RET_DK, RET_DV), lambda b, t: (b, 0, 0, 0))),
        scratch_shapes=[pltpu.VMEM((RET_HEADS, RET_DK, RET_DV), F32)],
        compiler_params=_params(("arbitrary", "arbitrary")),
        name="ret_prompt",
    )(x1, mod3, norm_w.reshape(1, D_MODEL), w_qkvg, cs, sn, intra, q_dec, k_dec, c_dec, s5o, w_out)


def _mix_in_sample_kernel(x_ref, mod_ref, nw_ref, win_ref, bw_ref, lam_ref, cw_ref, d_ref, wglu_ref,
                          h0re_ref, h0im_ref, qkvg_ref, s5o_ref, hre_ref, him_ref, xre, xim, *, steps, nb):
    x = x_ref[...]
    m = mod_ref[...]
    h = _rms_mod(x, nw_ref[...], m[..., :D_MODEL], m[..., D_MODEL:2 * D_MODEL])
    proj = _dot(h.reshape(steps * nb, D_MODEL).astype(BF16), win_ref[...])
    qkvg_ref[...] = proj[:, S5_WIDTH:]
    u = proj[:, :S5_WIDTH]
    for j in range(S5_PAIRS):
        jt = j // 4
        re, im = _s5_input(u[:, jt * LANES:(jt + 1) * LANES], bw_ref, j)
        lr = lam_ref[0, j][:1]
        li = lam_ref[1, j][:1]
        sre = h0re_ref[:, j * LANES:(j + 1) * LANES]
        sim = h0im_ref[:, j * LANES:(j + 1) * LANES]
        for t in range(steps):
            nre = lr * sre - li * sim + re[t * nb:(t + 1) * nb]
            nim = lr * sim + li * sre + im[t * nb:(t + 1) * nb]
            xre[j, t * nb:(t + 1) * nb, :] = nre
            xim[j, t * nb:(t + 1) * nb, :] = nim
            sre, sim = nre, nim
        hre_ref[:, j * LANES:(j + 1) * LANES] = sre
        him_ref[:, j * LANES:(j + 1) * LANES] = sim
    s5o = _s5_readout(lambda jt: u[:, jt * LANES:(jt + 1) * LANES], lambda j: xre[j], lambda j: xim[j],
                      cw_ref, d_ref, wglu_ref)
    s5o_ref[...] = s5o.astype(BF16)


def _mix_in_sample(x1, mod3, norm_w, w_in, bw, lam_t, cw, d_skip, w_glu, h0_re, h0_im):
    steps, nb, _ = x1.shape
    rows = steps * nb
    n_state = S5_GROUPS * S5_STATE
    in_width = w_in.shape[1]
    args = (x1, mod3, norm_w.reshape(1, D_MODEL), w_in, bw, lam_t, cw, d_skip.reshape(1, S5_WIDTH), w_glu,
            h0_re, h0_im)
    in_specs = [pl.BlockSpec((steps, nb, D_MODEL), lambda i: (0, 0, 0)),
                pl.BlockSpec((1, nb, 3 * D_MODEL), lambda i: (0, 0, 1))]
    in_specs += [_resident(a.shape) for a in args[2:]]
    return pl.pallas_call(
        functools.partial(_mix_in_sample_kernel, steps=steps, nb=nb),
        out_shape=(jax.ShapeDtypeStruct((rows, in_width - S5_WIDTH), F32),
                   jax.ShapeDtypeStruct((rows, S5_WIDTH), BF16),
                   jax.ShapeDtypeStruct((nb, n_state), F32),
                   jax.ShapeDtypeStruct((nb, n_state), F32)),
        grid=(1,),
        in_specs=in_specs,
        out_specs=(pl.BlockSpec((rows, in_width - S5_WIDTH), lambda i: (0, 0)),
                   pl.BlockSpec((rows, S5_WIDTH), lambda i: (0, 0)),
                   pl.BlockSpec((nb, n_state), lambda i: (0, 0)),
                   pl.BlockSpec((nb, n_state), lambda i: (0, 0))),
        scratch_shapes=[pltpu.VMEM((S5_PAIRS, rows, LANES), F32),
                        pltpu.VMEM((S5_PAIRS, rows, LANES), F32)],
        compiler_params=_params(("arbitrary",)),
        name="mix_in_sample",
    )(*args)


def _ret_sample_kernel(q_ref, k_ref, v_ref, g_ref, s_ref, cs_ref, sn_ref, intra_ref, qd_ref, kd_ref, cd_ref,
                       o_ref, sout_ref, *, bb):
    cs, sn = cs_ref[...], sn_ref[...]
    for b in range(bb):
        for hd in range(RET_HEADS):
            q = _rotary(q_ref[b, hd], cs, sn)
            k = _rotary(k_ref[b, hd], cs, sn) * (RET_DK ** -0.5)
            o, s_new = _retention_chunk(q, k, v_ref[b, hd], s_ref[b, hd],
                                        intra_ref[hd], qd_ref[hd], kd_ref[hd], cd_ref[hd])
            sout_ref[b, hd] = s_new
            o_ref[b, hd] = _head_norm_gate(o, g_ref[b, hd])


def _ret_sample(q, k, v, g, s0, cs, sn, decay, bb=8):
    nb, _, rows, _ = q.shape
    intra, q_dec, k_dec, c_dec = decay
    tok = pl.BlockSpec((bb, RET_HEADS, rows, RET_DK), lambda i: (i, 0, 0, 0))
    st = pl.BlockSpec((bb, RET_HEADS, RET_DK, RET_DV), lambda i: (i, 0, 0, 0))
    return pl.pallas_call(
        functools.partial(_ret_sample_kernel, bb=bb),
        out_shape=(jax.ShapeDtypeStruct(q.shape, F32), jax.ShapeDtypeStruct(s0.shape, F32)),
        grid=(nb // bb,),
        in_specs=[tok, tok, tok, tok, st, _resident(cs.shape), _resident(sn.shape),
                  _resident(intra.shape), _resident(q_dec.shape), _resident(k_dec.shape), _resident(c_dec.shape)],
        out_specs=(tok, st),
        compiler_params=_params(("arbitrary",)),
        name="ret_sample",
    )(q, k, v, g, s0, cs, sn, intra, q_dec, k_dec, c_dec)


def _mix_out_sample_kernel(x_ref, mod_ref, s5o_ref, ret_ref, wout_ref, o_ref, *, steps, nb):
    x = x_ref[...]
    gate = mod_ref[...][..., 2 * D_MODEL:]
    mix = jnp.concatenate([s5o_ref[...], ret_ref[...].astype(BF16)], axis=-1)
    o_ref[...] = x + gate * _dot(mix, wout_ref[...]).reshape(steps, nb, D_MODEL)


def _mix_out_sample(x1, mod3, s5o, ret, w_out):
    steps, nb, _ = x1.shape
    rows = steps * nb
    return pl.pallas_call(
        functools.partial(_mix_out_sample_kernel, steps=steps, nb=nb),
        out_shape=jax.ShapeDtypeStruct(x1.shape, F32),
        grid=(1,),
        in_specs=[pl.BlockSpec((steps, nb, D_MODEL), lambda i: (0, 0, 0)),
                  pl.BlockSpec((1, nb, 3 * D_MODEL), lambda i: (0, 0, 1)),
                  pl.BlockSpec((rows, S5_WIDTH), lambda i: (0, 0)),
                  pl.BlockSpec((rows, RET_WIDTH), lambda i: (0, 0)),
                  _resident((D_MODEL, D_MODEL))],
        out_specs=pl.BlockSpec((steps, nb, D_MODEL), lambda i: (0, 0, 0)),
        compiler_params=_params(("arbitrary",)),
        name="mix_out_sample",
    )(x1, mod3, s5o, ret, w_out)


def kernel(x_prompt, x_sample, state_ssm_re, state_ssm_im, state_ret, c_prompt, c_sample,
           w_ada, b_ada, norm_ffn1, ffn1_w_in, ffn1_w_out, norm_mix, w_in_mix,
           s5_lambda_re, s5_lambda_im, s5_log_dt, s5_b_re, s5_b_im, s5_c_re, s5_c_im, s5_d, s5_w_glu,
           w_out_mix, norm_ffn2, ffn2_w_in, ffn2_w_out, w_ada_final, b_ada_final, norm_final):
    depth = w_ada.shape[0]
    bp, seq, _ = x_prompt.shape
    bs, steps, _ = x_sample.shape

    c_rows = bp + bs
    c_pad = -c_rows % 16
    c_all = jnp.concatenate([c_prompt, c_sample, jnp.zeros((c_pad, D_MODEL), F32)], axis=0)

    cs_p, sn_p = _rotary_tables(np.arange(seq))
    chunk_p = RET_CHUNK if seq % RET_CHUNK == 0 else seq
    decay_p = _decay_tables(chunk_p, chunk_p)
    rows_s = -(-steps // SUBLANES) * SUBLANES
    pos_s = np.concatenate([PAST_LEN + np.arange(steps), np.zeros(rows_s - steps)])
    cs_s, sn_s = _rotary_tables(pos_s)
    decay_s = _decay_tables(steps, rows_s)

    xp = x_prompt
    xs = jnp.transpose(x_sample, (1, 0, 2))
    outs = {k: [] for k in ("p_re", "p_im", "p_ret", "s_re", "s_im", "s_ret")}
    for l in range(depth):
        mod = _ada(c_all, w_ada[l], b_ada[l])
        mod_p = mod[:bp][:, None, :]
        mod_s = mod[bp:c_rows][None]
        w1_in, w1_out = ffn1_w_in[l].astype(BF16), ffn1_w_out[l].astype(BF16)
        w2_in, w2_out = ffn2_w_in[l].astype(BF16), ffn2_w_out[l].astype(BF16)
        w_mix = w_in_mix[l].astype(BF16)
        w_glu = s5_w_glu[l].astype(BF16)
        w_out = w_out_mix[l].astype(BF16)
        lam_t, bw, cw = _s5_tables(s5_lambda_re[l], s5_lambda_im[l], s5_log_dt[l],
                                   s5_b_re[l], s5_b_im[l], s5_c_re[l], s5_c_im[l])
        last = l == depth - 1
        if last:
            fin = _ada(c_all, w_ada_final, b_ada_final)
            fin_p, fin_s = fin[:bp][:, None, :], fin[bp:c_rows][None]
        else:
            fin_p = fin_s = None

        xp = _ffn(xp, mod_p, 0, norm_ffn1[l], w1_in, w1_out, (1, FFN_ROWS))
        s5o, hre, him = _s5_prompt(xp, mod_p, norm_mix[l], w_mix[:, :S5_WIDTH], bw, lam_t, cw, s5_d[l], w_glu)
        xp, sret = _ret_prompt(xp, mod_p, norm_mix[l], w_mix[:, S5_WIDTH:], cs_p, sn_p, decay_p, s5o, w_out)
        xp = _ffn(xp, mod_p, 2, norm_ffn2[l], w2_in, w2_out, (1, FFN_ROWS),
                  fin3=fin_p, norm_f=norm_final if last else None)
        outs["p_re"].append(hre.reshape(bp, S5_GROUPS, S5_STATE))
        outs["p_im"].append(him.reshape(bp, S5_GROUPS, S5_STATE))
        outs["p_ret"].append(sret)

        xs = _ffn(xs, mod_s, 0, norm_ffn1[l], w1_in, w1_out, (steps, bs))
        qkvg, s5o_s, hre_s, him_s = _mix_in_sample(
            xs, mod_s, norm_mix[l], w_mix, bw, lam_t, cw, s5_d[l], w_glu,
            state_ssm_re[l].reshape(bs, -1), state_ssm_im[l].reshape(bs, -1))
        qkvg = jnp.transpose(qkvg.reshape(steps, bs, 4, RET_HEADS, RET_DK), (2, 1, 3, 0, 4))
        qkvg = jnp.pad(qkvg, ((0, 0), (0, 0), (0, 0), (0, rows_s - steps), (0, 0)))
        ret_s, sret_s = _ret_sample(qkvg[0], qkvg[1], qkvg[2], qkvg[3], state_ret[l], cs_s, sn_s, decay_s)
        ret_s = jnp.transpose(ret_s[:, :, :steps, :], (2, 0, 1, 3)).reshape(steps * bs, RET_WIDTH)
        xs = _mix_out_sample(xs, mod_s, s5o_s, ret_s, w_out)
        xs = _ffn(xs, mod_s, 2, norm_ffn2[l], w2_in, w2_out, (steps, bs),
                  fin3=fin_s, norm_f=norm_final if last else None)
        outs["s_re"].append(hre_s.reshape(bs, S5_GROUPS, S5_STATE))
        outs["s_im"].append(him_s.reshape(bs, S5_GROUPS, S5_STATE))
        outs["s_ret"].append(sret_s)

    y_prompt = xp
    y_sample = jnp.transpose(xs, (1, 0, 2))
    return (y_prompt, y_sample, jnp.stack(outs["p_re"]), jnp.stack(outs["p_im"]), jnp.stack(outs["p_ret"]),
            jnp.stack(outs["s_re"]), jnp.stack(outs["s_im"]), jnp.stack(outs["s_ret"]))
```

```python
import functools

import numpy as np
import jax
import jax.numpy as jnp
from jax import lax
from jax.experimental import pallas as pl
from jax.experimental.pallas import tpu as pltpu

F32 = jnp.float32
BF16 = jnp.bfloat16

D_MODEL = 1024
D_FF = 2816
N_MOD = 9
S5_WIDTH = 512
S5_GROUP = 16
S5_GROUPS = 32
S5_STATE = 64
RET_HEADS = 4
RET_DK = 128
RET_DV = 128
RET_WIDTH = 512
RET_CHUNK = 128
PAST_LEN = 16384
ROPE_BASE = 10000.0
EPS = 1e-6

LANES = 128
SUBLANES = 8
S5_PAIRS = S5_GROUPS // 2
S5_LANE_TILES = S5_WIDTH // LANES
VMEM_LIMIT_BYTES = 56 * 1024 * 1024


def _dot(a, b):
    return jnp.dot(a, b, preferred_element_type=F32)


def _params(semantics):
    return pltpu.CompilerParams(dimension_semantics=semantics, vmem_limit_bytes=VMEM_LIMIT_BYTES)


def _resident(shape):
    nd = len(shape)
    return pl.BlockSpec(shape, lambda *_: (0,) * nd, pipeline_mode=pl.Buffered(1))


def _rms_mod(x, norm_w, shift, scale):
    xn = x * lax.rsqrt(jnp.mean(x * x, axis=-1, keepdims=True) + EPS) * norm_w
    return xn * (1.0 + scale) + shift


def _ada_kernel(c_ref, w_ref, b_ref, o_ref):
    c = c_ref[...]
    s = (c * jax.nn.sigmoid(c)).astype(BF16)
    o_ref[...] = _dot(s, w_ref[...].astype(BF16)) + b_ref[...]


def _ada(c, w, b, tn=1024):
    m, k = c.shape
    n = w.shape[1]
    return pl.pallas_call(
        _ada_kernel,
        out_shape=jax.ShapeDtypeStruct((m, n), F32),
        grid=(n // tn,),
        in_specs=[pl.BlockSpec((m, k), lambda j: (0, 0)),
                  pl.BlockSpec((k, tn), lambda j: (0, j)),
                  pl.BlockSpec((1, tn), lambda j: (0, j))],
        out_specs=pl.BlockSpec((m, tn), lambda j: (0, j)),
        compiler_params=_params(("arbitrary",)),
        name="ada_mod",
    )(c, w, b.reshape(1, n))


FF_SPLITS = ((0, 1024), (1024, 2048), (2048, 2816))
FFN_ROWS = 1024


def _ffn_kernel(*refs, final):
    if final:
        x_ref, mod_ref, nw_ref, win_ref, wout_ref, fin_ref, nf_ref, o_ref = refs
    else:
        x_ref, mod_ref, nw_ref, win_ref, wout_ref, o_ref = refs
    x = x_ref[...]
    a_dim, r_dim, _ = x.shape
    m = mod_ref[...]
    shift, scale, gate = m[..., :D_MODEL], m[..., D_MODEL:2 * D_MODEL], m[..., 2 * D_MODEL:]
    h = _rms_mod(x, nw_ref[...], shift, scale)
    hb = h.reshape(a_dim * r_dim, D_MODEL).astype(BF16)
    o = None
    for lo, hi in FF_SPLITS:
        a = _dot(hb, win_ref[:, lo:hi])
        b = _dot(hb, win_ref[:, D_FF + lo:D_FF + hi])
        act = (a * jax.nn.sigmoid(a) * b).astype(BF16)
        part = _dot(act, wout_ref[lo:hi, :])
        o = part if o is None else o + part
    y = x + (0.5 * gate) * o.reshape(a_dim, r_dim, D_MODEL)
    if final:
        f = fin_ref[...]
        y = _rms_mod(y, nf_ref[...], f[..., :D_MODEL], f[..., D_MODEL:])
    o_ref[...] = y


def _ffn(x3, mod3, sub, norm_w, w_in, w_out, blk, fin3=None, norm_f=None):
    at, rt, _ = x3.shape
    a_blk, r_blk = blk
    grid = (at // a_blk, rt // r_blk)
    bm, rm, _ = mod3.shape
    per_batch = bm > 1
    mod_idx = (lambda i, j: (i, 0, sub)) if per_batch else (lambda i, j: (0, 0, sub))
    fin_idx = (lambda i, j: (i, 0, 0)) if per_batch else (lambda i, j: (0, 0, 0))
    final = fin3 is not None
    in_specs = [pl.BlockSpec((a_blk, r_blk, D_MODEL), lambda i, j: (i, j, 0)),
                pl.BlockSpec((1, rm, 3 * D_MODEL), mod_idx),
                _resident((1, D_MODEL)),
                _resident((D_MODEL, 2 * D_FF)),
                _resident((D_FF, D_MODEL))]
    args = [x3, mod3, norm_w.reshape(1, D_MODEL), w_in, w_out]
    if final:
        in_specs += [pl.BlockSpec((1, rm, 2 * D_MODEL), fin_idx), _resident((1, D_MODEL))]
        args += [fin3, norm_f.reshape(1, D_MODEL)]
    return pl.pallas_call(
        functools.partial(_ffn_kernel, final=final),
        out_shape=jax.ShapeDtypeStruct(x3.shape, F32),
        grid=grid,
        in_specs=in_specs,
        out_specs=pl.BlockSpec((a_blk, r_blk, D_MODEL), lambda i, j: (i, j, 0)),
        compiler_params=_params(("arbitrary", "arbitrary")),
        name="ffn_final" if final else "ffn",
    )(*args)


def _s5_tables(lam_re, lam_im, log_dt, b_re, b_im, c_re, c_im):
    lr, li = lam_re.astype(F32), lam_im.astype(F32)
    dt = jnp.exp(log_dt.astype(F32))[:, None]
    mag = jnp.exp(lr * dt)
    ar, ai = mag * jnp.cos(li * dt), mag * jnp.sin(li * dt)
    a2r, a2i = ar * ar - ai * ai, 2.0 * ar * ai
    den = lr * lr + li * li
    cr = ((ar - 1.0) * lr + ai * li) / den
    ci = (ai * lr - (ar - 1.0) * li) / den
    br, bi = b_re.astype(F32), b_im.astype(F32)
    bb_re = cr[:, :, None] * br - ci[:, :, None] * bi
    bb_im = cr[:, :, None] * bi + ci[:, :, None] * br
    lb_re = ar[:, :, None] * bb_re - ai[:, :, None] * bb_im
    lb_im = ar[:, :, None] * bb_im + ai[:, :, None] * bb_re
    cre, cim = c_re.astype(F32), c_im.astype(F32)
    c1_re = cre * ar[:, None, :] - cim * ai[:, None, :]
    c1_im = cre * ai[:, None, :] + cim * ar[:, None, :]
    c2_re = cre * a2r[:, None, :] - cim * a2i[:, None, :]
    c2_im = cre * a2i[:, None, :] + cim * a2r[:, None, :]

    def lanes(t):
        return jnp.broadcast_to(t.reshape(S5_PAIRS, 1, 2 * S5_STATE), (S5_PAIRS, SUBLANES, LANES))

    lam_t = jnp.stack([lanes(ar), lanes(ai)])
    lam2_t = jnp.stack([lanes(a2r), lanes(a2i)])

    groups_per_tile = LANES // S5_GROUP
    sel = (jnp.arange(groups_per_tile)[None, :, None]
           == (2 * (jnp.arange(S5_PAIRS) % 4)[:, None, None] + jnp.arange(2)[None, None, :])).astype(F32)

    def in_op(re, im):
        def one(part):
            part = jnp.transpose(part, (0, 2, 1)).reshape(S5_PAIRS, 1, 2, S5_GROUP, S5_STATE)
            full = sel[:, :, :, None, None] * part
            return jnp.transpose(full, (0, 1, 3, 2, 4)).reshape(S5_PAIRS, LANES, 2 * S5_STATE)
        return jnp.concatenate([one(re), one(im)], axis=-1)

    def out_op(re, im):
        def one(part):
            part = jnp.transpose(part, (0, 2, 1)).reshape(S5_PAIRS, 1, 2, S5_STATE, S5_GROUP)
            full = sel[:, :, :, None, None] * part
            return jnp.transpose(full, (0, 2, 3, 1, 4)).reshape(S5_PAIRS, 2 * S5_STATE, LANES)
        return jnp.concatenate([one(re), one(-im)], axis=1)

    bw = in_op(bb_re, bb_im)
    cw = out_op(cre, cim)
    wz = jnp.concatenate([in_op(lb_re, lb_im), bw], axis=1)
    ws = jnp.concatenate([out_op(c1_re, c1_im), out_op(c2_re, c2_im)], axis=-1)

    def direct(xr, xi):
        hi = lax.Precision.HIGHEST
        return (jnp.einsum('gon,gni->gio', cre, xr, precision=hi)
                - jnp.einsum('gon,gni->gio', cim, xi, precision=hi))

    def tile_diag(k):
        k = k.reshape(S5_LANE_TILES, groups_per_tile, 1, S5_GROUP, S5_GROUP)
        eye = jnp.eye(groups_per_tile, dtype=F32)[None, :, :, None, None]
        full = eye * k
        return jnp.transpose(full, (0, 1, 3, 2, 4)).reshape(S5_LANE_TILES, LANES, LANES)

    k0, k1 = tile_diag(direct(bb_re, bb_im)), tile_diag(direct(lb_re, lb_im))
    wk = jnp.concatenate([jnp.concatenate([k0, k1], axis=-1),
                          jnp.concatenate([jnp.zeros_like(k0), k0], axis=-1)], axis=1)
    one_step = (lam_t, bw.astype(BF16), cw.astype(BF16))
    two_step = (lam2_t, wz.astype(BF16), ws.astype(BF16), wk.astype(BF16))
    return one_step, two_step


def _s5_input(u_tile, bw_ref, j):
    bu = _dot(u_tile.astype(BF16), bw_ref[j])
    return bu[:, :LANES], bu[:, LANES:]


def _s5_readout(u_tiles, xre, xim, cw_ref, d_ref, wglu_ref):
    ys = []
    for jt in range(S5_LANE_TILES):
        acc = d_ref[:, jt * LANES:(jt + 1) * LANES] * u_tiles(jt)
        for jj in range(4):
            j = 4 * jt + jj
            xc = jnp.concatenate([xre(j), xim(j)], axis=-1).astype(BF16)
            acc = acc + _dot(xc, cw_ref[j])
        ys.append(jax.nn.gelu(acc).astype(BF16))
    glu = _dot(jnp.concatenate(ys, axis=-1), wglu_ref[...])
    return glu[:, :S5_WIDTH] * jax.nn.sigmoid(glu[:, S5_WIDTH:])


def _s5_prompt_kernel(x_ref, mod_ref, nw_ref, wu_ref, wz_ref, lam2_ref, ws_ref, wk_ref, d_ref, wglu_ref,
                      o_ref, hre_ref, him_ref,
                      u_slab, sre, sim, st_re, st_im, y_slab, *, tl, nb):
    i = pl.program_id(0)
    nk = tl // 2
    half = nk * nb

    @pl.when(i == 0)
    def _():
        st_re[...] = jnp.zeros_like(st_re)
        st_im[...] = jnp.zeros_like(st_im)

    x = x_ref[...]
    m = mod_ref[...]
    h = _rms_mod(x, nw_ref[...], m[..., :D_MODEL], m[..., D_MODEL:2 * D_MODEL])
    u = _dot(h.reshape(nb * tl, D_MODEL).astype(BF16), wu_ref[...])

    for s in range(S5_LANE_TILES):
        for b in range(nb):
            u_slab[s, pl.ds(b, tl, stride=nb), :] = u[b * tl:(b + 1) * tl, s * LANES:(s + 1) * LANES]

    u_ev, u_od, u_cat = [], [], []
    for s in range(S5_LANE_TILES):
        tiles = u_slab[s].reshape(nk, 2 * nb, LANES)
        ev = tiles[:, :nb, :].reshape(half, LANES)
        od = tiles[:, nb:, :].reshape(half, LANES)
        u_ev.append(ev)
        u_od.append(od)
        u_cat.append(jnp.concatenate([ev, od], axis=-1).astype(BF16))

    for j in range(S5_PAIRS):
        z = _dot(u_cat[j // 4], wz_ref[j])
        sre[j, 0:nb, :] = st_re[j]
        sim[j, 0:nb, :] = st_im[j]
        sre[j, nb:nb + half, :] = z[:, :LANES]
        sim[j, nb:nb + half, :] = z[:, LANES:]

    pairs_per_pass = 8
    for j0 in range(0, S5_PAIRS, pairs_per_pass):
        js = range(j0, j0 + pairs_per_pass)
        lr = [lam2_ref[0, j] for j in js]
        li = [lam2_ref[1, j] for j in js]
        carry = []
        for j in js:
            carry += [st_re[j], st_im[j]]
        for k in range(nk):
            r0 = (k + 1) * nb
            for n, j in enumerate(js):
                re, im = carry[2 * n], carry[2 * n + 1]
                nre = lr[n] * re - li[n] * im + sre[j, r0:r0 + nb, :]
                nim = lr[n] * im + li[n] * re + sim[j, r0:r0 + nb, :]
                if k + 1 < nk:
                    sre[j, r0:r0 + nb, :] = nre
                    sim[j, r0:r0 + nb, :] = nim
                carry[2 * n], carry[2 * n + 1] = nre, nim
        for n, j in enumerate(js):
            st_re[j] = carry[2 * n]
            st_im[j] = carry[2 * n + 1]

    y_ev, y_od = [], []
    for jt in range(S5_LANE_TILES):
        acc = _dot(u_cat[jt], wk_ref[jt])
        for jj in range(4):
            j = 4 * jt + jj
            sp = jnp.concatenate([sre[j, 0:half, :], sim[j, 0:half, :]], axis=-1).astype(BF16)
            acc = acc + _dot(sp, ws_ref[j])
        d = d_ref[:, jt * LANES:(jt + 1) * LANES]
        y_ev.append(jax.nn.gelu(acc[:, :LANES] + d * u_ev[jt]))
        y_od.append(jax.nn.gelu(acc[:, LANES:] + d * u_od[jt]))
    y_ev = jnp.concatenate(y_ev, axis=-1).reshape(nk, nb, S5_WIDTH)
    y_od = jnp.concatenate(y_od, axis=-1).reshape(nk, nb, S5_WIDTH)
    y = jnp.concatenate([y_ev, y_od], axis=1).reshape(tl * nb, S5_WIDTH)
    glu = _dot(y.astype(BF16), wglu_ref[...])
    s5o = glu[:, :S5_WIDTH] * jax.nn.sigmoid(glu[:, S5_WIDTH:])
    for s in range(S5_LANE_TILES):
        y_slab[s] = s5o[:, s * LANES:(s + 1) * LANES]
    for b in range(nb):
        for s in range(S5_LANE_TILES):
            c0 = b * S5_WIDTH + s * LANES
            o_ref[:, c0:c0 + LANES] = y_slab[s, pl.ds(b, tl, stride=nb), :].astype(BF16)

    @pl.when(i == pl.num_programs(0) - 1)
    def _():
        hre_ref[...] = jnp.concatenate([st_re[j] for j in range(S5_PAIRS)], axis=-1)
        him_ref[...] = jnp.concatenate([st_im[j] for j in range(S5_PAIRS)], axis=-1)


def _s5_prompt(x1, mod3, norm_w, w_u, two_step, d_skip, w_glu, tl=128):
    nb, seq, _ = x1.shape
    lam2_t, wz, ws, wk = two_step
    rows = nb * tl
    half = rows // 2
    n_state = S5_GROUPS * S5_STATE
    return pl.pallas_call(
        functools.partial(_s5_prompt_kernel, tl=tl, nb=nb),
        out_shape=(jax.ShapeDtypeStruct((seq, nb * S5_WIDTH), BF16),
                   jax.ShapeDtypeStruct((nb, n_state), F32),
                   jax.ShapeDtypeStruct((nb, n_state), F32)),
        grid=(seq // tl,),
        in_specs=[pl.BlockSpec((nb, tl, D_MODEL), lambda i: (0, i, 0)),
                  pl.BlockSpec((nb, 1, 3 * D_MODEL), lambda i: (0, 0, 1)),
                  _resident((1, D_MODEL)),
                  _resident((D_MODEL, S5_WIDTH)),
                  _resident(wz.shape), _resident(lam2_t.shape), _resident(ws.shape), _resident(wk.shape),
                  _resident((1, S5_WIDTH)),
                  _resident((S5_WIDTH, 2 * S5_WIDTH))],
        out_specs=(pl.BlockSpec((tl, nb * S5_WIDTH), lambda i: (i, 0)),
                   pl.BlockSpec((nb, n_state), lambda i: (0, 0)),
                   pl.BlockSpec((nb, n_state), lambda i: (0, 0))),
        scratch_shapes=[pltpu.VMEM((S5_LANE_TILES, rows, LANES), F32),
                        pltpu.VMEM((S5_PAIRS, nb + half, LANES), F32),
                        pltpu.VMEM((S5_PAIRS, nb + half, LANES), F32),
                        pltpu.VMEM((S5_PAIRS, nb, LANES), F32),
                        pltpu.VMEM((S5_PAIRS, nb, LANES), F32),
                        pltpu.VMEM((S5_LANE_TILES, rows, LANES), F32)],
        compiler_params=_params(("arbitrary",)),
        name="s5_prompt",
    )(x1, mod3, norm_w.reshape(1, D_MODEL), w_u, wz, lam2_t, ws, wk, d_skip.reshape(1, S5_WIDTH), w_glu)


def _rotary_tables(pos):
    half = RET_DK // 2
    inv = ROPE_BASE ** (-np.arange(half, dtype=np.float64) / half)
    ang = np.asarray(pos, np.float64)[:, None] * inv[None, :]
    cos, sin = np.cos(ang), np.sin(ang)
    return (jnp.asarray(np.concatenate([cos, cos], axis=-1), F32),
            jnp.asarray(np.concatenate([-sin, sin], axis=-1), F32))


def _decay_tables(chunk, rows):
    lg = np.log1p(-np.exp2(-5.0 - np.arange(RET_HEADS, dtype=np.float64)))
    idx = np.arange(rows, dtype=np.float64)
    valid = idx < chunk
    diff = idx[:, None] - idx[None, :]
    intra = np.where((diff >= 0) & valid[:, None] & valid[None, :],
                     np.exp(lg[:, None, None] * np.maximum(diff, 0.0)), 0.0)
    q_dec = np.where(valid[None, :], np.exp(lg[:, None] * (idx[None, :] + 1.0)), 0.0)
    k_dec = np.where(valid[None, :], np.exp(lg[:, None] * (chunk - 1.0 - idx)[None, :]), 0.0)
    c_dec = np.exp(lg * chunk)
    q_dec = np.broadcast_to(q_dec[:, :, None], (RET_HEADS, rows, RET_DV))
    k_dec = np.broadcast_to(k_dec[:, :, None], (RET_HEADS, rows, RET_DK))
    c_dec = np.broadcast_to(c_dec[:, None, None], (RET_HEADS, 1, RET_DV))
    return tuple(jnp.asarray(t, F32) for t in (intra, q_dec, k_dec, c_dec))


def _rotary(x, cs, sn):
    return x * cs + pltpu.roll(x, RET_DK // 2, axis=1) * sn


def _retention_chunk(q, k, v, s, intra, q_dec, k_dec, c_dec):
    qb, kb, vb = q.astype(BF16), k.astype(BF16), v.astype(BF16)
    scores = lax.dot_general(qb, kb, (((1,), (1,)), ((), ())), preferred_element_type=F32) * intra
    inner = _dot(scores.astype(BF16), vb)
    cross = _dot(qb, s.astype(BF16)) * q_dec
    kd_t = jnp.transpose(k * k_dec).astype(BF16)
    s_new = s * c_dec + _dot(kd_t, vb)
    return inner + cross, s_new


def _head_norm_gate(ret, g):
    mu = jnp.mean(ret, axis=-1, keepdims=True)
    cen = ret - mu
    var = jnp.mean(cen * cen, axis=-1, keepdims=True)
    return (g * jax.nn.sigmoid(g)) * (cen * lax.rsqrt(var + EPS))


def _ret_prompt_kernel(x_ref, mod_ref, nw_ref, wq_ref, cs_ref, sn_ref, intra_ref, qd_ref, kd_ref, cd_ref,
                       s5o_ref, wout_ref, o_ref, sout_ref, s_scr, *, tm):
    t = pl.program_id(1)

    @pl.when(t == 0)
    def _():
        s_scr[...] = jnp.zeros_like(s_scr)

    x = x_ref[0]
    m = mod_ref[0]
    h = _rms_mod(x, nw_ref[...], m[:, :D_MODEL], m[:, D_MODEL:2 * D_MODEL])
    proj = _dot(h.astype(BF16), wq_ref[...])
    cs, sn = cs_ref[...], sn_ref[...]
    n_chunks = tm // RET_CHUNK
    heads = range(RET_HEADS)
    chunks = range(n_chunks)

    lhs, vbs, kvs = {}, {}, {}
    for hd in heads:
        lo = hd * RET_DK
        q = _rotary(proj[:, lo:lo + RET_DK], cs, sn)
        k = _rotary(proj[:, RET_WIDTH + lo:RET_WIDTH + lo + RET_DK], cs, sn) * (RET_DK ** -0.5)
        v = proj[:, 2 * RET_WIDTH + lo:2 * RET_WIDTH + lo + RET_DV]
        for c in chunks:
            rows = slice(c * RET_CHUNK, (c + 1) * RET_CHUNK)
            qc, kc, vb = q[rows], k[rows], v[rows].astype(BF16)
            scores = lax.dot_general(qc.astype(BF16), kc.astype(BF16), (((1,), (1,)), ((), ())),
                                     preferred_element_type=F32) * intra_ref[hd]
            lhs[hd, c] = jnp.concatenate([scores.astype(BF16), (qc * qd_ref[hd]).astype(BF16)], axis=-1)
            vbs[hd, c] = vb
            kvs[hd, c] = _dot(jnp.transpose(kc * kd_ref[hd]).astype(BF16), vb)

    states = {}
    for hd in heads:
        s = s_scr[hd]
        for c in chunks:
            states[hd, c] = s
            s = s * cd_ref[hd] + kvs[hd, c]
        s_scr[hd] = s

    parts = [s5o_ref[...]]
    for hd in heads:
        lo = 3 * RET_WIDTH + hd * RET_DV
        outs = [_dot(lhs[hd, c], jnp.concatenate([vbs[hd, c], states[hd, c].astype(BF16)], axis=0))
                for c in chunks]
        parts.append(_head_norm_gate(jnp.concatenate(outs, axis=0), proj[:, lo:lo + RET_DV]).astype(BF16))
    mix = jnp.concatenate(parts, axis=-1)
    o_ref[0] = x + m[:, 2 * D_MODEL:] * _dot(mix, wout_ref[...])

    @pl.when(t == pl.num_programs(1) - 1)
    def _():
        sout_ref[0] = s_scr[...]


def _ret_prompt(x1, mod3, norm_w, w_qkvg, cs, sn, decay, s5o, w_out, tm=512):
    nb, seq, _ = x1.shape
    intra, q_dec, k_dec, c_dec = decay
    return pl.pallas_call(
        functools.partial(_ret_prompt_kernel, tm=tm),
        out_shape=(jax.ShapeDtypeStruct(x1.shape, F32),
                   jax.ShapeDtypeStruct((nb, RET_HEADS, RET_DK, RET_DV), F32)),
        grid=(nb, seq // tm),
        in_specs=[pl.BlockSpec((1, tm, D_MODEL), lambda b, t: (b, t, 0)),
                  pl.BlockSpec((1, 1, 3 * D_MODEL), lambda b, t: (b, 0, 1)),
                  _resident((1, D_MODEL)),
                  _resident((D_MODEL, 4 * RET_WIDTH)),
                  pl.BlockSpec((tm, RET_DK), lambda b, t: (t, 0)),
                  pl.BlockSpec((tm, RET_DK), lambda b, t: (t, 0)),
                  _resident(intra.shape), _resident(q_dec.shape), _resident(k_dec.shape), _resident(c_dec.shape),
                  pl.BlockSpec((tm, S5_WIDTH), lambda b, t: (t, b)),
                  _resident((D_MODEL, D_MODEL))],
        out_specs=(pl.BlockSpec((1, tm, D_MODEL), lambda b, t: (b, t, 0)),
                   pl.BlockSpec((1, RET_HEADS, RET_DK, RET_DV), lambda b, t: (b, 0, 0, 0))),
        scratch_shapes=[pltpu.VMEM((RET_HEADS, RET_DK, RET_DV), F32)],
        compiler_params=_params(("arbitrary", "arbitrary")),
        name="ret_prompt",
    )(x1, mod3, norm_w.reshape(1, D_MODEL), w_qkvg, cs, sn, intra, q_dec, k_dec, c_dec, s5o, w_out)


def _mix_in_sample_kernel(x_ref, mod_ref, nw_ref, win_ref, bw_ref, lam_ref, cw_ref, d_ref, wglu_ref,
                          h0re_ref, h0im_ref, qkvg_ref, s5o_ref, hre_ref, him_ref, xre, xim, *, steps, nb):
    x = x_ref[...]
    m = mod_ref[...]
    h = _rms_mod(x, nw_ref[...], m[..., :D_MODEL], m[..., D_MODEL:2 * D_MODEL])
    proj = _dot(h.reshape(steps * nb, D_MODEL).astype(BF16), win_ref[...])
    qkvg_ref[...] = proj[:, S5_WIDTH:]
    u = proj[:, :S5_WIDTH]
    for j in range(S5_PAIRS):
        jt = j // 4
        re, im = _s5_input(u[:, jt * LANES:(jt + 1) * LANES], bw_ref, j)
        lr = lam_ref[0, j][:1]
        li = lam_ref[1, j][:1]
        sre = h0re_ref[:, j * LANES:(j + 1) * LANES]
        sim = h0im_ref[:, j * LANES:(j + 1) * LANES]
        for t in range(steps):
            nre = lr * sre - li * sim + re[t * nb:(t + 1) * nb]
            nim = lr * sim + li * sre + im[t * nb:(t + 1) * nb]
            xre[j, t * nb:(t + 1) * nb, :] = nre
            xim[j, t * nb:(t + 1) * nb, :] = nim
            sre, sim = nre, nim
        hre_ref[:, j * LANES:(j + 1) * LANES] = sre
        him_ref[:, j * LANES:(j + 1) * LANES] = sim
    s5o = _s5_readout(lambda jt: u[:, jt * LANES:(jt + 1) * LANES], lambda j: xre[j], lambda j: xim[j],
                      cw_ref, d_ref, wglu_ref)
    s5o_ref[...] = s5o.astype(BF16)


def _mix_in_sample(x1, mod3, norm_w, w_in, bw, lam_t, cw, d_skip, w_glu, h0_re, h0_im):
    steps, nb, _ = x1.shape
    rows = steps * nb
    n_state = S5_GROUPS * S5_STATE
    in_width = w_in.shape[1]
    args = (x1, mod3, norm_w.reshape(1, D_MODEL), w_in, bw, lam_t, cw, d_skip.reshape(1, S5_WIDTH), w_glu,
            h0_re, h0_im)
    in_specs = [pl.BlockSpec((steps, nb, D_MODEL), lambda i: (0, 0, 0)),
                pl.BlockSpec((1, nb, 3 * D_MODEL), lambda i: (0, 0, 1))]
    in_specs += [_resident(a.shape) for a in args[2:]]
    return pl.pallas_call(
        functools.partial(_mix_in_sample_kernel, steps=steps, nb=nb),
        out_shape=(jax.ShapeDtypeStruct((rows, in_width - S5_WIDTH), F32),
                   jax.ShapeDtypeStruct((rows, S5_WIDTH), BF16),
                   jax.ShapeDtypeStruct((nb, n_state), F32),
                   jax.ShapeDtypeStruct((nb, n_state), F32)),
        grid=(1,),
        in_specs=in_specs,
        out_specs=(pl.BlockSpec((rows, in_width - S5_WIDTH), lambda i: (0, 0)),
                   pl.BlockSpec((rows, S5_WIDTH), lambda i: (0, 0)),
                   pl.BlockSpec((nb, n_state), lambda i: (0, 0)),
                   pl.BlockSpec((nb, n_state), lambda i: (0, 0))),
        scratch_shapes=[pltpu.VMEM((S5_PAIRS, rows, LANES), F32),
                        pltpu.VMEM((S5_PAIRS, rows, LANES), F32)],
        compiler_params=_params(("arbitrary",)),
        name="mix_in_sample",
    )(*args)


def _ret_sample_kernel(q_ref, k_ref, v_ref, g_ref, s_ref, cs_ref, sn_ref, intra_ref, qd_ref, kd_ref, cd_ref,
                       o_ref, sout_ref, *, bb):
    cs, sn = cs_ref[...], sn_ref[...]
    for b in range(bb):
        for hd in range(RET_HEADS):
            q = _rotary(q_ref[b, hd], cs, sn)
            k = _rotary(k_ref[b, hd], cs, sn) * (RET_DK ** -0.5)
            o, s_new = _retention_chunk(q, k, v_ref[b, hd], s_ref[b, hd],
                                        intra_ref[hd], qd_ref[hd], kd_ref[hd], cd_ref[hd])
            sout_ref[b, hd] = s_new
            o_ref[b, hd] = _head_norm_gate(o, g_ref[b, hd])


def _ret_sample(q, k, v, g, s0, cs, sn, decay, bb=8):
    nb, _, rows, _ = q.shape
    intra, q_dec, k_dec, c_dec = decay
    tok = pl.BlockSpec((bb, RET_HEADS, rows, RET_DK), lambda i: (i, 0, 0, 0))
    st = pl.BlockSpec((bb, RET_HEADS, RET_DK, RET_DV), lambda i: (i, 0, 0, 0))
    return pl.pallas_call(
        functools.partial(_ret_sample_kernel, bb=bb),
        out_shape=(jax.ShapeDtypeStruct(q.shape, F32), jax.ShapeDtypeStruct(s0.shape, F32)),
        grid=(nb // bb,),
        in_specs=[tok, tok, tok, tok, st, _resident(cs.shape), _resident(sn.shape),
                  _resident(intra.shape), _resident(q_dec.shape), _resident(k_dec.shape), _resident(c_dec.shape)],
        out_specs=(tok, st),
        compiler_params=_params(("arbitrary",)),
        name="ret_sample",
    )(q, k, v, g, s0, cs, sn, intra, q_dec, k_dec, c_dec)


def _mix_out_sample_kernel(x_ref, mod_ref, s5o_ref, ret_ref, wout_ref, o_ref, *, steps, nb):
    x = x_ref[...]
    gate = mod_ref[...][..., 2 * D_MODEL:]
    mix = jnp.concatenate([s5o_ref[...], ret_ref[...].astype(BF16)], axis=-1)
    o_ref[...] = x + gate * _dot(mix, wout_ref[...]).reshape(steps, nb, D_MODEL)


def _mix_out_sample(x1, mod3, s5o, ret, w_out):
    steps, nb, _ = x1.shape
    rows = steps * nb
    return pl.pallas_call(
        functools.partial(_mix_out_sample_kernel, steps=steps, nb=nb),
        out_shape=jax.ShapeDtypeStruct(x1.shape, F32),
        grid=(1,),
        in_specs=[pl.BlockSpec((steps, nb, D_MODEL), lambda i: (0, 0, 0)),
                  pl.BlockSpec((1, nb, 3 * D_MODEL), lambda i: (0, 0, 1)),
                  pl.BlockSpec((rows, S5_WIDTH), lambda i: (0, 0)),
                  pl.BlockSpec((rows, RET_WIDTH), lambda i: (0, 0)),
                  _resident((D_MODEL, D_MODEL))],
        out_specs=pl.BlockSpec((steps, nb, D_MODEL), lambda i: (0, 0, 0)),
        compiler_params=_params(("arbitrary",)),
        name="mix_out_sample",
    )(x1, mod3, s5o, ret, w_out)


def kernel(x_prompt, x_sample, state_ssm_re, state_ssm_im, state_ret, c_prompt, c_sample,
           w_ada, b_ada, norm_ffn1, ffn1_w_in, ffn1_w_out, norm_mix, w_in_mix,
           s5_lambda_re, s5_lambda_im, s5_log_dt, s5_b_re, s5_b_im, s5_c_re, s5_c_im, s5_d, s5_w_glu,
           w_out_mix, norm_ffn2, ffn2_w_in, ffn2_w_out, w_ada_final, b_ada_final, norm_final):
    depth = w_ada.shape[0]
    bp, seq, _ = x_prompt.shape
    bs, steps, _ = x_sample.shape

    c_rows = bp + bs
    c_pad = -c_rows % 16
    c_all = jnp.concatenate([c_prompt, c_sample, jnp.zeros((c_pad, D_MODEL), F32)], axis=0)

    cs_p, sn_p = _rotary_tables(np.arange(seq))
    chunk_p = RET_CHUNK if seq % RET_CHUNK == 0 else seq
    decay_p = _decay_tables(chunk_p, chunk_p)
    rows_s = -(-steps // SUBLANES) * SUBLANES
    pos_s = np.concatenate([PAST_LEN + np.arange(steps), np.zeros(rows_s - steps)])
    cs_s, sn_s = _rotary_tables(pos_s)
    decay_s = _decay_tables(steps, rows_s)

    xp = x_prompt
    xs = jnp.transpose(x_sample, (1, 0, 2))
    outs = {k: [] for k in ("p_re", "p_im", "p_ret", "s_re", "s_im", "s_ret")}
    for l in range(depth):
        mod = _ada(c_all, w_ada[l], b_ada[l])
        mod_p = mod[:bp][:, None, :]
        mod_s = mod[bp:c_rows][None]
        w1_in, w1_out = ffn1_w_in[l].astype(BF16), ffn1_w_out[l].astype(BF16)
        w2_in, w2_out = ffn2_w_in[l].astype(BF16), ffn2_w_out[l].astype(BF16)
        w_mix = w_in_mix[l].astype(BF16)
        w_glu = s5_w_glu[l].astype(BF16)
        w_out = w_out_mix[l].astype(BF16)
        one_step, two_step = _s5_tables(s5_lambda_re[l], s5_lambda_im[l], s5_log_dt[l],
                                        s5_b_re[l], s5_b_im[l], s5_c_re[l], s5_c_im[l])
        lam_t, bw, cw = one_step
        last = l == depth - 1
        if last:
            fin = _ada(c_all, w_ada_final, b_ada_final)
            fin_p, fin_s = fin[:bp][:, None, :], fin[bp:c_rows][None]
        else:
            fin_p = fin_s = None

        xp = _ffn(xp, mod_p, 0, norm_ffn1[l], w1_in, w1_out, (1, FFN_ROWS))
        s5o, hre, him = _s5_prompt(xp, mod_p, norm_mix[l], w_mix[:, :S5_WIDTH], two_step, s5_d[l], w_glu)
        xp, sret = _ret_prompt(xp, mod_p, norm_mix[l], w_mix[:, S5_WIDTH:], cs_p, sn_p, decay_p, s5o, w_out)
        xp = _ffn(xp, mod_p, 2, norm_ffn2[l], w2_in, w2_out, (1, FFN_ROWS),
                  fin3=fin_p, norm_f=norm_final if last else None)
        outs["p_re"].append(hre.reshape(bp, S5_GROUPS, S5_STATE))
        outs["p_im"].append(him.reshape(bp, S5_GROUPS, S5_STATE))
        outs["p_ret"].append(sret)

        xs = _ffn(xs, mod_s, 0, norm_ffn1[l], w1_in, w1_out, (steps, bs))
        qkvg, s5o_s, hre_s, him_s = _mix_in_sample(
            xs, mod_s, norm_mix[l], w_mix, bw, lam_t, cw, s5_d[l], w_glu,
            state_ssm_re[l].reshape(bs, -1), state_ssm_im[l].reshape(bs, -1))
        qkvg = jnp.transpose(qkvg.reshape(steps, bs, 4, RET_HEADS, RET_DK), (2, 1, 3, 0, 4))
        qkvg = jnp.pad(qkvg, ((0, 0), (0, 0), (0, 0), (0, rows_s - steps), (0, 0)))
        ret_s, sret_s = _ret_sample(qkvg[0], qkvg[1], qkvg[2], qkvg[3], state_ret[l], cs_s, sn_s, decay_s)
        ret_s = jnp.transpose(ret_s[:, :, :steps, :], (2, 0, 1, 3)).reshape(steps * bs, RET_WIDTH)
        xs = _mix_out_sample(xs, mod_s, s5o_s, ret_s, w_out)
        xs = _ffn(xs, mod_s, 2, norm_ffn2[l], w2_in, w2_out, (steps, bs),
                  fin3=fin_s, norm_f=norm_final if last else None)
        outs["s_re"].append(hre_s.reshape(bs, S5_GROUPS, S5_STATE))
        outs["s_im"].append(him_s.reshape(bs, S5_GROUPS, S5_STATE))
        outs["s_ret"].append(sret_s)

    y_prompt = xp
    y_sample = jnp.transpose(xs, (1, 0, 2))
    return (y_prompt, y_sample, jnp.stack(outs["p_re"]), jnp.stack(outs["p_im"]), jnp.stack(outs["p_ret"]),
            jnp.stack(outs["s_re"]), jnp.stack(outs["s_im"]), jnp.stack(outs["s_ret"]))
```

```python
import functools

import numpy as np
import jax
import jax.numpy as jnp
from jax import lax
from jax.experimental import pallas as pl
from jax.experimental.pallas import tpu as pltpu

F32 = jnp.float32
BF16 = jnp.bfloat16

D_MODEL = 1024
D_FF = 2816
N_MOD = 9
S5_WIDTH = 512
S5_GROUP = 16
S5_GROUPS = 32
S5_STATE = 64
RET_HEADS = 4
RET_DK = 128
RET_DV = 128
RET_WIDTH = 512
RET_CHUNK = 128
PAST_LEN = 16384
ROPE_BASE = 10000.0
EPS = 1e-6

LANES = 128
SUBLANES = 8
S5_PAIRS = S5_GROUPS // 2
S5_LANE_TILES = S5_WIDTH // LANES
VMEM_LIMIT_BYTES = 56 * 1024 * 1024


def _dot(a, b):
    return jnp.dot(a, b, preferred_element_type=F32)


def _dot_nt(a, b):
    return lax.dot_general(a, b, (((1,), (1,)), ((), ())), preferred_element_type=F32)


def _params(semantics):
    return pltpu.CompilerParams(dimension_semantics=semantics, vmem_limit_bytes=VMEM_LIMIT_BYTES)


def _resident(shape):
    nd = len(shape)
    return pl.BlockSpec(shape, lambda *_: (0,) * nd, pipeline_mode=pl.Buffered(1))


def _rms_mod(x, norm_w, shift, scale):
    xn = x * lax.rsqrt(jnp.mean(x * x, axis=-1, keepdims=True) + EPS) * norm_w
    return xn * (1.0 + scale) + shift


def _ada_kernel(c_ref, w_ref, b_ref, o_ref):
    c = c_ref[...]
    s = (c * jax.nn.sigmoid(c)).astype(BF16)
    o_ref[...] = _dot(s, w_ref[...].astype(BF16)) + b_ref[...]


def _ada(c, w, b, tn=1024):
    m, k = c.shape
    n = w.shape[1]
    return pl.pallas_call(
        _ada_kernel,
        out_shape=jax.ShapeDtypeStruct((m, n), F32),
        grid=(n // tn,),
        in_specs=[pl.BlockSpec((m, k), lambda j: (0, 0)),
                  pl.BlockSpec((k, tn), lambda j: (0, j)),
                  pl.BlockSpec((1, tn), lambda j: (0, j))],
        out_specs=pl.BlockSpec((m, tn), lambda j: (0, j)),
        compiler_params=_params(("arbitrary",)),
        name="ada_mod",
    )(c, w, b.reshape(1, n))


FF_SPLITS = ((0, 1024), (1024, 2048), (2048, 2816))
FFN_ROWS = 1024


def _ffn_kernel(*refs, final):
    if final:
        x_ref, mod_ref, nw_ref, win_ref, wout_ref, fin_ref, nf_ref, o_ref = refs
    else:
        x_ref, mod_ref, nw_ref, win_ref, wout_ref, o_ref = refs
    x = x_ref[...]
    a_dim, r_dim, _ = x.shape
    m = mod_ref[...]
    shift, scale, gate = m[..., :D_MODEL], m[..., D_MODEL:2 * D_MODEL], m[..., 2 * D_MODEL:]
    h = _rms_mod(x, nw_ref[...], shift, scale)
    hb = h.reshape(a_dim * r_dim, D_MODEL).astype(BF16)
    o = None
    for lo, hi in FF_SPLITS:
        a = _dot(hb, win_ref[:, lo:hi])
        b = _dot(hb, win_ref[:, D_FF + lo:D_FF + hi])
        act = (a * jax.nn.sigmoid(a) * b).astype(BF16)
        part = _dot(act, wout_ref[lo:hi, :])
        o = part if o is None else o + part
    y = x + (0.5 * gate) * o.reshape(a_dim, r_dim, D_MODEL)
    if final:
        f = fin_ref[...]
        y = _rms_mod(y, nf_ref[...], f[..., :D_MODEL], f[..., D_MODEL:])
    o_ref[...] = y


def _ffn(x3, mod3, sub, norm_w, w_in, w_out, blk, fin3=None, norm_f=None):
    at, rt, _ = x3.shape
    a_blk, r_blk = blk
    grid = (at // a_blk, rt // r_blk)
    per_batch = mod3.shape[0] > 1
    rm = 1 if per_batch else r_blk
    mod_idx = (lambda i, j: (i, 0, sub)) if per_batch else (lambda i, j: (0, 0, sub))
    fin_idx = (lambda i, j: (i, 0, 0)) if per_batch else (lambda i, j: (0, 0, 0))
    final = fin3 is not None
    in_specs = [pl.BlockSpec((a_blk, r_blk, D_MODEL), lambda i, j: (i, j, 0)),
                pl.BlockSpec((1, rm, 3 * D_MODEL), mod_idx),
                _resident((1, D_MODEL)),
                _resident((D_MODEL, 2 * D_FF)),
                _resident((D_FF, D_MODEL))]
    args = [x3, mod3, norm_w.reshape(1, D_MODEL), w_in, w_out]
    if final:
        in_specs += [pl.BlockSpec((1, rm, 2 * D_MODEL), fin_idx), _resident((1, D_MODEL))]
        args += [fin3, norm_f.reshape(1, D_MODEL)]
    return pl.pallas_call(
        functools.partial(_ffn_kernel, final=final),
        out_shape=jax.ShapeDtypeStruct(x3.shape, F32),
        grid=grid,
        in_specs=in_specs,
        out_specs=pl.BlockSpec((a_blk, r_blk, D_MODEL), lambda i, j: (i, j, 0)),
        compiler_params=_params(("arbitrary", "arbitrary")),
        name="ffn_final" if final else "ffn",
    )(*args)


def _s5_tables(lam_re, lam_im, log_dt, b_re, b_im, c_re, c_im):
    lr, li = lam_re.astype(F32), lam_im.astype(F32)
    dt = jnp.exp(log_dt.astype(F32))[:, None]
    mag = jnp.exp(lr * dt)
    ar, ai = mag * jnp.cos(li * dt), mag * jnp.sin(li * dt)
    a2r, a2i = ar * ar - ai * ai, 2.0 * ar * ai
    den = lr * lr + li * li
    cr = ((ar - 1.0) * lr + ai * li) / den
    ci = (ai * lr - (ar - 1.0) * li) / den
    br, bi = b_re.astype(F32), b_im.astype(F32)
    bb_re = cr[:, :, None] * br - ci[:, :, None] * bi
    bb_im = cr[:, :, None] * bi + ci[:, :, None] * br
    lb_re = ar[:, :, None] * bb_re - ai[:, :, None] * bb_im
    lb_im = ar[:, :, None] * bb_im + ai[:, :, None] * bb_re
    cre, cim = c_re.astype(F32), c_im.astype(F32)
    c1_re = cre * ar[:, None, :] - cim * ai[:, None, :]
    c1_im = cre * ai[:, None, :] + cim * ar[:, None, :]
    c2_re = cre * a2r[:, None, :] - cim * a2i[:, None, :]
    c2_im = cre * a2i[:, None, :] + cim * a2r[:, None, :]

    lam2_t = jnp.broadcast_to(jnp.stack([a2r, a2i]).reshape(2, S5_PAIRS, 1, LANES),
                              (2, S5_PAIRS, SUBLANES, LANES))

    groups_per_tile = LANES // S5_GROUP
    q_of = 2 * (np.arange(S5_PAIRS) % 4)[:, None] + np.arange(2)[None, :]
    slot = (np.arange(groups_per_tile)[None, :, None] == q_of[:, None, :])
    slot = np.broadcast_to(slot[:, :, None, :, None], (S5_PAIRS, groups_per_tile, 2, 2, S5_STATE))
    slot = jnp.asarray(slot.reshape(S5_PAIRS, 1, groups_per_tile, 1, 2 * LANES), F32)

    def place(parts):
        v = jnp.stack([jnp.stack(p) for p in parts])
        v = v.reshape(2, 2, S5_PAIRS, 2, S5_GROUP, S5_STATE)
        v = jnp.transpose(v, (2, 0, 4, 1, 3, 5)).reshape(S5_PAIRS, 2, 1, S5_GROUP, 2 * LANES)
        return (v * slot).reshape(S5_PAIRS, 2 * LANES, 2 * LANES).astype(BF16)

    def cn(t):
        return jnp.transpose(t, (0, 2, 1))

    wz = place([[cn(lb_re), cn(lb_im)], [cn(bb_re), cn(bb_im)]])
    ws_t = place([[c1_re, -c1_im], [c2_re, -c2_im]])

    c_cat = jnp.concatenate([cre, -cim], axis=-1)
    x_cat = jnp.stack([jnp.concatenate([bb_re, bb_im], axis=1),
                       jnp.concatenate([lb_re, lb_im], axis=1)])
    k = jnp.einsum('gom,xgmi->xgio', c_cat, x_cat, precision=lax.Precision.HIGHEST)
    k = k.reshape(2, S5_LANE_TILES, groups_per_tile, 1, S5_GROUP, S5_GROUP)
    eye = jnp.asarray(np.eye(groups_per_tile)[None, None, :, :, None, None], F32)
    k = jnp.transpose(eye * k, (0, 1, 2, 4, 3, 5)).reshape(2, S5_LANE_TILES, LANES, LANES)
    k0, k1 = k[0], k[1]
    wk = jnp.concatenate([jnp.concatenate([k0, k1], axis=-1),
                          jnp.concatenate([jnp.zeros_like(k0), k0], axis=-1)], axis=1).astype(BF16)
    return lam2_t, wz, ws_t, wk


def _s5_prompt_kernel(x_ref, mod_ref, nw_ref, wu_ref, wz_ref, lam2_ref, ws_ref, wk_ref, d_ref, wglu_ref,
                      o_ref, hre_ref, him_ref,
                      u_slab, sre, sim, st_re, st_im, y_slab, *, tl, nb):
    i = pl.program_id(0)
    nk = tl // 2
    half = nk * nb

    @pl.when(i == 0)
    def _():
        st_re[...] = jnp.zeros_like(st_re)
        st_im[...] = jnp.zeros_like(st_im)

    x = x_ref[...]
    m = mod_ref[...]
    h = _rms_mod(x, nw_ref[...], m[..., :D_MODEL], m[..., D_MODEL:2 * D_MODEL])
    u = _dot(h.reshape(nb * tl, D_MODEL).astype(BF16), wu_ref[...])

    for s in range(S5_LANE_TILES):
        for b in range(nb):
            u_slab[s, pl.ds(b, tl, stride=nb), :] = u[b * tl:(b + 1) * tl, s * LANES:(s + 1) * LANES]

    u_ev, u_od, u_cat = [], [], []
    for s in range(S5_LANE_TILES):
        tiles = u_slab[s].reshape(nk, 2 * nb, LANES)
        ev = tiles[:, :nb, :].reshape(half, LANES)
        od = tiles[:, nb:, :].reshape(half, LANES)
        u_ev.append(ev)
        u_od.append(od)
        u_cat.append(jnp.concatenate([ev, od], axis=-1).astype(BF16))

    for j in range(S5_PAIRS):
        z = _dot(u_cat[j // 4], wz_ref[j])
        sre[j, 0:nb, :] = st_re[j]
        sim[j, 0:nb, :] = st_im[j]
        sre[j, nb:nb + half, :] = z[:, :LANES]
        sim[j, nb:nb + half, :] = z[:, LANES:]

    pairs_per_pass = 8
    for j0 in range(0, S5_PAIRS, pairs_per_pass):
        js = range(j0, j0 + pairs_per_pass)
        lr = [lam2_ref[0, j] for j in js]
        li = [lam2_ref[1, j] for j in js]
        carry = []
        for j in js:
            carry += [st_re[j], st_im[j]]
        for k in range(nk):
            r0 = (k + 1) * nb
            for n, j in enumerate(js):
                re, im = carry[2 * n], carry[2 * n + 1]
                nre = lr[n] * re - li[n] * im + sre[j, r0:r0 + nb, :]
                nim = lr[n] * im + li[n] * re + sim[j, r0:r0 + nb, :]
                if k + 1 < nk:
                    sre[j, r0:r0 + nb, :] = nre
                    sim[j, r0:r0 + nb, :] = nim
                carry[2 * n], carry[2 * n + 1] = nre, nim
        for n, j in enumerate(js):
            st_re[j] = carry[2 * n]
            st_im[j] = carry[2 * n + 1]

    y_ev, y_od = [], []
    for jt in range(S5_LANE_TILES):
        acc = _dot(u_cat[jt], wk_ref[jt])
        for jj in range(4):
            j = 4 * jt + jj
            sp = jnp.concatenate([sre[j, 0:half, :], sim[j, 0:half, :]], axis=-1).astype(BF16)
            acc = acc + _dot_nt(sp, ws_ref[j])
        d = d_ref[:, jt * LANES:(jt + 1) * LANES]
        y_ev.append(jax.nn.gelu(acc[:, :LANES] + d * u_ev[jt]))
        y_od.append(jax.nn.gelu(acc[:, LANES:] + d * u_od[jt]))
    y_ev = jnp.concatenate(y_ev, axis=-1).reshape(nk, nb, S5_WIDTH)
    y_od = jnp.concatenate(y_od, axis=-1).reshape(nk, nb, S5_WIDTH)
    y = jnp.concatenate([y_ev, y_od], axis=1).reshape(tl * nb, S5_WIDTH)
    glu = _dot(y.astype(BF16), wglu_ref[...])
    s5o = glu[:, :S5_WIDTH] * jax.nn.sigmoid(glu[:, S5_WIDTH:])
    for s in range(S5_LANE_TILES):
        y_slab[s] = s5o[:, s * LANES:(s + 1) * LANES]
    for b in range(nb):
        for s in range(S5_LANE_TILES):
            c0 = b * S5_WIDTH + s * LANES
            o_ref[:, c0:c0 + LANES] = y_slab[s, pl.ds(b, tl, stride=nb), :].astype(BF16)

    @pl.when(i == pl.num_programs(0) - 1)
    def _():
        hre_ref[...] = jnp.concatenate([st_re[j] for j in range(S5_PAIRS)], axis=-1)
        him_ref[...] = jnp.concatenate([st_im[j] for j in range(S5_PAIRS)], axis=-1)


def _s5_prompt(x1, mod3, norm_w, w_u, two_step, d_skip, w_glu, tl=128):
    nb, seq, _ = x1.shape
    lam2_t, wz, ws_t, wk = two_step
    rows = nb * tl
    half = rows // 2
    n_state = S5_GROUPS * S5_STATE
    return pl.pallas_call(
        functools.partial(_s5_prompt_kernel, tl=tl, nb=nb),
        out_shape=(jax.ShapeDtypeStruct((seq, nb * S5_WIDTH), BF16),
                   jax.ShapeDtypeStruct((nb, n_state), F32),
                   jax.ShapeDtypeStruct((nb, n_state), F32)),
        grid=(seq // tl,),
        in_specs=[pl.BlockSpec((nb, tl, D_MODEL), lambda i: (0, i, 0)),
                  pl.BlockSpec((nb, 1, 3 * D_MODEL), lambda i: (0, 0, 1)),
                  _resident((1, D_MODEL)),
                  _resident((D_MODEL, S5_WIDTH)),
                  _resident(wz.shape), _resident(lam2_t.shape), _resident(ws_t.shape), _resident(wk.shape),
                  _resident((1, S5_WIDTH)),
                  _resident((S5_WIDTH, 2 * S5_WIDTH))],
        out_specs=(pl.BlockSpec((tl, nb * S5_WIDTH), lambda i: (i, 0)),
                   pl.BlockSpec((nb, n_state), lambda i: (0, 0)),
                   pl.BlockSpec((nb, n_state), lambda i: (0, 0))),
        scratch_shapes=[pltpu.VMEM((S5_LANE_TILES, rows, LANES), F32),
                        pltpu.VMEM((S5_PAIRS, nb + half, LANES), F32),
                        pltpu.VMEM((S5_PAIRS, nb + half, LANES), F32),
                        pltpu.VMEM((S5_PAIRS, nb, LANES), F32),
                        pltpu.VMEM((S5_PAIRS, nb, LANES), F32),
                        pltpu.VMEM((S5_LANE_TILES, rows, LANES), F32)],
        compiler_params=_params(("arbitrary",)),
        name="s5_prompt",
    )(x1, mod3, norm_w.reshape(1, D_MODEL), w_u, wz, lam2_t, ws_t, wk, d_skip.reshape(1, S5_WIDTH), w_glu)


def _rotary_tables(pos):
    half = RET_DK // 2
    inv = ROPE_BASE ** (-np.arange(half, dtype=np.float64) / half)
    ang = np.asarray(pos, np.float64)[:, None] * inv[None, :]
    cos, sin = np.cos(ang), np.sin(ang)
    return (jnp.asarray(np.concatenate([cos, cos], axis=-1), F32),
            jnp.asarray(np.concatenate([-sin, sin], axis=-1), F32))


def _decay_tables(chunk, rows):
    lg = np.log1p(-np.exp2(-5.0 - np.arange(RET_HEADS, dtype=np.float64)))
    idx = np.arange(rows, dtype=np.float64)
    valid = idx < chunk
    diff = idx[:, None] - idx[None, :]
    intra = np.where((diff >= 0) & valid[:, None] & valid[None, :],
                     np.exp(lg[:, None, None] * np.maximum(diff, 0.0)), 0.0)
    q_dec = np.where(valid[None, :], np.exp(lg[:, None] * (idx[None, :] + 1.0)), 0.0)
    k_dec = np.where(valid[None, :], np.exp(lg[:, None] * (chunk - 1.0 - idx)[None, :]), 0.0)
    c_dec = np.exp(lg * chunk)
    q_dec = np.broadcast_to(q_dec[:, :, None], (RET_HEADS, rows, RET_DV))
    k_dec = np.broadcast_to(k_dec[:, :, None], (RET_HEADS, rows, RET_DK))
    c_dec = np.broadcast_to(c_dec[:, None, None], (RET_HEADS, 1, RET_DV))
    return tuple(jnp.asarray(t, F32) for t in (intra, q_dec, k_dec, c_dec))


def _rotary(x, cs, sn):
    return x * cs + pltpu.roll(x, RET_DK // 2, axis=1) * sn


def _head_norm_gate(ret, g):
    mu = jnp.mean(ret, axis=-1, keepdims=True)
    cen = ret - mu
    var = jnp.mean(cen * cen, axis=-1, keepdims=True)
    return (g * jax.nn.sigmoid(g)) * (cen * lax.rsqrt(var + EPS))


def _ret_prompt_kernel(x_ref, mod_ref, nw_ref, wq_ref, cs_ref, sn_ref, intra_ref, qd_ref, kd_ref, cd_ref,
                       s5o_ref, wout_ref, o_ref, sout_ref, s_scr, *, tm):
    t = pl.program_id(1)

    @pl.when(t == 0)
    def _():
        s_scr[...] = jnp.zeros_like(s_scr)

    x = x_ref[0]
    m = mod_ref[0]
    h = _rms_mod(x, nw_ref[...], m[:, :D_MODEL], m[:, D_MODEL:2 * D_MODEL])
    proj = _dot(h.astype(BF16), wq_ref[...])
    cs, sn = cs_ref[...], sn_ref[...]
    n_chunks = tm // RET_CHUNK
    heads = range(RET_HEADS)
    chunks = range(n_chunks)

    lhs, vbs, kvs = {}, {}, {}
    for hd in heads:
        lo = hd * RET_DK
        q = _rotary(proj[:, lo:lo + RET_DK], cs, sn)
        k = _rotary(proj[:, RET_WIDTH + lo:RET_WIDTH + lo + RET_DK], cs, sn) * (RET_DK ** -0.5)
        v = proj[:, 2 * RET_WIDTH + lo:2 * RET_WIDTH + lo + RET_DV]
        for c in chunks:
            rows = slice(c * RET_CHUNK, (c + 1) * RET_CHUNK)
            qc, kc, vb = q[rows], k[rows], v[rows].astype(BF16)
            scores = _dot_nt(qc.astype(BF16), kc.astype(BF16)) * intra_ref[hd]
            lhs[hd, c] = jnp.concatenate([scores.astype(BF16), (qc * qd_ref[hd]).astype(BF16)], axis=-1)
            vbs[hd, c] = vb
            kvs[hd, c] = _dot(jnp.transpose(kc * kd_ref[hd]).astype(BF16), vb)

    states = {}
    for hd in heads:
        s = s_scr[hd]
        for c in chunks:
            states[hd, c] = s
            s = s * cd_ref[hd] + kvs[hd, c]
        s_scr[hd] = s

    parts = [s5o_ref[...]]
    for hd in heads:
        lo = 3 * RET_WIDTH + hd * RET_DV
        outs = [_dot(lhs[hd, c], jnp.concatenate([vbs[hd, c], states[hd, c].astype(BF16)], axis=0))
                for c in chunks]
        parts.append(_head_norm_gate(jnp.concatenate(outs, axis=0), proj[:, lo:lo + RET_DV]).astype(BF16))
    mix = jnp.concatenate(parts, axis=-1)
    o_ref[0] = x + m[:, 2 * D_MODEL:] * _dot(mix, wout_ref[...])

    @pl.when(t == pl.num_programs(1) - 1)
    def _():
        sout_ref[0] = s_scr[...]


def _ret_prompt(x1, mod3, norm_w, w_qkvg, cs, sn, decay, s5o, w_out, tm=512):
    nb, seq, _ = x1.shape
    intra, q_dec, k_dec, c_dec = decay
    return pl.pallas_call(
        functools.partial(_ret_prompt_kernel, tm=tm),
        out_shape=(jax.ShapeDtypeStruct(x1.shape, F32),
                   jax.ShapeDtypeStruct((nb, RET_HEADS, RET_DK, RET_DV), F32)),
        grid=(nb, seq // tm),
        in_specs=[pl.BlockSpec((1, tm, D_MODEL), lambda b, t: (b, t, 0)),
                  pl.BlockSpec((1, 1, 3 * D_MODEL), lambda b, t: (b, 0, 1)),
                  _resident((1, D_MODEL)),
                  _resident((D_MODEL, 4 * RET_WIDTH)),
                  pl.BlockSpec((tm, RET_DK), lambda b, t: (t, 0)),
                  pl.BlockSpec((tm, RET_DK), lambda b, t: (t, 0)),
                  _resident(intra.shape), _resident(q_dec.shape), _resident(k_dec.shape), _resident(c_dec.shape),
                  pl.BlockSpec((tm, S5_WIDTH), lambda b, t: (t, b)),
                  _resident((D_MODEL, D_MODEL))],
        out_specs=(pl.BlockSpec((1, tm, D_MODEL), lambda b, t: (b, t, 0)),
                   pl.BlockSpec((1, RET_HEADS, RET_DK, RET_DV), lambda b, t: (b, 0, 0, 0))),
        scratch_shapes=[pltpu.VMEM((RET_HEADS, RET_DK, RET_DV), F32)],
        compiler_params=_params(("arbitrary", "arbitrary")),
        name="ret_prompt",
    )(x1, mod3, norm_w.reshape(1, D_MODEL), w_qkvg, cs, sn, intra, q_dec, k_dec, c_dec, s5o, w_out)


def _mix_in_sample_kernel(x_ref, mod_ref, nw_ref, win_ref, wz_ref, lam2_ref, ws_ref, wk_ref, d_ref, wglu_ref,
                          h0re_ref, h0im_ref, qkvg_ref, s5o_ref, hre_ref, him_ref, *, steps, nb):
    x = x_ref[...]
    m = mod_ref[...]
    h = _rms_mod(x, nw_ref[...], m[..., :D_MODEL], m[..., D_MODEL:2 * D_MODEL])
    proj = _dot(h.reshape(steps * nb, D_MODEL).astype(BF16), win_ref[...])
    qkvg_ref[...] = proj[:, S5_WIDTH:]
    u = proj[:, :S5_WIDTH]
    nk = steps // 2

    def rows_of(parity, lanes):
        return jnp.concatenate([u[(2 * k + parity) * nb:(2 * k + parity + 1) * nb, lanes] for k in range(nk)],
                               axis=0)

    u_ev, u_od, u_cat = [], [], []
    for s in range(S5_LANE_TILES):
        lanes = slice(s * LANES, (s + 1) * LANES)
        u_ev.append(rows_of(0, lanes))
        u_od.append(rows_of(1, lanes))
        u_cat.append(jnp.concatenate([u_ev[s], u_od[s]], axis=-1).astype(BF16))

    prev = []
    for j in range(S5_PAIRS):
        z = _dot(u_cat[j // 4], wz_ref[j])
        lr = lam2_ref[0, j][:1]
        li = lam2_ref[1, j][:1]
        sre = h0re_ref[:, j * LANES:(j + 1) * LANES]
        sim = h0im_ref[:, j * LANES:(j + 1) * LANES]
        pre, pim = [], []
        for k in range(nk):
            pre.append(sre)
            pim.append(sim)
            zre, zim = z[k * nb:(k + 1) * nb, :LANES], z[k * nb:(k + 1) * nb, LANES:]
            sre, sim = lr * sre - li * sim + zre, lr * sim + li * sre + zim
        hre_ref[:, j * LANES:(j + 1) * LANES] = sre
        him_ref[:, j * LANES:(j + 1) * LANES] = sim
        prev.append(jnp.concatenate([jnp.concatenate(pre, axis=0), jnp.concatenate(pim, axis=0)],
                                    axis=-1).astype(BF16))

    y_ev, y_od = [], []
    for jt in range(S5_LANE_TILES):
        acc = _dot(u_cat[jt], wk_ref[jt])
        for jj in range(4):
            acc = acc + _dot_nt(prev[4 * jt + jj], ws_ref[4 * jt + jj])
        d = d_ref[:, jt * LANES:(jt + 1) * LANES]
        y_ev.append(jax.nn.gelu(acc[:, :LANES] + d * u_ev[jt]))
        y_od.append(jax.nn.gelu(acc[:, LANES:] + d * u_od[jt]))
    y_ev = jnp.concatenate(y_ev, axis=-1)
    y_od = jnp.concatenate(y_od, axis=-1)
    y = jnp.concatenate([part[k * nb:(k + 1) * nb] for k in range(nk) for part in (y_ev, y_od)], axis=0)
    glu = _dot(y.astype(BF16), wglu_ref[...])
    s5o_ref[...] = (glu[:, :S5_WIDTH] * jax.nn.sigmoid(glu[:, S5_WIDTH:])).astype(BF16)


def _mix_in_sample(x1, mod3, norm_w, w_in, two_step, d_skip, w_glu, h0_re, h0_im):
    steps, nb, _ = x1.shape
    rows = steps * nb
    n_state = S5_GROUPS * S5_STATE
    in_width = w_in.shape[1]
    lam2_t, wz, ws_t, wk = two_step
    args = (x1, mod3, norm_w.reshape(1, D_MODEL), w_in, wz, lam2_t, ws_t, wk, d_skip.reshape(1, S5_WIDTH), w_glu,
            h0_re, h0_im)
    in_specs = [pl.BlockSpec((steps, nb, D_MODEL), lambda i: (0, 0, 0)),
                pl.BlockSpec((1, nb, 3 * D_MODEL), lambda i: (0, 0, 1))]
    in_specs += [_resident(a.shape) for a in args[2:]]
    return pl.pallas_call(
        functools.partial(_mix_in_sample_kernel, steps=steps, nb=nb),
        out_shape=(jax.ShapeDtypeStruct((rows, in_width - S5_WIDTH), F32),
                   jax.ShapeDtypeStruct((rows, S5_WIDTH), BF16),
                   jax.ShapeDtypeStruct((nb, n_state), F32),
                   jax.ShapeDtypeStruct((nb, n_state), F32)),
        grid=(1,),
        in_specs=in_specs,
        out_specs=(pl.BlockSpec((rows, in_width - S5_WIDTH), lambda i: (0, 0)),
                   pl.BlockSpec((rows, S5_WIDTH), lambda i: (0, 0)),
                   pl.BlockSpec((nb, n_state), lambda i: (0, 0)),
                   pl.BlockSpec((nb, n_state), lambda i: (0, 0))),
        compiler_params=_params(("arbitrary",)),
        name="mix_in_sample",
    )(*args)


def _ret_sample_kernel(q_ref, k_ref, v_ref, g_ref, s_ref, cs_ref, sn_ref, intra_ref, qd_ref, kd_ref, cd_ref,
                       o_ref, sout_ref, *, bb):
    cs, sn = cs_ref[...], sn_ref[...]
    pairs = [(b, hd) for b in range(bb) for hd in range(RET_HEADS)]
    scores, cross_lhs, vbs = {}, {}, {}
    for b, hd in pairs:
        q = _rotary(q_ref[b, hd], cs, sn)
        k = _rotary(k_ref[b, hd], cs, sn) * (RET_DK ** -0.5)
        vb = v_ref[b, hd].astype(BF16)
        scores[b, hd] = _dot_nt(q.astype(BF16), k.astype(BF16)) * intra_ref[hd]
        cross_lhs[b, hd] = (q * qd_ref[hd]).astype(BF16)
        vbs[b, hd] = vb
        sout_ref[b, hd] = s_ref[b, hd] * cd_ref[hd] + _dot(jnp.transpose(k * kd_ref[hd]).astype(BF16), vb)
    for b, hd in pairs:
        o = _dot(scores[b, hd].astype(BF16), vbs[b, hd]) + _dot(cross_lhs[b, hd], s_ref[b, hd].astype(BF16))
        o_ref[b, hd] = _head_norm_gate(o, g_ref[b, hd])


def _ret_sample(q, k, v, g, s0, cs, sn, decay, bb=8):
    nb, _, rows, _ = q.shape
    intra, q_dec, k_dec, c_dec = decay
    tok = pl.BlockSpec((bb, RET_HEADS, rows, RET_DK), lambda i: (i, 0, 0, 0))
    st = pl.BlockSpec((bb, RET_HEADS, RET_DK, RET_DV), lambda i: (i, 0, 0, 0))
    return pl.pallas_call(
        functools.partial(_ret_sample_kernel, bb=bb),
        out_shape=(jax.ShapeDtypeStruct(q.shape, F32), jax.ShapeDtypeStruct(s0.shape, F32)),
        grid=(nb // bb,),
        in_specs=[tok, tok, tok, tok, st, _resident(cs.shape), _resident(sn.shape),
                  _resident(intra.shape), _resident(q_dec.shape), _resident(k_dec.shape), _resident(c_dec.shape)],
        out_specs=(tok, st),
        compiler_params=_params(("arbitrary",)),
        name="ret_sample",
    )(q, k, v, g, s0, cs, sn, intra, q_dec, k_dec, c_dec)


def _mix_out_sample_kernel(x_ref, mod_ref, s5o_ref, ret_ref, wout_ref, o_ref, *, steps, nb):
    x = x_ref[...]
    gate = mod_ref[...][..., 2 * D_MODEL:]
    mix = jnp.concatenate([s5o_ref[...], ret_ref[...].astype(BF16)], axis=-1)
    o_ref[...] = x + gate * _dot(mix, wout_ref[...]).reshape(steps, nb, D_MODEL)


def _mix_out_sample(x1, mod3, s5o, ret, w_out):
    steps, nb, _ = x1.shape
    rows = steps * nb
    return pl.pallas_call(
        functools.partial(_mix_out_sample_kernel, steps=steps, nb=nb),
        out_shape=jax.ShapeDtypeStruct(x1.shape, F32),
        grid=(1,),
        in_specs=[pl.BlockSpec((steps, nb, D_MODEL), lambda i: (0, 0, 0)),
                  pl.BlockSpec((1, nb, 3 * D_MODEL), lambda i: (0, 0, 1)),
                  pl.BlockSpec((rows, S5_WIDTH), lambda i: (0, 0)),
                  pl.BlockSpec((rows, RET_WIDTH), lambda i: (0, 0)),
                  _resident((D_MODEL, D_MODEL))],
        out_specs=pl.BlockSpec((steps, nb, D_MODEL), lambda i: (0, 0, 0)),
        compiler_params=_params(("arbitrary",)),
        name="mix_out_sample",
    )(x1, mod3, s5o, ret, w_out)


def kernel(x_prompt, x_sample, state_ssm_re, state_ssm_im, state_ret, c_prompt, c_sample,
           w_ada, b_ada, norm_ffn1, ffn1_w_in, ffn1_w_out, norm_mix, w_in_mix,
           s5_lambda_re, s5_lambda_im, s5_log_dt, s5_b_re, s5_b_im, s5_c_re, s5_c_im, s5_d, s5_w_glu,
           w_out_mix, norm_ffn2, ffn2_w_in, ffn2_w_out, w_ada_final, b_ada_final, norm_final):
    depth = w_ada.shape[0]
    bp, seq, _ = x_prompt.shape
    bs, steps, _ = x_sample.shape
    assert seq % RET_CHUNK == 0 and steps % 2 == 0 and steps <= SUBLANES

    c_rows = bs + bp
    c_pad = -c_rows % 16
    c_all = jnp.concatenate([c_sample, c_prompt, jnp.zeros((c_pad, D_MODEL), F32)], axis=0)

    cs_p, sn_p = _rotary_tables(np.arange(seq))
    decay_p = _decay_tables(RET_CHUNK, RET_CHUNK)
    rows_s = SUBLANES
    pos_s = np.concatenate([PAST_LEN + np.arange(steps), np.zeros(rows_s - steps)])
    cs_s, sn_s = _rotary_tables(pos_s)
    decay_s = _decay_tables(steps, rows_s)

    xp = x_prompt
    xs = jnp.transpose(x_sample, (1, 0, 2))
    outs = {k: [] for k in ("p_re", "p_im", "p_ret", "s_re", "s_im", "s_ret")}
    for l in range(depth):
        mod = _ada(c_all, w_ada[l], b_ada[l])
        mod_p = mod[bs:c_rows][:, None, :]
        mod_s = mod[None]
        w1_in, w1_out = ffn1_w_in[l].astype(BF16), ffn1_w_out[l].astype(BF16)
        w2_in, w2_out = ffn2_w_in[l].astype(BF16), ffn2_w_out[l].astype(BF16)
        w_mix = w_in_mix[l].astype(BF16)
        w_glu = s5_w_glu[l].astype(BF16)
        w_out = w_out_mix[l].astype(BF16)
        two_step = _s5_tables(s5_lambda_re[l], s5_lambda_im[l], s5_log_dt[l],
                              s5_b_re[l], s5_b_im[l], s5_c_re[l], s5_c_im[l])
        last = l == depth - 1
        if last:
            fin = _ada(c_all, w_ada_final, b_ada_final)
            fin_p, fin_s = fin[bs:c_rows][:, None, :], fin[None]
        else:
            fin_p = fin_s = None

        xp = _ffn(xp, mod_p, 0, norm_ffn1[l], w1_in, w1_out, (1, FFN_ROWS))
        s5o, hre, him = _s5_prompt(xp, mod_p, norm_mix[l], w_mix[:, :S5_WIDTH], two_step, s5_d[l], w_glu)
        xp, sret = _ret_prompt(xp, mod_p, norm_mix[l], w_mix[:, S5_WIDTH:], cs_p, sn_p, decay_p, s5o, w_out)
        xp = _ffn(xp, mod_p, 2, norm_ffn2[l], w2_in, w2_out, (1, FFN_ROWS),
                  fin3=fin_p, norm_f=norm_final if last else None)
        outs["p_re"].append(hre.reshape(bp, S5_GROUPS, S5_STATE))
        outs["p_im"].append(him.reshape(bp, S5_GROUPS, S5_STATE))
        outs["p_ret"].append(sret)

        xs = _ffn(xs, mod_s, 0, norm_ffn1[l], w1_in, w1_out, (steps, bs))
        qkvg, s5o_s, hre_s, him_s = _mix_in_sample(
            xs, mod_s, norm_mix[l], w_mix, two_step, s5_d[l], w_glu,
            state_ssm_re[l].reshape(bs, -1), state_ssm_im[l].reshape(bs, -1))
        qkvg = jnp.transpose(qkvg.reshape(steps, bs, 4, RET_HEADS, RET_DK), (2, 1, 3, 0, 4))
        qkvg = jnp.pad(qkvg, ((0, 0), (0, 0), (0, 0), (0, rows_s - steps), (0, 0)))
        ret_s, sret_s = _ret_sample(qkvg[0], qkvg[1], qkvg[2], qkvg[3], state_ret[l], cs_s, sn_s, decay_s)
        ret_s = jnp.transpose(ret_s[:, :, :steps, :], (2, 0, 1, 3)).reshape(steps * bs, RET_WIDTH)
        xs = _mix_out_sample(xs, mod_s, s5o_s, ret_s, w_out)
        xs = _ffn(xs, mod_s, 2, norm_ffn2[l], w2_in, w2_out, (steps, bs),
                  fin3=fin_s, norm_f=norm_final if last else None)
        outs["s_re"].append(hre_s.reshape(bs, S5_GROUPS, S5_STATE))
        outs["s_im"].append(him_s.reshape(bs, S5_GROUPS, S5_STATE))
        outs["s_ret"].append(sret_s)

    y_prompt = xp
    y_sample = jnp.transpose(xs, (1, 0, 2))
    return (y_prompt, y_sample, jnp.stack(outs["p_re"]), jnp.stack(outs["p_im"]), jnp.stack(outs["p_ret"]),
            jnp.stack(outs["s_re"]), jnp.stack(outs["s_im"]), jnp.stack(outs["s_ret"]))
```

```python
import functools

import numpy as np
import jax
import jax.numpy as jnp
from jax import lax
from jax.experimental import pallas as pl
from jax.experimental.pallas import tpu as pltpu

F32 = jnp.float32
BF16 = jnp.bfloat16

D_MODEL = 1024
D_FF = 2816
N_MOD = 9
S5_WIDTH = 512
S5_GROUP = 16
S5_GROUPS = 32
S5_STATE = 64
RET_HEADS = 4
RET_DK = 128
RET_DV = 128
RET_WIDTH = 512
RET_CHUNK = 128
PAST_LEN = 16384
ROPE_BASE = 10000.0
EPS = 1e-6

LANES = 128
SUBLANES = 8
S5_PAIRS = S5_GROUPS // 2
S5_LANE_TILES = S5_WIDTH // LANES
VMEM_LIMIT_BYTES = 56 * 1024 * 1024


def _dot(a, b):
    return jnp.dot(a, b, preferred_element_type=F32)


def _dot_nt(a, b):
    return lax.dot_general(a, b, (((1,), (1,)), ((), ())), preferred_element_type=F32)


def _params(semantics):
    return pltpu.CompilerParams(dimension_semantics=semantics, vmem_limit_bytes=VMEM_LIMIT_BYTES)


def _resident(shape):
    nd = len(shape)
    return pl.BlockSpec(shape, lambda *_: (0,) * nd, pipeline_mode=pl.Buffered(1))


def _rms_mod(x, norm_w, shift, scale):
    xn = x * lax.rsqrt(jnp.mean(x * x, axis=-1, keepdims=True) + EPS) * norm_w
    return xn * (1.0 + scale) + shift


def _ada_kernel(c_ref, w_ref, b_ref, o_ref):
    c = c_ref[...]
    s = (c * jax.nn.sigmoid(c)).astype(BF16)
    o_ref[...] = _dot(s, w_ref[...].astype(BF16)) + b_ref[...]


def _ada(c, w, b, tn=1024):
    m, k = c.shape
    n = w.shape[1]
    return pl.pallas_call(
        _ada_kernel,
        out_shape=jax.ShapeDtypeStruct((m, n), F32),
        grid=(n // tn,),
        in_specs=[pl.BlockSpec((m, k), lambda j: (0, 0)),
                  pl.BlockSpec((k, tn), lambda j: (0, j)),
                  pl.BlockSpec((1, tn), lambda j: (0, j))],
        out_specs=pl.BlockSpec((m, tn), lambda j: (0, j)),
        compiler_params=_params(("arbitrary",)),
        name="ada_mod",
    )(c, w, b.reshape(1, n))


FF_SPLITS = ((0, 1024), (1024, 2048), (2048, 2816))
FFN_ROWS = 1024


def _ffn_kernel(*refs, final):
    if final:
        x_ref, mod_ref, nw_ref, win_ref, wout_ref, fin_ref, nf_ref, o_ref = refs
    else:
        x_ref, mod_ref, nw_ref, win_ref, wout_ref, o_ref = refs
    x = x_ref[...]
    a_dim, r_dim, _ = x.shape
    m = mod_ref[...]
    shift, scale, gate = m[..., :D_MODEL], m[..., D_MODEL:2 * D_MODEL], m[..., 2 * D_MODEL:]
    h = _rms_mod(x, nw_ref[...], shift, scale)
    hb = h.reshape(a_dim * r_dim, D_MODEL).astype(BF16)
    o = None
    for lo, hi in FF_SPLITS:
        a = _dot(hb, win_ref[:, lo:hi])
        b = _dot(hb, win_ref[:, D_FF + lo:D_FF + hi])
        act = (a * jax.nn.sigmoid(a) * b).astype(BF16)
        part = _dot(act, wout_ref[lo:hi, :])
        o = part if o is None else o + part
    y = x + (0.5 * gate) * o.reshape(a_dim, r_dim, D_MODEL)
    if final:
        f = fin_ref[...]
        y = _rms_mod(y, nf_ref[...], f[..., :D_MODEL], f[..., D_MODEL:])
    o_ref[...] = y


def _ffn(x3, mod3, sub, norm_w, w_in, w_out, blk, fin3=None, norm_f=None):
    at, rt, _ = x3.shape
    a_blk, r_blk = blk
    grid = (at // a_blk, rt // r_blk)
    per_batch = mod3.shape[0] > 1
    rm = 1 if per_batch else r_blk
    mod_idx = (lambda i, j: (i, 0, sub)) if per_batch else (lambda i, j: (0, 0, sub))
    fin_idx = (lambda i, j: (i, 0, 0)) if per_batch else (lambda i, j: (0, 0, 0))
    final = fin3 is not None
    in_specs = [pl.BlockSpec((a_blk, r_blk, D_MODEL), lambda i, j: (i, j, 0)),
                pl.BlockSpec((1, rm, 3 * D_MODEL), mod_idx),
                _resident((1, D_MODEL)),
                _resident((D_MODEL, 2 * D_FF)),
                _resident((D_FF, D_MODEL))]
    args = [x3, mod3, norm_w.reshape(1, D_MODEL), w_in, w_out]
    if final:
        in_specs += [pl.BlockSpec((1, rm, 2 * D_MODEL), fin_idx), _resident((1, D_MODEL))]
        args += [fin3, norm_f.reshape(1, D_MODEL)]
    return pl.pallas_call(
        functools.partial(_ffn_kernel, final=final),
        out_shape=jax.ShapeDtypeStruct(x3.shape, F32),
        grid=grid,
        in_specs=in_specs,
        out_specs=pl.BlockSpec((a_blk, r_blk, D_MODEL), lambda i, j: (i, j, 0)),
        compiler_params=_params(("arbitrary", "arbitrary")),
        name="ffn_final" if final else "ffn",
    )(*args)


def _s5_tables(lam_re, lam_im, log_dt, b_re, b_im, c_re, c_im):
    lr, li = lam_re.astype(F32), lam_im.astype(F32)
    dt = jnp.exp(log_dt.astype(F32))[:, None]
    mag = jnp.exp(lr * dt)
    ar, ai = mag * jnp.cos(li * dt), mag * jnp.sin(li * dt)
    a2r, a2i = ar * ar - ai * ai, 2.0 * ar * ai
    den = lr * lr + li * li
    cr = ((ar - 1.0) * lr + ai * li) / den
    ci = (ai * lr - (ar - 1.0) * li) / den
    br, bi = b_re.astype(F32), b_im.astype(F32)
    bb_re = cr[:, :, None] * br - ci[:, :, None] * bi
    bb_im = cr[:, :, None] * bi + ci[:, :, None] * br
    lb_re = ar[:, :, None] * bb_re - ai[:, :, None] * bb_im
    lb_im = ar[:, :, None] * bb_im + ai[:, :, None] * bb_re
    cre, cim = c_re.astype(F32), c_im.astype(F32)
    c1_re = cre * ar[:, None, :] - cim * ai[:, None, :]
    c1_im = cre * ai[:, None, :] + cim * ar[:, None, :]
    c2_re = cre * a2r[:, None, :] - cim * a2i[:, None, :]
    c2_im = cre * a2i[:, None, :] + cim * a2r[:, None, :]

    lam2_t = jnp.broadcast_to(jnp.stack([a2r, a2i]).reshape(2, S5_PAIRS, 1, LANES),
                              (2, S5_PAIRS, SUBLANES, LANES))

    groups_per_tile = LANES // S5_GROUP
    q_of = 2 * (np.arange(S5_PAIRS) % 4)[:, None] + np.arange(2)[None, :]
    slot = (np.arange(groups_per_tile)[None, :, None] == q_of[:, None, :])
    slot = np.broadcast_to(slot[:, :, None, :, None], (S5_PAIRS, groups_per_tile, 2, 2, S5_STATE))
    slot = jnp.asarray(slot.reshape(S5_PAIRS, 1, groups_per_tile, 1, 2 * LANES), F32)

    def place(parts):
        v = jnp.stack([jnp.stack(p) for p in parts])
        v = v.reshape(2, 2, S5_PAIRS, 2, S5_GROUP, S5_STATE)
        v = jnp.transpose(v, (2, 0, 4, 1, 3, 5)).reshape(S5_PAIRS, 2, 1, S5_GROUP, 2 * LANES)
        return (v * slot).reshape(S5_PAIRS, 2 * LANES, 2 * LANES).astype(BF16)

    def cn(t):
        return jnp.transpose(t, (0, 2, 1))

    wz = place([[cn(lb_re), cn(lb_im)], [cn(bb_re), cn(bb_im)]])
    ws_t = place([[c1_re, -c1_im], [c2_re, -c2_im]])

    c_cat = jnp.concatenate([cre, -cim], axis=-1)
    x_cat = jnp.stack([jnp.concatenate([bb_re, bb_im], axis=1),
                       jnp.concatenate([lb_re, lb_im], axis=1)])
    k = jnp.einsum('gom,xgmi->xgio', c_cat, x_cat, precision=lax.Precision.HIGHEST)
    k = k.reshape(2, S5_LANE_TILES, groups_per_tile, 1, S5_GROUP, S5_GROUP)
    eye = jnp.asarray(np.eye(groups_per_tile)[None, None, :, :, None, None], F32)
    k = jnp.transpose(eye * k, (0, 1, 2, 4, 3, 5)).reshape(2, S5_LANE_TILES, LANES, LANES)
    k0, k1 = k[0], k[1]
    wk = jnp.concatenate([jnp.concatenate([k0, k1], axis=-1),
                          jnp.concatenate([jnp.zeros_like(k0), k0], axis=-1)], axis=1).astype(BF16)
    return lam2_t, wz, ws_t, wk


def _s5_prompt_kernel(x_ref, mod_ref, nw_ref, wu_ref, wz_ref, lam2_ref, ws_ref, wk_ref, d_ref, wglu_ref,
                      o_ref, hre_ref, him_ref,
                      u_slab, sre, sim, st_re, st_im, y_slab, *, tl, nb):
    i = pl.program_id(0)
    nk = tl // 2
    half = nk * nb

    @pl.when(i == 0)
    def _():
        st_re[...] = jnp.zeros_like(st_re)
        st_im[...] = jnp.zeros_like(st_im)

    x = x_ref[...]
    m = mod_ref[...]
    h = _rms_mod(x, nw_ref[...], m[..., :D_MODEL], m[..., D_MODEL:2 * D_MODEL])
    u = _dot(h.reshape(nb * tl, D_MODEL).astype(BF16), wu_ref[...])

    for s in range(S5_LANE_TILES):
        for b in range(nb):
            u_slab[s, pl.ds(b, tl, stride=nb), :] = u[b * tl:(b + 1) * tl, s * LANES:(s + 1) * LANES]

    u_ev, u_od, u_cat = [], [], []
    for s in range(S5_LANE_TILES):
        tiles = u_slab[s].reshape(nk, 2 * nb, LANES)
        ev = tiles[:, :nb, :].reshape(half, LANES)
        od = tiles[:, nb:, :].reshape(half, LANES)
        u_ev.append(ev)
        u_od.append(od)
        u_cat.append(jnp.concatenate([ev, od], axis=-1).astype(BF16))

    for j in range(S5_PAIRS):
        z = _dot(u_cat[j // 4], wz_ref[j])
        sre[j, 0:nb, :] = st_re[j]
        sim[j, 0:nb, :] = st_im[j]
        sre[j, nb:nb + half, :] = z[:, :LANES]
        sim[j, nb:nb + half, :] = z[:, LANES:]

    pairs_per_pass = 8
    for j0 in range(0, S5_PAIRS, pairs_per_pass):
        js = range(j0, j0 + pairs_per_pass)
        lr = [lam2_ref[0, j] for j in js]
        li = [lam2_ref[1, j] for j in js]
        carry = []
        for j in js:
            carry += [st_re[j], st_im[j]]
        for k in range(nk):
            r0 = (k + 1) * nb
            for n, j in enumerate(js):
                re, im = carry[2 * n], carry[2 * n + 1]
                nre = lr[n] * re - li[n] * im + sre[j, r0:r0 + nb, :]
                nim = lr[n] * im + li[n] * re + sim[j, r0:r0 + nb, :]
                if k + 1 < nk:
                    sre[j, r0:r0 + nb, :] = nre
                    sim[j, r0:r0 + nb, :] = nim
                carry[2 * n], carry[2 * n + 1] = nre, nim
        for n, j in enumerate(js):
            st_re[j] = carry[2 * n]
            st_im[j] = carry[2 * n + 1]

    y_ev, y_od = [], []
    for jt in range(S5_LANE_TILES):
        acc = _dot(u_cat[jt], wk_ref[jt])
        for jj in range(4):
            j = 4 * jt + jj
            sp = jnp.concatenate([sre[j, 0:half, :], sim[j, 0:half, :]], axis=-1).astype(BF16)
            acc = acc + _dot_nt(sp, ws_ref[j])
        d = d_ref[:, jt * LANES:(jt + 1) * LANES]
        y_ev.append(jax.nn.gelu(acc[:, :LANES] + d * u_ev[jt]))
        y_od.append(jax.nn.gelu(acc[:, LANES:] + d * u_od[jt]))
    y_ev = jnp.concatenate(y_ev, axis=-1).reshape(nk, nb, S5_WIDTH)
    y_od = jnp.concatenate(y_od, axis=-1).reshape(nk, nb, S5_WIDTH)
    y = jnp.concatenate([y_ev, y_od], axis=1).reshape(tl * nb, S5_WIDTH)
    glu = _dot(y.astype(BF16), wglu_ref[...])
    s5o = glu[:, :S5_WIDTH] * jax.nn.sigmoid(glu[:, S5_WIDTH:])
    for s in range(S5_LANE_TILES):
        y_slab[s] = s5o[:, s * LANES:(s + 1) * LANES]
    for b in range(nb):
        for s in range(S5_LANE_TILES):
            c0 = b * S5_WIDTH + s * LANES
            o_ref[:, c0:c0 + LANES] = y_slab[s, pl.ds(b, tl, stride=nb), :].astype(BF16)

    @pl.when(i == pl.num_programs(0) - 1)
    def _():
        hre_ref[...] = jnp.concatenate([st_re[j] for j in range(S5_PAIRS)], axis=-1)
        him_ref[...] = jnp.concatenate([st_im[j] for j in range(S5_PAIRS)], axis=-1)


def _s5_prompt(x1, mod3, norm_w, w_u, two_step, d_skip, w_glu, tl=128):
    nb, seq, _ = x1.shape
    lam2_t, wz, ws_t, wk = two_step
    rows = nb * tl
    half = rows // 2
    n_state = S5_GROUPS * S5_STATE
    return pl.pallas_call(
        functools.partial(_s5_prompt_kernel, tl=tl, nb=nb),
        out_shape=(jax.ShapeDtypeStruct((seq, nb * S5_WIDTH), BF16),
                   jax.ShapeDtypeStruct((nb, n_state), F32),
                   jax.ShapeDtypeStruct((nb, n_state), F32)),
        grid=(seq // tl,),
        in_specs=[pl.BlockSpec((nb, tl, D_MODEL), lambda i: (0, i, 0)),
                  pl.BlockSpec((nb, 1, 3 * D_MODEL), lambda i: (0, 0, 1)),
                  _resident((1, D_MODEL)),
                  _resident((D_MODEL, S5_WIDTH)),
                  _resident(wz.shape), _resident(lam2_t.shape), _resident(ws_t.shape), _resident(wk.shape),
                  _resident((1, S5_WIDTH)),
                  _resident((S5_WIDTH, 2 * S5_WIDTH))],
        out_specs=(pl.BlockSpec((tl, nb * S5_WIDTH), lambda i: (i, 0)),
                   pl.BlockSpec((nb, n_state), lambda i: (0, 0)),
                   pl.BlockSpec((nb, n_state), lambda i: (0, 0))),
        scratch_shapes=[pltpu.VMEM((S5_LANE_TILES, rows, LANES), F32),
                        pltpu.VMEM((S5_PAIRS, nb + half, LANES), F32),
                        pltpu.VMEM((S5_PAIRS, nb + half, LANES), F32),
                        pltpu.VMEM((S5_PAIRS, nb, LANES), F32),
                        pltpu.VMEM((S5_PAIRS, nb, LANES), F32),
                        pltpu.VMEM((S5_LANE_TILES, rows, LANES), F32)],
        compiler_params=_params(("arbitrary",)),
        name="s5_prompt",
    )(x1, mod3, norm_w.reshape(1, D_MODEL), w_u, wz, lam2_t, ws_t, wk, d_skip.reshape(1, S5_WIDTH), w_glu)


def _rotary_tables(pos):
    half = RET_DK // 2
    inv = ROPE_BASE ** (-np.arange(half, dtype=np.float64) / half)
    ang = np.asarray(pos, np.float64)[:, None] * inv[None, :]
    cos, sin = np.cos(ang), np.sin(ang)
    return (jnp.asarray(np.concatenate([cos, cos], axis=-1), F32),
            jnp.asarray(np.concatenate([-sin, sin], axis=-1), F32))


def _decay_tables(chunk, rows):
    lg = np.log1p(-np.exp2(-5.0 - np.arange(RET_HEADS, dtype=np.float64)))
    idx = np.arange(rows, dtype=np.float64)
    valid = idx < chunk
    diff = idx[:, None] - idx[None, :]
    intra = np.where((diff >= 0) & valid[:, None] & valid[None, :],
                     np.exp(lg[:, None, None] * np.maximum(diff, 0.0)), 0.0)
    q_dec = np.where(valid[None, :], np.exp(lg[:, None] * (idx[None, :] + 1.0)), 0.0)
    k_dec = np.where(valid[None, :], np.exp(lg[:, None] * (chunk - 1.0 - idx)[None, :]), 0.0)
    c_dec = np.exp(lg * chunk)
    q_dec = np.broadcast_to(q_dec[:, :, None], (RET_HEADS, rows, RET_DV))
    k_dec = np.broadcast_to(k_dec[:, :, None], (RET_HEADS, rows, RET_DK))
    c_dec = np.broadcast_to(c_dec[:, None, None], (RET_HEADS, 1, RET_DV))
    return tuple(jnp.asarray(t, F32) for t in (intra, q_dec, k_dec, c_dec))


def _rotary(x, cs, sn):
    return x * cs + pltpu.roll(x, RET_DK // 2, axis=1) * sn


def _head_norm_gate(ret, g):
    mu = jnp.mean(ret, axis=-1, keepdims=True)
    cen = ret - mu
    var = jnp.mean(cen * cen, axis=-1, keepdims=True)
    return (g * jax.nn.sigmoid(g)) * (cen * lax.rsqrt(var + EPS))


def _ret_prompt_kernel(x_ref, mod_ref, nw_ref, wq_ref, cs_ref, sn_ref, intra_ref, qd_ref, kd_ref, cd_ref,
                       s5o_ref, wout_ref, o_ref, sout_ref, s_scr, *, tm):
    t = pl.program_id(1)

    @pl.when(t == 0)
    def _():
        s_scr[...] = jnp.zeros_like(s_scr)

    x = x_ref[0]
    m = mod_ref[0]
    h = _rms_mod(x, nw_ref[...], m[:, :D_MODEL], m[:, D_MODEL:2 * D_MODEL])
    proj = _dot(h.astype(BF16), wq_ref[...])
    cs, sn = cs_ref[...], sn_ref[...]
    n_chunks = tm // RET_CHUNK
    heads = range(RET_HEADS)
    chunks = range(n_chunks)

    lhs, vbs, kvs = {}, {}, {}
    for hd in heads:
        lo = hd * RET_DK
        q = _rotary(proj[:, lo:lo + RET_DK], cs, sn)
        k = _rotary(proj[:, RET_WIDTH + lo:RET_WIDTH + lo + RET_DK], cs, sn) * (RET_DK ** -0.5)
        v = proj[:, 2 * RET_WIDTH + lo:2 * RET_WIDTH + lo + RET_DV]
        for c in chunks:
            rows = slice(c * RET_CHUNK, (c + 1) * RET_CHUNK)
            qc, kc, vb = q[rows], k[rows], v[rows].astype(BF16)
            scores = _dot_nt(qc.astype(BF16), kc.astype(BF16)) * intra_ref[hd]
            lhs[hd, c] = jnp.concatenate([scores.astype(BF16), (qc * qd_ref[hd]).astype(BF16)], axis=-1)
            vbs[hd, c] = vb
            kvs[hd, c] = _dot(jnp.transpose(kc * kd_ref[hd]).astype(BF16), vb)

    states = {}
    for hd in heads:
        s = s_scr[hd]
        for c in chunks:
            states[hd, c] = s
            s = s * cd_ref[hd] + kvs[hd, c]
        s_scr[hd] = s

    parts = [s5o_ref[...]]
    for hd in heads:
        lo = 3 * RET_WIDTH + hd * RET_DV
        outs = [_dot(lhs[hd, c], jnp.concatenate([vbs[hd, c], states[hd, c].astype(BF16)], axis=0))
                for c in chunks]
        parts.append(_head_norm_gate(jnp.concatenate(outs, axis=0), proj[:, lo:lo + RET_DV]).astype(BF16))
    mix = jnp.concatenate(parts, axis=-1)
    o_ref[0] = x + m[:, 2 * D_MODEL:] * _dot(mix, wout_ref[...])

    @pl.when(t == pl.num_programs(1) - 1)
    def _():
        sout_ref[0] = s_scr[...]


def _ret_prompt(x1, mod3, norm_w, w_qkvg, cs, sn, decay, s5o, w_out, tm=1024):
    nb, seq, _ = x1.shape
    intra, q_dec, k_dec, c_dec = decay
    return pl.pallas_call(
        functools.partial(_ret_prompt_kernel, tm=tm),
        out_shape=(jax.ShapeDtypeStruct(x1.shape, F32),
                   jax.ShapeDtypeStruct((nb, RET_HEADS, RET_DK, RET_DV), F32)),
        grid=(nb, seq // tm),
        in_specs=[pl.BlockSpec((1, tm, D_MODEL), lambda b, t: (b, t, 0)),
                  pl.BlockSpec((1, 1, 3 * D_MODEL), lambda b, t: (b, 0, 1)),
                  _resident((1, D_MODEL)),
                  _resident((D_MODEL, 4 * RET_WIDTH)),
                  pl.BlockSpec((tm, RET_DK), lambda b, t: (t, 0)),
                  pl.BlockSpec((tm, RET_DK), lambda b, t: (t, 0)),
                  _resident(intra.shape), _resident(q_dec.shape), _resident(k_dec.shape), _resident(c_dec.shape),
                  pl.BlockSpec((tm, S5_WIDTH), lambda b, t: (t, b)),
                  _resident((D_MODEL, D_MODEL))],
        out_specs=(pl.BlockSpec((1, tm, D_MODEL), lambda b, t: (b, t, 0)),
                   pl.BlockSpec((1, RET_HEADS, RET_DK, RET_DV), lambda b, t: (b, 0, 0, 0))),
        scratch_shapes=[pltpu.VMEM((RET_HEADS, RET_DK, RET_DV), F32)],
        compiler_params=_params(("arbitrary", "arbitrary")),
        name="ret_prompt",
    )(x1, mod3, norm_w.reshape(1, D_MODEL), w_qkvg, cs, sn, intra, q_dec, k_dec, c_dec, s5o, w_out)


def _mix_in_sample_kernel(x_ref, mod_ref, nw_ref, wu_ref, wq_ref, wz_ref, lam2_ref, ws_ref, wk_ref, d_ref,
                          wglu_ref, h0re_ref, h0im_ref, qkvg_ref, s5o_ref, hre_ref, him_ref, *, steps, nb):
    x = x_ref[...]
    m = mod_ref[...]
    h = _rms_mod(x, nw_ref[...], m[..., :D_MODEL], m[..., D_MODEL:2 * D_MODEL])
    hb = h.reshape(steps * nb, D_MODEL).astype(BF16)
    proj = _dot(hb, wq_ref[...])
    qkvg_ref[...] = jnp.zeros_like(qkvg_ref)
    for s in range(4 * RET_HEADS):
        for t in range(steps):
            qkvg_ref[s, pl.ds(t, nb, stride=SUBLANES), :] = proj[t * nb:(t + 1) * nb, s * LANES:(s + 1) * LANES]
    u = _dot(hb, wu_ref[...])
    nk = steps // 2

    def rows_of(parity, lanes):
        return jnp.concatenate([u[(2 * k + parity) * nb:(2 * k + parity + 1) * nb, lanes] for k in range(nk)],
                               axis=0)

    u_ev, u_od, u_cat = [], [], []
    for s in range(S5_LANE_TILES):
        lanes = slice(s * LANES, (s + 1) * LANES)
        u_ev.append(rows_of(0, lanes))
        u_od.append(rows_of(1, lanes))
        u_cat.append(jnp.concatenate([u_ev[s], u_od[s]], axis=-1).astype(BF16))

    prev = []
    for j in range(S5_PAIRS):
        z = _dot(u_cat[j // 4], wz_ref[j])
        lr = lam2_ref[0, j][:1]
        li = lam2_ref[1, j][:1]
        sre = h0re_ref[:, j * LANES:(j + 1) * LANES]
        sim = h0im_ref[:, j * LANES:(j + 1) * LANES]
        pre, pim = [], []
        for k in range(nk):
            pre.append(sre)
            pim.append(sim)
            zre, zim = z[k * nb:(k + 1) * nb, :LANES], z[k * nb:(k + 1) * nb, LANES:]
            sre, sim = lr * sre - li * sim + zre, lr * sim + li * sre + zim
        hre_ref[:, j * LANES:(j + 1) * LANES] = sre
        him_ref[:, j * LANES:(j + 1) * LANES] = sim
        prev.append(jnp.concatenate([jnp.concatenate(pre, axis=0), jnp.concatenate(pim, axis=0)],
                                    axis=-1).astype(BF16))

    y_ev, y_od = [], []
    for jt in range(S5_LANE_TILES):
        acc = _dot(u_cat[jt], wk_ref[jt])
        for jj in range(4):
            acc = acc + _dot_nt(prev[4 * jt + jj], ws_ref[4 * jt + jj])
        d = d_ref[:, jt * LANES:(jt + 1) * LANES]
        y_ev.append(jax.nn.gelu(acc[:, :LANES] + d * u_ev[jt]))
        y_od.append(jax.nn.gelu(acc[:, LANES:] + d * u_od[jt]))
    y_ev = jnp.concatenate(y_ev, axis=-1)
    y_od = jnp.concatenate(y_od, axis=-1)
    y = jnp.concatenate([part[k * nb:(k + 1) * nb] for k in range(nk) for part in (y_ev, y_od)], axis=0)
    glu = _dot(y.astype(BF16), wglu_ref[...])
    s5o_ref[...] = (glu[:, :S5_WIDTH] * jax.nn.sigmoid(glu[:, S5_WIDTH:])).astype(BF16)


def _mix_in_sample(x1, mod3, norm_w, w_u, w_qkvg, two_step, d_skip, w_glu, h0_re, h0_im):
    steps, nb, _ = x1.shape
    rows = steps * nb
    n_state = S5_GROUPS * S5_STATE
    slabs = (4 * RET_HEADS, nb * SUBLANES, LANES)
    lam2_t, wz, ws_t, wk = two_step
    args = (x1, mod3, norm_w.reshape(1, D_MODEL), w_u, w_qkvg, wz, lam2_t, ws_t, wk,
            d_skip.reshape(1, S5_WIDTH), w_glu, h0_re, h0_im)
    in_specs = [pl.BlockSpec((steps, nb, D_MODEL), lambda i: (0, 0, 0)),
                pl.BlockSpec((1, nb, 3 * D_MODEL), lambda i: (0, 0, 1))]
    in_specs += [_resident(a.shape) for a in args[2:]]
    return pl.pallas_call(
        functools.partial(_mix_in_sample_kernel, steps=steps, nb=nb),
        out_shape=(jax.ShapeDtypeStruct(slabs, F32),
                   jax.ShapeDtypeStruct((rows, S5_WIDTH), BF16),
                   jax.ShapeDtypeStruct((nb, n_state), F32),
                   jax.ShapeDtypeStruct((nb, n_state), F32)),
        grid=(1,),
        in_specs=in_specs,
        out_specs=(pl.BlockSpec(slabs, lambda i: (0, 0, 0)),
                   pl.BlockSpec((rows, S5_WIDTH), lambda i: (0, 0)),
                   pl.BlockSpec((nb, n_state), lambda i: (0, 0)),
                   pl.BlockSpec((nb, n_state), lambda i: (0, 0))),
        compiler_params=_params(("arbitrary",)),
        name="mix_in_sample",
    )(*args)


def _ret_sample_kernel(qkvg_ref, s_ref, cs_ref, sn_ref, intra_ref, qd_ref, kd_ref, cd_ref,
                       o_ref, sout_ref, *, bb):
    cs, sn = cs_ref[...], sn_ref[...]
    pairs = [(b, hd) for b in range(bb) for hd in range(RET_HEADS)]

    def tile(kind, b, hd):
        return qkvg_ref[kind * RET_HEADS + hd, b * SUBLANES:(b + 1) * SUBLANES, :]

    scores, cross_lhs, vbs = {}, {}, {}
    for b, hd in pairs:
        q = _rotary(tile(0, b, hd), cs, sn)
        k = _rotary(tile(1, b, hd), cs, sn) * (RET_DK ** -0.5)
        vb = tile(2, b, hd).astype(BF16)
        scores[b, hd] = _dot_nt(q.astype(BF16), k.astype(BF16)) * intra_ref[hd]
        cross_lhs[b, hd] = (q * qd_ref[hd]).astype(BF16)
        vbs[b, hd] = vb
        sout_ref[b, hd] = s_ref[b, hd] * cd_ref[hd] + _dot(jnp.transpose(k * kd_ref[hd]).astype(BF16), vb)
    for b, hd in pairs:
        o = _dot(scores[b, hd].astype(BF16), vbs[b, hd]) + _dot(cross_lhs[b, hd], s_ref[b, hd].astype(BF16))
        o_ref[hd, b * SUBLANES:(b + 1) * SUBLANES, :] = _head_norm_gate(o, tile(3, b, hd))


def _ret_sample(qkvg, s0, cs, sn, decay, bb=8):
    nb = s0.shape[0]
    intra, q_dec, k_dec, c_dec = decay
    st = pl.BlockSpec((bb, RET_HEADS, RET_DK, RET_DV), lambda i: (i, 0, 0, 0))
    return pl.pallas_call(
        functools.partial(_ret_sample_kernel, bb=bb),
        out_shape=(jax.ShapeDtypeStruct((RET_HEADS, nb * SUBLANES, LANES), F32),
                   jax.ShapeDtypeStruct(s0.shape, F32)),
        grid=(nb // bb,),
        in_specs=[pl.BlockSpec((4 * RET_HEADS, bb * SUBLANES, LANES), lambda i: (0, i, 0)),
                  st, _resident(cs.shape), _resident(sn.shape),
                  _resident(intra.shape), _resident(q_dec.shape), _resident(k_dec.shape), _resident(c_dec.shape)],
        out_specs=(pl.BlockSpec((RET_HEADS, bb * SUBLANES, LANES), lambda i: (0, i, 0)), st),
        compiler_params=_params(("arbitrary",)),
        name="ret_sample",
    )(qkvg, s0, cs, sn, intra, q_dec, k_dec, c_dec)


def _mix_out_sample_kernel(x_ref, mod_ref, s5o_ref, ret_ref, wout_ref, o_ref, *, steps, nb):
    x = x_ref[...]
    gate = mod_ref[...][..., 2 * D_MODEL:]
    ret = jnp.concatenate(
        [jnp.concatenate([ret_ref[hd, pl.ds(t, nb, stride=SUBLANES), :] for hd in range(RET_HEADS)], axis=-1)
         for t in range(steps)], axis=0)
    mix = jnp.concatenate([s5o_ref[...], ret.astype(BF16)], axis=-1)
    o_ref[...] = x + gate * _dot(mix, wout_ref[...]).reshape(steps, nb, D_MODEL)


def _mix_out_sample(x1, mod3, s5o, ret, w_out):
    steps, nb, _ = x1.shape
    rows = steps * nb
    return pl.pallas_call(
        functools.partial(_mix_out_sample_kernel, steps=steps, nb=nb),
        out_shape=jax.ShapeDtypeStruct(x1.shape, F32),
        grid=(1,),
        in_specs=[pl.BlockSpec((steps, nb, D_MODEL), lambda i: (0, 0, 0)),
                  pl.BlockSpec((1, nb, 3 * D_MODEL), lambda i: (0, 0, 1)),
                  pl.BlockSpec((rows, S5_WIDTH), lambda i: (0, 0)),
                  pl.BlockSpec(ret.shape, lambda i: (0, 0, 0)),
                  _resident((D_MODEL, D_MODEL))],
        out_specs=pl.BlockSpec((steps, nb, D_MODEL), lambda i: (0, 0, 0)),
        compiler_params=_params(("arbitrary",)),
        name="mix_out_sample",
    )(x1, mod3, s5o, ret, w_out)


def kernel(x_prompt, x_sample, state_ssm_re, state_ssm_im, state_ret, c_prompt, c_sample,
           w_ada, b_ada, norm_ffn1, ffn1_w_in, ffn1_w_out, norm_mix, w_in_mix,
           s5_lambda_re, s5_lambda_im, s5_log_dt, s5_b_re, s5_b_im, s5_c_re, s5_c_im, s5_d, s5_w_glu,
           w_out_mix, norm_ffn2, ffn2_w_in, ffn2_w_out, w_ada_final, b_ada_final, norm_final):
    depth = w_ada.shape[0]
    bp, seq, _ = x_prompt.shape
    bs, steps, _ = x_sample.shape
    assert seq % RET_CHUNK == 0 and steps % 2 == 0 and steps <= SUBLANES

    c_rows = bs + bp
    c_pad = -c_rows % 16
    c_all = jnp.concatenate([c_sample, c_prompt, jnp.zeros((c_pad, D_MODEL), F32)], axis=0)

    cs_p, sn_p = _rotary_tables(np.arange(seq))
    decay_p = _decay_tables(RET_CHUNK, RET_CHUNK)
    rows_s = SUBLANES
    pos_s = np.concatenate([PAST_LEN + np.arange(steps), np.zeros(rows_s - steps)])
    cs_s, sn_s = _rotary_tables(pos_s)
    decay_s = _decay_tables(steps, rows_s)

    xp = x_prompt
    xs = jnp.transpose(x_sample, (1, 0, 2))
    outs = {k: [] for k in ("p_re", "p_im", "p_ret", "s_re", "s_im", "s_ret")}
    for l in range(depth):
        mod = _ada(c_all, w_ada[l], b_ada[l])
        mod_p = mod[bs:c_rows][:, None, :]
        mod_s = mod[None]
        w1_in, w1_out = ffn1_w_in[l].astype(BF16), ffn1_w_out[l].astype(BF16)
        w2_in, w2_out = ffn2_w_in[l].astype(BF16), ffn2_w_out[l].astype(BF16)
        w_u = w_in_mix[l][:, :S5_WIDTH].astype(BF16)
        w_qkvg = w_in_mix[l][:, S5_WIDTH:].astype(BF16)
        w_glu = s5_w_glu[l].astype(BF16)
        w_out = w_out_mix[l].astype(BF16)
        two_step = _s5_tables(s5_lambda_re[l], s5_lambda_im[l], s5_log_dt[l],
                              s5_b_re[l], s5_b_im[l], s5_c_re[l], s5_c_im[l])
        last = l == depth - 1
        if last:
            fin = _ada(c_all, w_ada_final, b_ada_final)
            fin_p, fin_s = fin[bs:c_rows][:, None, :], fin[None]
        else:
            fin_p = fin_s = None

        xp = _ffn(xp, mod_p, 0, norm_ffn1[l], w1_in, w1_out, (1, FFN_ROWS))
        s5o, hre, him = _s5_prompt(xp, mod_p, norm_mix[l], w_u, two_step, s5_d[l], w_glu)
        xp, sret = _ret_prompt(xp, mod_p, norm_mix[l], w_qkvg, cs_p, sn_p, decay_p, s5o, w_out)
        xp = _ffn(xp, mod_p, 2, norm_ffn2[l], w2_in, w2_out, (1, FFN_ROWS),
                  fin3=fin_p, norm_f=norm_final if last else None)
        outs["p_re"].append(hre.reshape(bp, S5_GROUPS, S5_STATE))
        outs["p_im"].append(him.reshape(bp, S5_GROUPS, S5_STATE))
        outs["p_ret"].append(sret)

        xs = _ffn(xs, mod_s, 0, norm_ffn1[l], w1_in, w1_out, (steps, bs))
        qkvg, s5o_s, hre_s, him_s = _mix_in_sample(
            xs, mod_s, norm_mix[l], w_u, w_qkvg, two_step, s5_d[l], w_glu,
            state_ssm_re[l].reshape(bs, -1), state_ssm_im[l].reshape(bs, -1))
        ret_s, sret_s = _ret_sample(qkvg, state_ret[l], cs_s, sn_s, decay_s)
        xs = _mix_out_sample(xs, mod_s, s5o_s, ret_s, w_out)
        xs = _ffn(xs, mod_s, 2, norm_ffn2[l], w2_in, w2_out, (steps, bs),
                  fin3=fin_s, norm_f=norm_final if last else None)
        outs["s_re"].append(hre_s.reshape(bs, S5_GROUPS, S5_STATE))
        outs["s_im"].append(him_s.reshape(bs, S5_GROUPS, S5_STATE))
        outs["s_ret"].append(sret_s)

    y_prompt = xp
    y_sample = jnp.transpose(xs, (1, 0, 2))
    return (y_prompt, y_sample, jnp.stack(outs["p_re"]), jnp.stack(outs["p_im"]), jnp.stack(outs["p_ret"]),
            jnp.stack(outs["s_re"]), jnp.stack(outs["s_im"]), jnp.stack(outs["s_ret"]))
```

```python
import functools

import numpy as np
import jax
import jax.numpy as jnp
from jax import lax
from jax.experimental import pallas as pl
from jax.experimental.pallas import tpu as pltpu

F32 = jnp.float32
BF16 = jnp.bfloat16

D_MODEL = 1024
D_FF = 2816
N_MOD = 9
S5_WIDTH = 512
S5_GROUP = 16
S5_GROUPS = 32
S5_STATE = 64
RET_HEADS = 4
RET_DK = 128
RET_DV = 128
RET_WIDTH = 512
RET_CHUNK = 128
PAST_LEN = 16384
ROPE_BASE = 10000.0
EPS = 1e-6

LANES = 128
SUBLANES = 8
S5_PAIRS = S5_GROUPS // 2
S5_LANE_TILES = S5_WIDTH // LANES
VMEM_LIMIT_BYTES = 56 * 1024 * 1024


def _dot(a, b):
    return jnp.dot(a, b, preferred_element_type=F32)


def _dot_nt(a, b):
    return lax.dot_general(a, b, (((1,), (1,)), ((), ())), preferred_element_type=F32)


def _params(semantics):
    return pltpu.CompilerParams(dimension_semantics=semantics, vmem_limit_bytes=VMEM_LIMIT_BYTES)


def _resident(shape):
    nd = len(shape)
    return pl.BlockSpec(shape, lambda *_: (0,) * nd, pipeline_mode=pl.Buffered(1))


def _rms_mod(x, norm_w, shift, scale):
    xn = x * lax.rsqrt(jnp.mean(x * x, axis=-1, keepdims=True) + EPS)
    return xn * (norm_w * (1.0 + scale)) + shift


ADA_K_ROWS = 128


def _ada_kernel(c_ref, w_ref, b_ref, o_ref):
    @pl.when(pl.program_id(0) == 0)
    def _():
        o_ref[...] = jnp.broadcast_to(b_ref[...], o_ref.shape)

    c = c_ref[...]
    s = (c * jax.nn.sigmoid(c)).astype(BF16)
    o_ref[...] += _dot(s, w_ref[...].astype(BF16))


def _ada(c, w, b):
    m, k = c.shape
    n = w.shape[1]
    return pl.pallas_call(
        _ada_kernel,
        out_shape=jax.ShapeDtypeStruct((m, n), F32),
        grid=(k // ADA_K_ROWS,),
        in_specs=[pl.BlockSpec((m, ADA_K_ROWS), lambda j: (0, j)),
                  pl.BlockSpec((ADA_K_ROWS, n), lambda j: (j, 0)),
                  _resident((1, n))],
        out_specs=pl.BlockSpec((m, n), lambda j: (0, 0)),
        compiler_params=_params(("arbitrary",)),
        name="ada_mod",
    )(c, w, b.reshape(1, n))


FF_CHUNK = 256
FF_CHUNKS = D_FF // FF_CHUNK
FF_GROUPS = ((0, 4), (4, 8), (8, FF_CHUNKS))
FFN_ROWS = 512


def _ffn_math(x, m, norm_w, win_scr, wout_scr, fin, norm_f):
    a_dim, r_dim, _ = x.shape
    shift, scale, gate = m[..., :D_MODEL], m[..., D_MODEL:2 * D_MODEL], m[..., 2 * D_MODEL:]
    hb = _rms_mod(x, norm_w, shift, scale).reshape(a_dim * r_dim, D_MODEL).astype(BF16)
    o = None
    for c0, c1 in FF_GROUPS:
        a = jnp.concatenate([_dot(hb, win_scr[c]) for c in range(c0, c1)], axis=-1)
        b = jnp.concatenate([_dot(hb, win_scr[FF_CHUNKS + c]) for c in range(c0, c1)], axis=-1)
        act = (a * jax.nn.sigmoid(a) * b).astype(BF16)
        part = _dot(act, wout_scr[c0:c1].reshape((c1 - c0) * FF_CHUNK, D_MODEL))
        o = part if o is None else o + part
    y = x + (0.5 * gate) * o.reshape(a_dim, r_dim, D_MODEL)
    if fin is not None:
        y = _rms_mod(y, norm_f, fin[..., :D_MODEL], fin[..., D_MODEL:])
    return y


def _ffn_kernel(*refs, n_prompt, final):
    if final:
        (xp_ref, mp_ref, xs_ref, ms_ref, nw_ref, wa_ref, wb_ref, wo_ref, fp_ref, fs_ref, nf_ref,
         op_ref, os_ref, win_scr, wout_scr) = refs
    else:
        xp_ref, mp_ref, xs_ref, ms_ref, nw_ref, wa_ref, wb_ref, wo_ref, op_ref, os_ref, win_scr, wout_scr = refs
        fp_ref = fs_ref = nf_ref = None
    i = pl.program_id(0)

    @pl.when(i < FF_CHUNKS)
    def _():
        win_scr[i] = wa_ref[...].astype(BF16)
        win_scr[FF_CHUNKS + i] = wb_ref[...].astype(BF16)
        wout_scr[i] = wo_ref[...].astype(BF16)

    def run(x_ref, m_ref, f_ref, o_ref):
        o_ref[...] = _ffn_math(x_ref[...], m_ref[...], nw_ref[...], win_scr, wout_scr,
                               f_ref[...] if final else None, nf_ref[...] if final else None)

    @pl.when((i >= FF_CHUNKS) & (i < FF_CHUNKS + n_prompt))
    def _():
        run(xp_ref, mp_ref, fp_ref, op_ref)

    @pl.when(i == FF_CHUNKS + n_prompt)
    def _():
        run(xs_ref, ms_ref, fs_ref, os_ref)


def _ffn(xp, mod_p, xs, mod_s, sub, norm_w, w_in, w_out, fin_p=None, fin_s=None, norm_f=None):
    bp, seq, _ = xp.shape
    steps, bs, _ = xs.shape
    tiles = seq // FFN_ROWS
    n_prompt = bp * tiles
    final = fin_p is not None

    def chunk(i):
        return jnp.minimum(i, FF_CHUNKS - 1)

    def tile(i):
        return jnp.clip(i - FF_CHUNKS, 0, n_prompt - 1)

    tok_p = pl.BlockSpec((1, FFN_ROWS, D_MODEL), lambda i: (tile(i) // tiles, tile(i) % tiles, 0))
    tok_s = pl.BlockSpec((steps, bs, D_MODEL), lambda i: (0, 0, 0))
    in_specs = [tok_p,
                pl.BlockSpec((1, 1, 3 * D_MODEL), lambda i: (tile(i) // tiles, 0, sub)),
                tok_s,
                pl.BlockSpec((1, bs, 3 * D_MODEL), lambda i: (0, 0, sub)),
                _resident((1, D_MODEL)),
                pl.BlockSpec((D_MODEL, FF_CHUNK), lambda i: (0, chunk(i))),
                pl.BlockSpec((D_MODEL, FF_CHUNK), lambda i: (0, FF_CHUNKS + chunk(i))),
                pl.BlockSpec((FF_CHUNK, D_MODEL), lambda i: (chunk(i), 0))]
    args = [xp, mod_p, xs, mod_s, norm_w.reshape(1, D_MODEL), w_in, w_in, w_out]
    if final:
        in_specs += [pl.BlockSpec((1, 1, 2 * D_MODEL), lambda i: (tile(i) // tiles, 0, 0)),
                     pl.BlockSpec((1, bs, 2 * D_MODEL), lambda i: (0, 0, 0)),
                     _resident((1, D_MODEL))]
        args += [fin_p, fin_s, norm_f.reshape(1, D_MODEL)]
    return pl.pallas_call(
        functools.partial(_ffn_kernel, n_prompt=n_prompt, final=final),
        out_shape=(jax.ShapeDtypeStruct(xp.shape, F32), jax.ShapeDtypeStruct(xs.shape, F32)),
        grid=(FF_CHUNKS + n_prompt + 1,),
        in_specs=in_specs,
        out_specs=(tok_p, tok_s),
        scratch_shapes=[pltpu.VMEM((2 * FF_CHUNKS, D_MODEL, FF_CHUNK), BF16),
                        pltpu.VMEM((FF_CHUNKS, FF_CHUNK, D_MODEL), BF16)],
        compiler_params=_params(("arbitrary",)),
        name="ffn_final" if final else "ffn",
    )(*args)


def _s5_tables(lam_re, lam_im, log_dt, b_re, b_im, c_re, c_im):
    lr, li = lam_re.astype(F32), lam_im.astype(F32)
    dt = jnp.exp(log_dt.astype(F32))[:, None]
    mag = jnp.exp(lr * dt)
    ar, ai = mag * jnp.cos(li * dt), mag * jnp.sin(li * dt)
    a2r, a2i = ar * ar - ai * ai, 2.0 * ar * ai
    den = lr * lr + li * li
    cr = ((ar - 1.0) * lr + ai * li) / den
    ci = (ai * lr - (ar - 1.0) * li) / den
    br, bi = b_re.astype(F32), b_im.astype(F32)
    bb_re = cr[:, :, None] * br - ci[:, :, None] * bi
    bb_im = cr[:, :, None] * bi + ci[:, :, None] * br
    lb_re = ar[:, :, None] * bb_re - ai[:, :, None] * bb_im
    lb_im = ar[:, :, None] * bb_im + ai[:, :, None] * bb_re
    cre, cim = c_re.astype(F32), c_im.astype(F32)
    c1_re = cre * ar[:, None, :] - cim * ai[:, None, :]
    c1_im = cre * ai[:, None, :] + cim * ar[:, None, :]
    c2_re = cre * a2r[:, None, :] - cim * a2i[:, None, :]
    c2_im = cre * a2i[:, None, :] + cim * a2r[:, None, :]

    lam2_t = jnp.broadcast_to(jnp.stack([a2r, a2i]).reshape(2, S5_PAIRS, 1, LANES),
                              (2, S5_PAIRS, SUBLANES, LANES))

    groups_per_tile = LANES // S5_GROUP
    q_of = 2 * (np.arange(S5_PAIRS) % 4)[:, None] + np.arange(2)[None, :]
    slot = (np.arange(groups_per_tile)[None, :, None] == q_of[:, None, :])
    slot = np.broadcast_to(slot[:, :, None, :, None], (S5_PAIRS, groups_per_tile, 2, 2, S5_STATE))
    slot = jnp.asarray(slot.reshape(S5_PAIRS, 1, groups_per_tile, 1, 2 * LANES), F32)

    def place(parts):
        v = jnp.stack([jnp.stack(p) for p in parts])
        v = v.reshape(2, 2, S5_PAIRS, 2, S5_GROUP, S5_STATE)
        v = jnp.transpose(v, (2, 0, 4, 1, 3, 5)).reshape(S5_PAIRS, 2, 1, S5_GROUP, 2 * LANES)
        return (v * slot).reshape(S5_PAIRS, 2 * LANES, 2 * LANES).astype(BF16)

    def cn(t):
        return jnp.transpose(t, (0, 2, 1))

    wz = place([[cn(lb_re), cn(lb_im)], [cn(bb_re), cn(bb_im)]])
    ws_t = place([[c1_re, -c1_im], [c2_re, -c2_im]])

    c_cat = jnp.concatenate([cre, -cim], axis=-1)
    x_cat = jnp.stack([jnp.concatenate([bb_re, bb_im], axis=1),
                       jnp.concatenate([lb_re, lb_im], axis=1)])
    k = jnp.einsum('gom,xgmi->xgio', c_cat, x_cat, precision=lax.Precision.HIGHEST)
    k = k.reshape(2, S5_LANE_TILES, groups_per_tile, 1, S5_GROUP, S5_GROUP)
    eye = jnp.asarray(np.eye(groups_per_tile)[None, None, :, :, None, None], F32)
    k = jnp.transpose(eye * k, (0, 1, 2, 4, 3, 5)).reshape(2, S5_LANE_TILES, LANES, LANES)
    k0, k1 = k[0], k[1]
    wk = jnp.concatenate([jnp.concatenate([k0, k1], axis=-1),
                          jnp.concatenate([jnp.zeros_like(k0), k0], axis=-1)], axis=1).astype(BF16)
    return lam2_t, wz, ws_t, wk


def _s5_prompt_kernel(x_ref, mod_ref, nw_ref, wu_ref, wz_ref, lam2_ref, ws_ref, wk_ref, d_ref, wglu_ref,
                      o_ref, hre_ref, him_ref,
                      u_slab, sre, sim, st_re, st_im, y_slab, *, tl, nb):
    i = pl.program_id(0)
    nk = tl // 2
    half = nk * nb

    @pl.when(i == 0)
    def _():
        st_re[...] = jnp.zeros_like(st_re)
        st_im[...] = jnp.zeros_like(st_im)

    x = x_ref[...]
    m = mod_ref[...]
    h = _rms_mod(x, nw_ref[...], m[..., :D_MODEL], m[..., D_MODEL:2 * D_MODEL])
    u = _dot(h.reshape(nb * tl, D_MODEL).astype(BF16), wu_ref[...])

    for s in range(S5_LANE_TILES):
        for b in range(nb):
            u_slab[s, pl.ds(b, tl, stride=nb), :] = u[b * tl:(b + 1) * tl, s * LANES:(s + 1) * LANES]

    u_ev, u_od, u_cat = [], [], []
    for s in range(S5_LANE_TILES):
        tiles = u_slab[s].reshape(nk, 2 * nb, LANES)
        ev = tiles[:, :nb, :].reshape(half, LANES)
        od = tiles[:, nb:, :].reshape(half, LANES)
        u_ev.append(ev)
        u_od.append(od)
        u_cat.append(jnp.concatenate([ev, od], axis=-1).astype(BF16))

    for j in range(S5_PAIRS):
        z = _dot(u_cat[j // 4], wz_ref[j])
        sre[j, 0:nb, :] = st_re[j]
        sim[j, 0:nb, :] = st_im[j]
        sre[j, nb:nb + half, :] = z[:, :LANES]
        sim[j, nb:nb + half, :] = z[:, LANES:]

    pairs_per_pass = 8
    for j0 in range(0, S5_PAIRS, pairs_per_pass):
        js = range(j0, j0 + pairs_per_pass)
        lr = [lam2_ref[0, j] for j in js]
        li = [lam2_ref[1, j] for j in js]
        carry = []
        for j in js:
            carry += [st_re[j], st_im[j]]
        for k in range(nk):
            r0 = (k + 1) * nb
            for n, j in enumerate(js):
                re, im = carry[2 * n], carry[2 * n + 1]
                nre = lr[n] * re - li[n] * im + sre[j, r0:r0 + nb, :]
                nim = lr[n] * im + li[n] * re + sim[j, r0:r0 + nb, :]
                if k + 1 < nk:
                    sre[j, r0:r0 + nb, :] = nre
                    sim[j, r0:r0 + nb, :] = nim
                carry[2 * n], carry[2 * n + 1] = nre, nim
        for n, j in enumerate(js):
            st_re[j] = carry[2 * n]
            st_im[j] = carry[2 * n + 1]

    y_ev, y_od = [], []
    for jt in range(S5_LANE_TILES):
        acc = _dot(u_cat[jt], wk_ref[jt])
        for jj in range(4):
            j = 4 * jt + jj
            sp = jnp.concatenate([sre[j, 0:half, :], sim[j, 0:half, :]], axis=-1).astype(BF16)
            acc = acc + _dot_nt(sp, ws_ref[j])
        d = d_ref[:, jt * LANES:(jt + 1) * LANES]
        y_ev.append(jax.nn.gelu(acc[:, :LANES] + d * u_ev[jt]))
        y_od.append(jax.nn.gelu(acc[:, LANES:] + d * u_od[jt]))
    y_ev = jnp.concatenate(y_ev, axis=-1).reshape(nk, nb, S5_WIDTH)
    y_od = jnp.concatenate(y_od, axis=-1).reshape(nk, nb, S5_WIDTH)
    y = jnp.concatenate([y_ev, y_od], axis=1).reshape(tl * nb, S5_WIDTH)
    glu = _dot(y.astype(BF16), wglu_ref[...])
    s5o = glu[:, :S5_WIDTH] * jax.nn.sigmoid(glu[:, S5_WIDTH:])
    for s in range(S5_LANE_TILES):
        y_slab[s] = s5o[:, s * LANES:(s + 1) * LANES]
    for b in range(nb):
        for s in range(S5_LANE_TILES):
            c0 = b * S5_WIDTH + s * LANES
            o_ref[:, c0:c0 + LANES] = y_slab[s, pl.ds(b, tl, stride=nb), :].astype(BF16)

    @pl.when(i == pl.num_programs(0) - 1)
    def _():
        hre_ref[...] = jnp.concatenate([st_re[j] for j in range(S5_PAIRS)], axis=-1)
        him_ref[...] = jnp.concatenate([st_im[j] for j in range(S5_PAIRS)], axis=-1)


def _s5_prompt(x1, mod3, norm_w, w_u, two_step, d_skip, w_glu, tl=128):
    nb, seq, _ = x1.shape
    lam2_t, wz, ws_t, wk = two_step
    rows = nb * tl
    half = rows // 2
    n_state = S5_GROUPS * S5_STATE
    return pl.pallas_call(
        functools.partial(_s5_prompt_kernel, tl=tl, nb=nb),
        out_shape=(jax.ShapeDtypeStruct((seq, nb * S5_WIDTH), BF16),
                   jax.ShapeDtypeStruct((nb, n_state), F32),
                   jax.ShapeDtypeStruct((nb, n_state), F32)),
        grid=(seq // tl,),
        in_specs=[pl.BlockSpec((nb, tl, D_MODEL), lambda i: (0, i, 0)),
                  pl.BlockSpec((nb, 1, 3 * D_MODEL), lambda i: (0, 0, 1)),
                  _resident((1, D_MODEL)),
                  _resident((D_MODEL, S5_WIDTH)),
                  _resident(wz.shape), _resident(lam2_t.shape), _resident(ws_t.shape), _resident(wk.shape),
                  _resident((1, S5_WIDTH)),
                  _resident((S5_WIDTH, 2 * S5_WIDTH))],
        out_specs=(pl.BlockSpec((tl, nb * S5_WIDTH), lambda i: (i, 0)),
                   pl.BlockSpec((nb, n_state), lambda i: (0, 0)),
                   pl.BlockSpec((nb, n_state), lambda i: (0, 0))),
        scratch_shapes=[pltpu.VMEM((S5_LANE_TILES, rows, LANES), F32),
                        pltpu.VMEM((S5_PAIRS, nb + half, LANES), F32),
                        pltpu.VMEM((S5_PAIRS, nb + half, LANES), F32),
                        pltpu.VMEM((S5_PAIRS, nb, LANES), F32),
                        pltpu.VMEM((S5_PAIRS, nb, LANES), F32),
                        pltpu.VMEM((S5_LANE_TILES, rows, LANES), F32)],
        compiler_params=_params(("arbitrary",)),
        name="s5_prompt",
    )(x1, mod3, norm_w.reshape(1, D_MODEL), w_u, wz, lam2_t, ws_t, wk, d_skip.reshape(1, S5_WIDTH), w_glu)


def _rotary_tables(pos):
    half = RET_DK // 2
    inv = ROPE_BASE ** (-np.arange(half, dtype=np.float64) / half)
    ang = np.asarray(pos, np.float64)[:, None] * inv[None, :]
    cos, sin = np.cos(ang), np.sin(ang)
    return (jnp.asarray(np.concatenate([cos, cos], axis=-1), F32),
            jnp.asarray(np.concatenate([-sin, sin], axis=-1), F32))


def _decay_tables(chunk, rows):
    lg = np.log1p(-np.exp2(-5.0 - np.arange(RET_HEADS, dtype=np.float64)))
    idx = np.arange(rows, dtype=np.float64)
    valid = idx < chunk
    diff = idx[:, None] - idx[None, :]
    intra = np.where((diff >= 0) & valid[:, None] & valid[None, :],
                     np.exp(lg[:, None, None] * np.maximum(diff, 0.0)), 0.0)
    q_dec = np.where(valid[None, :], np.exp(lg[:, None] * (idx[None, :] + 1.0)), 0.0)
    k_dec = np.where(valid[None, :], np.exp(lg[:, None] * (chunk - 1.0 - idx)[None, :]), 0.0)
    c_dec = np.exp(lg * chunk)
    q_dec = np.broadcast_to(q_dec[:, :, None], (RET_HEADS, rows, RET_DV))
    k_dec = np.broadcast_to(k_dec[:, :, None], (RET_HEADS, rows, RET_DK))
    c_dec = np.broadcast_to(c_dec[:, None, None], (RET_HEADS, 1, RET_DV))
    return tuple(jnp.asarray(t, F32) for t in (intra, q_dec, k_dec, c_dec))


def _rotary(x, cs, sn):
    return x * cs + pltpu.roll(x, RET_DK // 2, axis=1) * sn


def _head_norm_gate(ret, g):
    mu = jnp.mean(ret, axis=-1, keepdims=True)
    cen = ret - mu
    var = jnp.mean(cen * cen, axis=-1, keepdims=True)
    return (g * jax.nn.sigmoid(g)) * (cen * lax.rsqrt(var + EPS))


def _ret_prompt_kernel(x_ref, mod_ref, nw_ref, wq_ref, cs_ref, sn_ref, intra_ref, qd_ref, kd_ref, cd_ref,
                       s5o_ref, wout_ref, o_ref, sout_ref, s_scr, *, tm):
    t = pl.program_id(1)

    @pl.when(t == 0)
    def _():
        s_scr[...] = jnp.zeros_like(s_scr)

    x = x_ref[0]
    m = mod_ref[0]
    h = _rms_mod(x, nw_ref[...], m[:, :D_MODEL], m[:, D_MODEL:2 * D_MODEL])
    proj = _dot(h.astype(BF16), wq_ref[...])
    cs, sn = cs_ref[...], sn_ref[...]
    n_chunks = tm // RET_CHUNK
    heads = range(RET_HEADS)
    chunks = range(n_chunks)

    lhs, vbs, kvs = {}, {}, {}
    for hd in heads:
        lo = hd * RET_DK
        q = _rotary(proj[:, lo:lo + RET_DK], cs, sn)
        k = _rotary(proj[:, RET_WIDTH + lo:RET_WIDTH + lo + RET_DK], cs, sn) * (RET_DK ** -0.5)
        v = proj[:, 2 * RET_WIDTH + lo:2 * RET_WIDTH + lo + RET_DV]
        for c in chunks:
            rows = slice(c * RET_CHUNK, (c + 1) * RET_CHUNK)
            qc, kc, vb = q[rows], k[rows], v[rows].astype(BF16)
            scores = _dot_nt(qc.astype(BF16), kc.astype(BF16)) * intra_ref[hd]
            lhs[hd, c] = jnp.concatenate([scores.astype(BF16), (qc * qd_ref[hd]).astype(BF16)], axis=-1)
            vbs[hd, c] = vb
            kvs[hd, c] = _dot(jnp.transpose(kc * kd_ref[hd]).astype(BF16), vb)

    states = {}
    for hd in heads:
        s = s_scr[hd]
        for c in chunks:
            states[hd, c] = s
            s = s * cd_ref[hd] + kvs[hd, c]
        s_scr[hd] = s

    parts = [s5o_ref[...]]
    for hd in heads:
        lo = 3 * RET_WIDTH + hd * RET_DV
        outs = [_dot(lhs[hd, c], jnp.concatenate([vbs[hd, c], states[hd, c].astype(BF16)], axis=0))
                for c in chunks]
        parts.append(_head_norm_gate(jnp.concatenate(outs, axis=0), proj[:, lo:lo + RET_DV]).astype(BF16))
    mix = jnp.concatenate(parts, axis=-1)
    o_ref[0] = x + m[:, 2 * D_MODEL:] * _dot(mix, wout_ref[...])

    @pl.when(t == pl.num_programs(1) - 1)
    def _():
        sout_ref[0] = s_scr[...]


def _ret_prompt(x1, mod3, norm_w, w_qkvg, cs, sn, decay, s5o, w_out, tm=1024):
    nb, seq, _ = x1.shape
    intra, q_dec, k_dec, c_dec = decay
    return pl.pallas_call(
        functools.partial(_ret_prompt_kernel, tm=tm),
        out_shape=(jax.ShapeDtypeStruct(x1.shape, F32),
                   jax.ShapeDtypeStruct((nb, RET_HEADS, RET_DK, RET_DV), F32)),
        grid=(nb, seq // tm),
        in_specs=[pl.BlockSpec((1, tm, D_MODEL), lambda b, t: (b, t, 0)),
                  pl.BlockSpec((1, 1, 3 * D_MODEL), lambda b, t: (b, 0, 1)),
                  _resident((1, D_MODEL)),
                  _resident((D_MODEL, 4 * RET_WIDTH)),
                  pl.BlockSpec((tm, RET_DK), lambda b, t: (t, 0)),
                  pl.BlockSpec((tm, RET_DK), lambda b, t: (t, 0)),
                  _resident(intra.shape), _resident(q_dec.shape), _resident(k_dec.shape), _resident(c_dec.shape),
                  pl.BlockSpec((tm, S5_WIDTH), lambda b, t: (t, b)),
                  _resident((D_MODEL, D_MODEL))],
        out_specs=(pl.BlockSpec((1, tm, D_MODEL), lambda b, t: (b, t, 0)),
                   pl.BlockSpec((1, RET_HEADS, RET_DK, RET_DV), lambda b, t: (b, 0, 0, 0))),
        scratch_shapes=[pltpu.VMEM((RET_HEADS, RET_DK, RET_DV), F32)],
        compiler_params=_params(("arbitrary", "arbitrary")),
        name="ret_prompt",
    )(x1, mod3, norm_w.reshape(1, D_MODEL), w_qkvg, cs, sn, intra, q_dec, k_dec, c_dec, s5o, w_out)


def _mix_in_sample_kernel(x_ref, mod_ref, nw_ref, wu_ref, wq_ref, wz_ref, lam2_ref, ws_ref, wk_ref, d_ref,
                          wglu_ref, h0re_ref, h0im_ref, qkvg_ref, s5o_ref, hre_ref, him_ref, *, steps, nb):
    x = x_ref[...]
    m = mod_ref[...]
    h = _rms_mod(x, nw_ref[...], m[..., :D_MODEL], m[..., D_MODEL:2 * D_MODEL])
    hb = h.reshape(steps * nb, D_MODEL).astype(BF16)
    proj = _dot(hb, wq_ref[...])
    qkvg_ref[...] = jnp.zeros_like(qkvg_ref)
    for s in range(4 * RET_HEADS):
        for t in range(steps):
            qkvg_ref[s, pl.ds(t, nb, stride=SUBLANES), :] = proj[t * nb:(t + 1) * nb, s * LANES:(s + 1) * LANES]
    u = _dot(hb, wu_ref[...])
    nk = steps // 2

    def rows_of(parity, lanes):
        return jnp.concatenate([u[(2 * k + parity) * nb:(2 * k + parity + 1) * nb, lanes] for k in range(nk)],
                               axis=0)

    u_ev, u_od, u_cat = [], [], []
    for s in range(S5_LANE_TILES):
        lanes = slice(s * LANES, (s + 1) * LANES)
        u_ev.append(rows_of(0, lanes))
        u_od.append(rows_of(1, lanes))
        u_cat.append(jnp.concatenate([u_ev[s], u_od[s]], axis=-1).astype(BF16))

    prev = []
    for j in range(S5_PAIRS):
        z = _dot(u_cat[j // 4], wz_ref[j])
        lr = lam2_ref[0, j][:1]
        li = lam2_ref[1, j][:1]
        sre = h0re_ref[:, j * LANES:(j + 1) * LANES]
        sim = h0im_ref[:, j * LANES:(j + 1) * LANES]
        pre, pim = [], []
        for k in range(nk):
            pre.append(sre)
            pim.append(sim)
            zre, zim = z[k * nb:(k + 1) * nb, :LANES], z[k * nb:(k + 1) * nb, LANES:]
            sre, sim = lr * sre - li * sim + zre, lr * sim + li * sre + zim
        hre_ref[:, j * LANES:(j + 1) * LANES] = sre
        him_ref[:, j * LANES:(j + 1) * LANES] = sim
        prev.append(jnp.concatenate([jnp.concatenate(pre, axis=0), jnp.concatenate(pim, axis=0)],
                                    axis=-1).astype(BF16))

    y_ev, y_od = [], []
    for jt in range(S5_LANE_TILES):
        acc = _dot(u_cat[jt], wk_ref[jt])
        for jj in range(4):
            acc = acc + _dot_nt(prev[4 * jt + jj], ws_ref[4 * jt + jj])
        d = d_ref[:, jt * LANES:(jt + 1) * LANES]
        y_ev.append(jax.nn.gelu(acc[:, :LANES] + d * u_ev[jt]))
        y_od.append(jax.nn.gelu(acc[:, LANES:] + d * u_od[jt]))
    y_ev = jnp.concatenate(y_ev, axis=-1)
    y_od = jnp.concatenate(y_od, axis=-1)
    y = jnp.concatenate([part[k * nb:(k + 1) * nb] for k in range(nk) for part in (y_ev, y_od)], axis=0)
    glu = _dot(y.astype(BF16), wglu_ref[...])
    s5o_ref[...] = (glu[:, :S5_WIDTH] * jax.nn.sigmoid(glu[:, S5_WIDTH:])).astype(BF16)


def _mix_in_sample(x1, mod3, norm_w, w_u, w_qkvg, two_step, d_skip, w_glu, h0_re, h0_im):
    steps, nb, _ = x1.shape
    rows = steps * nb
    n_state = S5_GROUPS * S5_STATE
    slabs = (4 * RET_HEADS, nb * SUBLANES, LANES)
    lam2_t, wz, ws_t, wk = two_step
    args = (x1, mod3, norm_w.reshape(1, D_MODEL), w_u, w_qkvg, wz, lam2_t, ws_t, wk,
            d_skip.reshape(1, S5_WIDTH), w_glu, h0_re, h0_im)
    in_specs = [pl.BlockSpec((steps, nb, D_MODEL), lambda i: (0, 0, 0)),
                pl.BlockSpec((1, nb, 3 * D_MODEL), lambda i: (0, 0, 1))]
    in_specs += [_resident(a.shape) for a in args[2:]]
    return pl.pallas_call(
        functools.partial(_mix_in_sample_kernel, steps=steps, nb=nb),
        out_shape=(jax.ShapeDtypeStruct(slabs, F32),
                   jax.ShapeDtypeStruct((rows, S5_WIDTH), BF16),
                   jax.ShapeDtypeStruct((nb, n_state), F32),
                   jax.ShapeDtypeStruct((nb, n_state), F32)),
        grid=(1,),
        in_specs=in_specs,
        out_specs=(pl.BlockSpec(slabs, lambda i: (0, 0, 0)),
                   pl.BlockSpec((rows, S5_WIDTH), lambda i: (0, 0)),
                   pl.BlockSpec((nb, n_state), lambda i: (0, 0)),
                   pl.BlockSpec((nb, n_state), lambda i: (0, 0))),
        compiler_params=_params(("arbitrary",)),
        name="mix_in_sample",
    )(*args)


def _ret_sample_kernel(qkvg_ref, s_ref, cs_ref, sn_ref, intra_ref, qd_ref, kd_ref, cd_ref,
                       o_ref, sout_ref, *, bb):
    cs, sn = cs_ref[...], sn_ref[...]
    pairs = [(b, hd) for b in range(bb) for hd in range(RET_HEADS)]

    def tile(kind, b, hd):
        return qkvg_ref[kind * RET_HEADS + hd, b * SUBLANES:(b + 1) * SUBLANES, :]

    scores, cross_lhs, vbs = {}, {}, {}
    for b, hd in pairs:
        q = _rotary(tile(0, b, hd), cs, sn)
        k = _rotary(tile(1, b, hd), cs, sn) * (RET_DK ** -0.5)
        vb = tile(2, b, hd).astype(BF16)
        scores[b, hd] = _dot_nt(q.astype(BF16), k.astype(BF16)) * intra_ref[hd]
        cross_lhs[b, hd] = (q * qd_ref[hd]).astype(BF16)
        vbs[b, hd] = vb
        sout_ref[b, hd] = s_ref[b, hd] * cd_ref[hd] + _dot(jnp.transpose(k * kd_ref[hd]).astype(BF16), vb)
    for b, hd in pairs:
        o = _dot(scores[b, hd].astype(BF16), vbs[b, hd]) + _dot(cross_lhs[b, hd], s_ref[b, hd].astype(BF16))
        o_ref[hd, b * SUBLANES:(b + 1) * SUBLANES, :] = _head_norm_gate(o, tile(3, b, hd))


def _ret_sample(qkvg, s0, cs, sn, decay, bb=16):
    nb = s0.shape[0]
    intra, q_dec, k_dec, c_dec = decay
    st = pl.BlockSpec((bb, RET_HEADS, RET_DK, RET_DV), lambda i: (i, 0, 0, 0))
    return pl.pallas_call(
        functools.partial(_ret_sample_kernel, bb=bb),
        out_shape=(jax.ShapeDtypeStruct((RET_HEADS, nb * SUBLANES, LANES), F32),
                   jax.ShapeDtypeStruct(s0.shape, F32)),
        grid=(nb // bb,),
        in_specs=[pl.BlockSpec((4 * RET_HEADS, bb * SUBLANES, LANES), lambda i: (0, i, 0)),
                  st, _resident(cs.shape), _resident(sn.shape),
                  _resident(intra.shape), _resident(q_dec.shape), _resident(k_dec.shape), _resident(c_dec.shape)],
        out_specs=(pl.BlockSpec((RET_HEADS, bb * SUBLANES, LANES), lambda i: (0, i, 0)), st),
        compiler_params=_params(("arbitrary",)),
        name="ret_sample",
    )(qkvg, s0, cs, sn, intra, q_dec, k_dec, c_dec)


def _mix_out_sample_kernel(x_ref, mod_ref, s5o_ref, ret_ref, wout_ref, o_ref, *, steps, nb):
    x = x_ref[...]
    gate = mod_ref[...][..., 2 * D_MODEL:]
    ret = jnp.concatenate(
        [jnp.concatenate([ret_ref[hd, pl.ds(t, nb, stride=SUBLANES), :] for hd in range(RET_HEADS)], axis=-1)
         for t in range(steps)], axis=0)
    mix = jnp.concatenate([s5o_ref[...], ret.astype(BF16)], axis=-1)
    o_ref[...] = x + gate * _dot(mix, wout_ref[...]).reshape(steps, nb, D_MODEL)


def _mix_out_sample(x1, mod3, s5o, ret, w_out):
    steps, nb, _ = x1.shape
    rows = steps * nb
    return pl.pallas_call(
        functools.partial(_mix_out_sample_kernel, steps=steps, nb=nb),
        out_shape=jax.ShapeDtypeStruct(x1.shape, F32),
        grid=(1,),
        in_specs=[pl.BlockSpec((steps, nb, D_MODEL), lambda i: (0, 0, 0)),
                  pl.BlockSpec((1, nb, 3 * D_MODEL), lambda i: (0, 0, 1)),
                  pl.BlockSpec((rows, S5_WIDTH), lambda i: (0, 0)),
                  pl.BlockSpec(ret.shape, lambda i: (0, 0, 0)),
                  _resident((D_MODEL, D_MODEL))],
        out_specs=pl.BlockSpec((steps, nb, D_MODEL), lambda i: (0, 0, 0)),
        compiler_params=_params(("arbitrary",)),
        name="mix_out_sample",
    )(x1, mod3, s5o, ret, w_out)


def kernel(x_prompt, x_sample, state_ssm_re, state_ssm_im, state_ret, c_prompt, c_sample,
           w_ada, b_ada, norm_ffn1, ffn1_w_in, ffn1_w_out, norm_mix, w_in_mix,
           s5_lambda_re, s5_lambda_im, s5_log_dt, s5_b_re, s5_b_im, s5_c_re, s5_c_im, s5_d, s5_w_glu,
           w_out_mix, norm_ffn2, ffn2_w_in, ffn2_w_out, w_ada_final, b_ada_final, norm_final):
    depth = w_ada.shape[0]
    bp, seq, _ = x_prompt.shape
    bs, steps, _ = x_sample.shape
    assert seq % RET_CHUNK == 0 and steps % 2 == 0 and steps <= SUBLANES

    c_rows = bs + bp
    c_pad = -c_rows % 16
    c_all = jnp.concatenate([c_sample, c_prompt, jnp.zeros((c_pad, D_MODEL), F32)], axis=0)

    cs_p, sn_p = _rotary_tables(np.arange(seq))
    decay_p = _decay_tables(RET_CHUNK, RET_CHUNK)
    rows_s = SUBLANES
    pos_s = np.concatenate([PAST_LEN + np.arange(steps), np.zeros(rows_s - steps)])
    cs_s, sn_s = _rotary_tables(pos_s)
    decay_s = _decay_tables(steps, rows_s)

    xp = x_prompt
    xs = jnp.transpose(x_sample, (1, 0, 2))
    outs = {k: [] for k in ("p_re", "p_im", "p_ret", "s_re", "s_im", "s_ret")}
    for l in range(depth):
        mod = _ada(c_all, w_ada[l], b_ada[l])
        mod_p = mod[bs:c_rows][:, None, :]
        mod_s = mod[None]
        w_u = w_in_mix[l][:, :S5_WIDTH].astype(BF16)
        w_qkvg = w_in_mix[l][:, S5_WIDTH:].astype(BF16)
        w_glu = s5_w_glu[l].astype(BF16)
        w_out = w_out_mix[l].astype(BF16)
        two_step = _s5_tables(s5_lambda_re[l], s5_lambda_im[l], s5_log_dt[l],
                              s5_b_re[l], s5_b_im[l], s5_c_re[l], s5_c_im[l])
        last = l == depth - 1
        if last:
            fin = _ada(c_all, w_ada_final, b_ada_final)
            fin_p, fin_s = fin[bs:c_rows][:, None, :], fin[None]
        else:
            fin_p = fin_s = None

        xp, xs = _ffn(xp, mod_p, xs, mod_s, 0, norm_ffn1[l], ffn1_w_in[l], ffn1_w_out[l])

        s5o, hre, him = _s5_prompt(xp, mod_p, norm_mix[l], w_u, two_step, s5_d[l], w_glu)
        xp, sret = _ret_prompt(xp, mod_p, norm_mix[l], w_qkvg, cs_p, sn_p, decay_p, s5o, w_out)
        outs["p_re"].append(hre.reshape(bp, S5_GROUPS, S5_STATE))
        outs["p_im"].append(him.reshape(bp, S5_GROUPS, S5_STATE))
        outs["p_ret"].append(sret)

        qkvg, s5o_s, hre_s, him_s = _mix_in_sample(
            xs, mod_s, norm_mix[l], w_u, w_qkvg, two_step, s5_d[l], w_glu,
            state_ssm_re[l].reshape(bs, -1), state_ssm_im[l].reshape(bs, -1))
        ret_s, sret_s = _ret_sample(qkvg, state_ret[l], cs_s, sn_s, decay_s)
        xs = _mix_out_sample(xs, mod_s, s5o_s, ret_s, w_out)

        xp, xs = _ffn(xp, mod_p, xs, mod_s, 2, norm_ffn2[l], ffn2_w_in[l], ffn2_w_out[l],
                      fin_p, fin_s, norm_final if last else None)
        outs["s_re"].append(hre_s.reshape(bs, S5_GROUPS, S5_STATE))
        outs["s_im"].append(him_s.reshape(bs, S5_GROUPS, S5_STATE))
        outs["s_ret"].append(sret_s)

    y_prompt = xp
    y_sample = jnp.transpose(xs, (1, 0, 2))
    return (y_prompt, y_sample, jnp.stack(outs["p_re"]), jnp.stack(outs["p_im"]), jnp.stack(outs["p_ret"]),
            jnp.stack(outs["s_re"]), jnp.stack(outs["s_im"]), jnp.stack(outs["s_ret"]))
```

```python
import functools

import numpy as np
import jax
import jax.numpy as jnp
from jax import lax
from jax.experimental import pallas as pl
from jax.experimental.pallas import tpu as pltpu

F32 = jnp.float32
BF16 = jnp.bfloat16

D_MODEL = 1024
D_FF = 2816
N_MOD = 9
S5_WIDTH = 512
S5_GROUP = 16
S5_GROUPS = 32
S5_STATE = 64
RET_HEADS = 4
RET_DK = 128
RET_DV = 128
RET_WIDTH = 512
RET_CHUNK = 128
PAST_LEN = 16384
ROPE_BASE = 10000.0
EPS = 1e-6

LANES = 128
SUBLANES = 8
S5_PAIRS = S5_GROUPS // 2
S5_LANE_TILES = S5_WIDTH // LANES
VMEM_LIMIT_BYTES = 56 * 1024 * 1024


def _dot(a, b):
    return jnp.dot(a, b, preferred_element_type=F32)


def _dot_nt(a, b):
    return lax.dot_general(a, b, (((1,), (1,)), ((), ())), preferred_element_type=F32)


def _params(semantics):
    return pltpu.CompilerParams(dimension_semantics=semantics, vmem_limit_bytes=VMEM_LIMIT_BYTES)


def _resident(shape):
    nd = len(shape)
    return pl.BlockSpec(shape, lambda *_: (0,) * nd, pipeline_mode=pl.Buffered(1))


def _rms_mod(x, norm_w, shift, scale):
    xn = x * lax.rsqrt(jnp.mean(x * x, axis=-1, keepdims=True) + EPS)
    return xn * (norm_w * (1.0 + scale)) + shift


ADA_COLS = 1024
ADA_STREAMS = 4


def _ada_kernel(cs_ref, cp_ref, *refs):
    w_refs, b_ref, o_ref = refs[:ADA_STREAMS], refs[ADA_STREAMS], refs[ADA_STREAMS + 1]
    pad = o_ref.shape[0] - cs_ref.shape[0] - cp_ref.shape[0]
    c = jnp.concatenate([cs_ref[...], cp_ref[...], jnp.zeros((pad, cs_ref.shape[1]), F32)], axis=0)
    s = (c * jax.nn.sigmoid(c)).astype(BF16)
    band = s.shape[1] // ADA_STREAMS
    acc = b_ref[...]
    for n, w_ref in enumerate(w_refs):
        acc = acc + _dot(s[:, n * band:(n + 1) * band], w_ref[...].astype(BF16))
    o_ref[...] = acc


def _ada(c_s, c_p, w, b):
    k, n = w.shape
    m = -(-(c_s.shape[0] + c_p.shape[0]) // 16) * 16
    band = k // ADA_STREAMS
    w_specs = [pl.BlockSpec((band, ADA_COLS), lambda j, r=r: (r, j)) for r in range(ADA_STREAMS)]
    return pl.pallas_call(
        _ada_kernel,
        out_shape=jax.ShapeDtypeStruct((m, n), F32),
        grid=(n // ADA_COLS,),
        in_specs=[pl.BlockSpec(c_s.shape, lambda j: (0, 0)), pl.BlockSpec(c_p.shape, lambda j: (0, 0))]
        + w_specs + [pl.BlockSpec((1, ADA_COLS), lambda j: (0, j))],
        out_specs=pl.BlockSpec((m, ADA_COLS), lambda j: (0, j)),
        compiler_params=_params(("arbitrary",)),
        name="ada_mod",
    )(c_s, c_p, *([w] * ADA_STREAMS), b.reshape(1, n))


FF_CHUNK = 256
FF_CHUNKS = D_FF // FF_CHUNK
FF_GROUPS = ((0, 4), (4, 8), (8, FF_CHUNKS))
FFN_ROWS = 512


def _ffn_math(x, m, norm_w, win_scr, wout_scr, fin, norm_f):
    a_dim, r_dim, _ = x.shape
    shift, scale, gate = m[..., :D_MODEL], m[..., D_MODEL:2 * D_MODEL], m[..., 2 * D_MODEL:]
    hb = _rms_mod(x, norm_w, shift, scale).reshape(a_dim * r_dim, D_MODEL).astype(BF16)
    o = None
    for c0, c1 in FF_GROUPS:
        a = jnp.concatenate([_dot(hb, win_scr[c]) for c in range(c0, c1)], axis=-1)
        b = jnp.concatenate([_dot(hb, win_scr[FF_CHUNKS + c]) for c in range(c0, c1)], axis=-1)
        act = (a * jax.nn.sigmoid(a) * b).astype(BF16)
        part = _dot(act, wout_scr[c0:c1].reshape((c1 - c0) * FF_CHUNK, D_MODEL))
        o = part if o is None else o + part
    y = x + (0.5 * gate) * o.reshape(a_dim, r_dim, D_MODEL)
    if fin is not None:
        y = _rms_mod(y, norm_f, fin[..., :D_MODEL], fin[..., D_MODEL:])
    return y


def _ffn_kernel(*refs, n_prompt, final):
    if final:
        (xp_ref, mp_ref, xs_ref, ms_ref, nw_ref, wa_ref, wb_ref, wo_ref, fp_ref, fs_ref, nf_ref,
         op_ref, os_ref, win_scr, wout_scr) = refs
    else:
        xp_ref, mp_ref, xs_ref, ms_ref, nw_ref, wa_ref, wb_ref, wo_ref, op_ref, os_ref, win_scr, wout_scr = refs
        fp_ref = fs_ref = nf_ref = None
    i = pl.program_id(0)

    @pl.when(i < FF_CHUNKS)
    def _():
        win_scr[i] = wa_ref[...].astype(BF16)
        win_scr[FF_CHUNKS + i] = wb_ref[...].astype(BF16)
        wout_scr[i] = wo_ref[...].astype(BF16)

    def run(x_ref, m_ref, f_ref, o_ref):
        o_ref[...] = _ffn_math(x_ref[...], m_ref[...], nw_ref[...], win_scr, wout_scr,
                               f_ref[...] if final else None, nf_ref[...] if final else None)

    @pl.when((i >= FF_CHUNKS) & (i < FF_CHUNKS + n_prompt))
    def _():
        run(xp_ref, mp_ref, fp_ref, op_ref)

    @pl.when(i == FF_CHUNKS + n_prompt)
    def _():
        run(xs_ref, ms_ref, fs_ref, os_ref)


def _ffn(xp, mod_p, xs, mod_s, sub, norm_w, w_in, w_out, fin_p=None, fin_s=None, norm_f=None):
    bp, seq, _ = xp.shape
    steps, bs, _ = xs.shape
    tiles = seq // FFN_ROWS
    n_prompt = bp * tiles
    final = fin_p is not None

    def chunk(i):
        return jnp.minimum(i, FF_CHUNKS - 1)

    def tile(i):
        return jnp.clip(i - FF_CHUNKS, 0, n_prompt - 1)

    tok_p = pl.BlockSpec((1, FFN_ROWS, D_MODEL), lambda i: (tile(i) // tiles, tile(i) % tiles, 0))
    tok_s = pl.BlockSpec((steps, bs, D_MODEL), lambda i: (0, 0, 0))
    in_specs = [tok_p,
                pl.BlockSpec((1, 1, 3 * D_MODEL), lambda i: (tile(i) // tiles, 0, sub)),
                tok_s,
                pl.BlockSpec((1, bs, 3 * D_MODEL), lambda i: (0, 0, sub)),
                _resident((1, D_MODEL)),
                pl.BlockSpec((D_MODEL, FF_CHUNK), lambda i: (0, chunk(i))),
                pl.BlockSpec((D_MODEL, FF_CHUNK), lambda i: (0, FF_CHUNKS + chunk(i))),
                pl.BlockSpec((FF_CHUNK, D_MODEL), lambda i: (chunk(i), 0))]
    args = [xp, mod_p, xs, mod_s, norm_w.reshape(1, D_MODEL), w_in, w_in, w_out]
    if final:
        in_specs += [pl.BlockSpec((1, 1, 2 * D_MODEL), lambda i: (tile(i) // tiles, 0, 0)),
                     pl.BlockSpec((1, bs, 2 * D_MODEL), lambda i: (0, 0, 0)),
                     _resident((1, D_MODEL))]
        args += [fin_p, fin_s, norm_f.reshape(1, D_MODEL)]
    return pl.pallas_call(
        functools.partial(_ffn_kernel, n_prompt=n_prompt, final=final),
        out_shape=(jax.ShapeDtypeStruct(xp.shape, F32), jax.ShapeDtypeStruct(xs.shape, F32)),
        grid=(FF_CHUNKS + n_prompt + 1,),
        in_specs=in_specs,
        out_specs=(tok_p, tok_s),
        scratch_shapes=[pltpu.VMEM((2 * FF_CHUNKS, D_MODEL, FF_CHUNK), BF16),
                        pltpu.VMEM((FF_CHUNKS, FF_CHUNK, D_MODEL), BF16)],
        compiler_params=_params(("arbitrary",)),
        name="ffn_final" if final else "ffn",
    )(*args)


def _s5_tables(lam_re, lam_im, log_dt, b_re, b_im, c_re, c_im):
    lr, li = lam_re.astype(F32), lam_im.astype(F32)
    dt = jnp.exp(log_dt.astype(F32))[:, None]
    mag = jnp.exp(lr * dt)
    ar, ai = mag * jnp.cos(li * dt), mag * jnp.sin(li * dt)
    a2r, a2i = ar * ar - ai * ai, 2.0 * ar * ai
    den = lr * lr + li * li
    cr = ((ar - 1.0) * lr + ai * li) / den
    ci = (ai * lr - (ar - 1.0) * li) / den
    br, bi = b_re.astype(F32), b_im.astype(F32)
    bb_re = cr[:, :, None] * br - ci[:, :, None] * bi
    bb_im = cr[:, :, None] * bi + ci[:, :, None] * br
    lb_re = ar[:, :, None] * bb_re - ai[:, :, None] * bb_im
    lb_im = ar[:, :, None] * bb_im + ai[:, :, None] * bb_re
    cre, cim = c_re.astype(F32), c_im.astype(F32)
    c1_re = cre * ar[:, None, :] - cim * ai[:, None, :]
    c1_im = cre * ai[:, None, :] + cim * ar[:, None, :]
    c2_re = cre * a2r[:, None, :] - cim * a2i[:, None, :]
    c2_im = cre * a2i[:, None, :] + cim * a2r[:, None, :]

    lam2_t = jnp.broadcast_to(jnp.stack([a2r, a2i]).reshape(2, S5_PAIRS, 1, LANES),
                              (2, S5_PAIRS, SUBLANES, LANES))

    groups_per_tile = LANES // S5_GROUP
    q_of = 2 * (np.arange(S5_PAIRS) % 4)[:, None] + np.arange(2)[None, :]
    slot = (np.arange(groups_per_tile)[None, :, None] == q_of[:, None, :])
    slot = np.broadcast_to(slot[:, :, None, :, None], (S5_PAIRS, groups_per_tile, 2, 2, S5_STATE))
    slot = jnp.asarray(slot.reshape(S5_PAIRS, 1, groups_per_tile, 1, 2 * LANES), F32)

    def place(parts):
        v = jnp.stack([jnp.stack(p) for p in parts])
        v = v.reshape(2, 2, S5_PAIRS, 2, S5_GROUP, S5_STATE)
        v = jnp.transpose(v, (2, 0, 4, 1, 3, 5)).reshape(S5_PAIRS, 2, 1, S5_GROUP, 2 * LANES)
        return (v * slot).reshape(S5_PAIRS, 2 * LANES, 2 * LANES).astype(BF16)

    def cn(t):
        return jnp.transpose(t, (0, 2, 1))

    wz = place([[cn(lb_re), cn(lb_im)], [cn(bb_re), cn(bb_im)]])
    ws_t = place([[c1_re, -c1_im], [c2_re, -c2_im]])

    c_cat = jnp.concatenate([cre, -cim], axis=-1)
    x_cat = jnp.stack([jnp.concatenate([bb_re, bb_im], axis=1),
                       jnp.concatenate([lb_re, lb_im], axis=1)])
    k = jnp.einsum('gom,xgmi->xgio', c_cat, x_cat, precision=lax.Precision.HIGHEST)
    k = k.reshape(2, S5_LANE_TILES, groups_per_tile, 1, S5_GROUP, S5_GROUP)
    eye = jnp.asarray(np.eye(groups_per_tile)[None, None, :, :, None, None], F32)
    k = jnp.transpose(eye * k, (0, 1, 2, 4, 3, 5)).reshape(2, S5_LANE_TILES, LANES, LANES)
    k0, k1 = k[0], k[1]
    wk = jnp.concatenate([jnp.concatenate([k0, k1], axis=-1),
                          jnp.concatenate([jnp.zeros_like(k0), k0], axis=-1)], axis=1).astype(BF16)
    return lam2_t, wz, ws_t, wk


def _s5_prompt_kernel(x_ref, mod_ref, nw_ref, wu_ref, wz_ref, lam2_ref, ws_ref, wk_ref, d_ref, wglu_ref,
                      o_ref, hre_ref, him_ref,
                      u_slab, sre, sim, st_re, st_im, y_slab, *, tl, nb):
    i = pl.program_id(0)
    nk = tl // 2
    half = nk * nb

    @pl.when(i == 0)
    def _():
        st_re[...] = jnp.zeros_like(st_re)
        st_im[...] = jnp.zeros_like(st_im)

    x = x_ref[...]
    m = mod_ref[...]
    h = _rms_mod(x, nw_ref[...], m[..., :D_MODEL], m[..., D_MODEL:2 * D_MODEL])
    u = _dot(h.reshape(nb * tl, D_MODEL).astype(BF16), wu_ref[...])

    for s in range(S5_LANE_TILES):
        for b in range(nb):
            u_slab[s, pl.ds(b, tl, stride=nb), :] = u[b * tl:(b + 1) * tl, s * LANES:(s + 1) * LANES]

    u_ev, u_od, u_cat = [], [], []
    for s in range(S5_LANE_TILES):
        tiles = u_slab[s].reshape(nk, 2 * nb, LANES)
        ev = tiles[:, :nb, :].reshape(half, LANES)
        od = tiles[:, nb:, :].reshape(half, LANES)
        u_ev.append(ev)
        u_od.append(od)
        u_cat.append(jnp.concatenate([ev, od], axis=-1).astype(BF16))

    for j in range(S5_PAIRS):
        z = _dot(u_cat[j // 4], wz_ref[j])
        sre[j, 0:nb, :] = st_re[j]
        sim[j, 0:nb, :] = st_im[j]
        sre[j, nb:nb + half, :] = z[:, :LANES]
        sim[j, nb:nb + half, :] = z[:, LANES:]

    pairs_per_pass = 8
    for j0 in range(0, S5_PAIRS, pairs_per_pass):
        js = range(j0, j0 + pairs_per_pass)
        lr = [lam2_ref[0, j] for j in js]
        li = [lam2_ref[1, j] for j in js]
        carry = []
        for j in js:
            carry += [st_re[j], st_im[j]]
        for k in range(nk):
            r0 = (k + 1) * nb
            for n, j in enumerate(js):
                re, im = carry[2 * n], carry[2 * n + 1]
                nre = lr[n] * re - li[n] * im + sre[j, r0:r0 + nb, :]
                nim = lr[n] * im + li[n] * re + sim[j, r0:r0 + nb, :]
                if k + 1 < nk:
                    sre[j, r0:r0 + nb, :] = nre
                    sim[j, r0:r0 + nb, :] = nim
                carry[2 * n], carry[2 * n + 1] = nre, nim
        for n, j in enumerate(js):
            st_re[j] = carry[2 * n]
            st_im[j] = carry[2 * n + 1]

    y_ev, y_od = [], []
    for jt in range(S5_LANE_TILES):
        acc = _dot(u_cat[jt], wk_ref[jt])
        for jj in range(4):
            j = 4 * jt + jj
            sp = jnp.concatenate([sre[j, 0:half, :], sim[j, 0:half, :]], axis=-1).astype(BF16)
            acc = acc + _dot_nt(sp, ws_ref[j])
        d = d_ref[:, jt * LANES:(jt + 1) * LANES]
        y_ev.append(jax.nn.gelu(acc[:, :LANES] + d * u_ev[jt]))
        y_od.append(jax.nn.gelu(acc[:, LANES:] + d * u_od[jt]))
    y_ev = jnp.concatenate(y_ev, axis=-1).reshape(nk, nb, S5_WIDTH)
    y_od = jnp.concatenate(y_od, axis=-1).reshape(nk, nb, S5_WIDTH)
    y = jnp.concatenate([y_ev, y_od], axis=1).reshape(tl * nb, S5_WIDTH)
    glu = _dot(y.astype(BF16), wglu_ref[...])
    s5o = glu[:, :S5_WIDTH] * jax.nn.sigmoid(glu[:, S5_WIDTH:])
    for s in range(S5_LANE_TILES):
        y_slab[s] = s5o[:, s * LANES:(s + 1) * LANES]
    for b in range(nb):
        for s in range(S5_LANE_TILES):
            c0 = b * S5_WIDTH + s * LANES
            o_ref[:, c0:c0 + LANES] = y_slab[s, pl.ds(b, tl, stride=nb), :].astype(BF16)

    @pl.when(i == pl.num_programs(0) - 1)
    def _():
        hre_ref[...] = jnp.concatenate([st_re[j] for j in range(S5_PAIRS)], axis=-1)
        him_ref[...] = jnp.concatenate([st_im[j] for j in range(S5_PAIRS)], axis=-1)


def _s5_prompt(x1, mod3, norm_w, w_u, two_step, d_skip, w_glu, tl=128):
    nb, seq, _ = x1.shape
    lam2_t, wz, ws_t, wk = two_step
    rows = nb * tl
    half = rows // 2
    n_state = S5_GROUPS * S5_STATE
    return pl.pallas_call(
        functools.partial(_s5_prompt_kernel, tl=tl, nb=nb),
        out_shape=(jax.ShapeDtypeStruct((seq, nb * S5_WIDTH), BF16),
                   jax.ShapeDtypeStruct((nb, n_state), F32),
                   jax.ShapeDtypeStruct((nb, n_state), F32)),
        grid=(seq // tl,),
        in_specs=[pl.BlockSpec((nb, tl, D_MODEL), lambda i: (0, i, 0)),
                  pl.BlockSpec((nb, 1, 3 * D_MODEL), lambda i: (0, 0, 1)),
                  _resident((1, D_MODEL)),
                  _resident((D_MODEL, S5_WIDTH)),
                  _resident(wz.shape), _resident(lam2_t.shape), _resident(ws_t.shape), _resident(wk.shape),
                  _resident((1, S5_WIDTH)),
                  _resident((S5_WIDTH, 2 * S5_WIDTH))],
        out_specs=(pl.BlockSpec((tl, nb * S5_WIDTH), lambda i: (i, 0)),
                   pl.BlockSpec((nb, n_state), lambda i: (0, 0)),
                   pl.BlockSpec((nb, n_state), lambda i: (0, 0))),
        scratch_shapes=[pltpu.VMEM((S5_LANE_TILES, rows, LANES), F32),
                        pltpu.VMEM((S5_PAIRS, nb + half, LANES), F32),
                        pltpu.VMEM((S5_PAIRS, nb + half, LANES), F32),
                        pltpu.VMEM((S5_PAIRS, nb, LANES), F32),
                        pltpu.VMEM((S5_PAIRS, nb, LANES), F32),
                        pltpu.VMEM((S5_LANE_TILES, rows, LANES), F32)],
        compiler_params=_params(("arbitrary",)),
        name="s5_prompt",
    )(x1, mod3, norm_w.reshape(1, D_MODEL), w_u, wz, lam2_t, ws_t, wk, d_skip.reshape(1, S5_WIDTH), w_glu)


def _rotary_tables(pos):
    half = RET_DK // 2
    inv = ROPE_BASE ** (-np.arange(half, dtype=np.float64) / half)
    ang = np.asarray(pos, np.float64)[:, None] * inv[None, :]
    cos, sin = np.cos(ang), np.sin(ang)
    return (jnp.asarray(np.concatenate([cos, cos], axis=-1), F32),
            jnp.asarray(np.concatenate([-sin, sin], axis=-1), F32))


def _decay_tables(chunk, rows):
    lg = np.log1p(-np.exp2(-5.0 - np.arange(RET_HEADS, dtype=np.float64)))
    idx = np.arange(rows, dtype=np.float64)
    valid = idx < chunk
    diff = idx[:, None] - idx[None, :]
    intra = np.where((diff >= 0) & valid[:, None] & valid[None, :],
                     np.exp(lg[:, None, None] * np.maximum(diff, 0.0)), 0.0)
    q_dec = np.where(valid[None, :], np.exp(lg[:, None] * (idx[None, :] + 1.0)), 0.0)
    k_dec = np.where(valid[None, :], np.exp(lg[:, None] * (chunk - 1.0 - idx)[None, :]), 0.0)
    c_dec = np.exp(lg * chunk)
    q_dec = np.broadcast_to(q_dec[:, :, None], (RET_HEADS, rows, RET_DV))
    k_dec = np.broadcast_to(k_dec[:, :, None], (RET_HEADS, rows, RET_DK))
    c_dec = np.broadcast_to(c_dec[:, None, None], (RET_HEADS, 1, RET_DV))
    return tuple(jnp.asarray(t, F32) for t in (intra, q_dec, k_dec, c_dec))


def _rotary(x, cs, sn):
    return x * cs + pltpu.roll(x, RET_DK // 2, axis=1) * sn


def _head_norm_gate(ret, g):
    mu = jnp.mean(ret, axis=-1, keepdims=True)
    cen = ret - mu
    var = jnp.mean(cen * cen, axis=-1, keepdims=True)
    return (g * jax.nn.sigmoid(g)) * (cen * lax.rsqrt(var + EPS))


def _ret_prompt_kernel(x_ref, mod_ref, nw_ref, wq_ref, cs_ref, sn_ref, intra_ref, qd_ref, kd_ref, cd_ref,
                       s5o_ref, wout_ref, o_ref, sout_ref, s_scr, *, tm):
    t = pl.program_id(1)

    @pl.when(t == 0)
    def _():
        s_scr[...] = jnp.zeros_like(s_scr)

    x = x_ref[0]
    m = mod_ref[0]
    h = _rms_mod(x, nw_ref[...], m[:, :D_MODEL], m[:, D_MODEL:2 * D_MODEL])
    proj = _dot(h.astype(BF16), wq_ref[...])
    cs, sn = cs_ref[...], sn_ref[...]
    n_chunks = tm // RET_CHUNK
    heads = range(RET_HEADS)
    chunks = range(n_chunks)

    lhs, vbs, kvs = {}, {}, {}
    for hd in heads:
        lo = hd * RET_DK
        q = _rotary(proj[:, lo:lo + RET_DK], cs, sn)
        k = _rotary(proj[:, RET_WIDTH + lo:RET_WIDTH + lo + RET_DK], cs, sn) * (RET_DK ** -0.5)
        v = proj[:, 2 * RET_WIDTH + lo:2 * RET_WIDTH + lo + RET_DV]
        for c in chunks:
            rows = slice(c * RET_CHUNK, (c + 1) * RET_CHUNK)
            qc, kc, vb = q[rows], k[rows], v[rows].astype(BF16)
            scores = _dot_nt(qc.astype(BF16), kc.astype(BF16)) * intra_ref[hd]
            lhs[hd, c] = jnp.concatenate([scores.astype(BF16), (qc * qd_ref[hd]).astype(BF16)], axis=-1)
            vbs[hd, c] = vb
            kvs[hd, c] = _dot(jnp.transpose(kc * kd_ref[hd]).astype(BF16), vb)

    states = {}
    for hd in heads:
        s = s_scr[hd]
        for c in chunks:
            states[hd, c] = s
            s = s * cd_ref[hd] + kvs[hd, c]
        s_scr[hd] = s

    parts = [s5o_ref[...]]
    for hd in heads:
        lo = 3 * RET_WIDTH + hd * RET_DV
        outs = [_dot(lhs[hd, c], jnp.concatenate([vbs[hd, c], states[hd, c].astype(BF16)], axis=0))
                for c in chunks]
        parts.append(_head_norm_gate(jnp.concatenate(outs, axis=0), proj[:, lo:lo + RET_DV]).astype(BF16))
    mix = jnp.concatenate(parts, axis=-1)
    o_ref[0] = x + m[:, 2 * D_MODEL:] * _dot(mix, wout_ref[...])

    @pl.when(t == pl.num_programs(1) - 1)
    def _():
        sout_ref[0] = s_scr[...]


def _ret_prompt(x1, mod3, norm_w, w_qkvg, cs, sn, decay, s5o, w_out, tm=1024):
    nb, seq, _ = x1.shape
    intra, q_dec, k_dec, c_dec = decay
    return pl.pallas_call(
        functools.partial(_ret_prompt_kernel, tm=tm),
        out_shape=(jax.ShapeDtypeStruct(x1.shape, F32),
                   jax.ShapeDtypeStruct((nb, RET_HEADS, RET_DK, RET_DV), F32)),
        grid=(nb, seq // tm),
        in_specs=[pl.BlockSpec((1, tm, D_MODEL), lambda b, t: (b, t, 0)),
                  pl.BlockSpec((1, 1, 3 * D_MODEL), lambda b, t: (b, 0, 1)),
                  _resident((1, D_MODEL)),
                  _resident((D_MODEL, 4 * RET_WIDTH)),
                  pl.BlockSpec((tm, RET_DK), lambda b, t: (t, 0)),
                  pl.BlockSpec((tm, RET_DK), lambda b, t: (t, 0)),
                  _resident(intra.shape), _resident(q_dec.shape), _resident(k_dec.shape), _resident(c_dec.shape),
                  pl.BlockSpec((tm, S5_WIDTH), lambda b, t: (t, b)),
                  _resident((D_MODEL, D_MODEL))],
        out_specs=(pl.BlockSpec((1, tm, D_MODEL), lambda b, t: (b, t, 0)),
                   pl.BlockSpec((1, RET_HEADS, RET_DK, RET_DV), lambda b, t: (b, 0, 0, 0))),
        scratch_shapes=[pltpu.VMEM((RET_HEADS, RET_DK, RET_DV), F32)],
        compiler_params=_params(("arbitrary", "arbitrary")),
        name="ret_prompt",
    )(x1, mod3, norm_w.reshape(1, D_MODEL), w_qkvg, cs, sn, intra, q_dec, k_dec, c_dec, s5o, w_out)


def _mix_in_sample_kernel(x_ref, mod_ref, nw_ref, wu_ref, wq_ref, wz_ref, lam2_ref, ws_ref, wk_ref, d_ref,
                          wglu_ref, h0re_ref, h0im_ref, qkvg_ref, s5o_ref, hre_ref, him_ref, *, steps, nb):
    x = x_ref[...]
    m = mod_ref[...]
    h = _rms_mod(x, nw_ref[...], m[..., :D_MODEL], m[..., D_MODEL:2 * D_MODEL])
    hb = h.reshape(steps * nb, D_MODEL).astype(BF16)
    proj = _dot(hb, wq_ref[...])
    qkvg_ref[...] = jnp.zeros_like(qkvg_ref)
    for s in range(4 * RET_HEADS):
        for t in range(steps):
            qkvg_ref[s, pl.ds(t, nb, stride=SUBLANES), :] = proj[t * nb:(t + 1) * nb, s * LANES:(s + 1) * LANES]
    u = _dot(hb, wu_ref[...])
    nk = steps // 2

    def rows_of(parity, lanes):
        return jnp.concatenate([u[(2 * k + parity) * nb:(2 * k + parity + 1) * nb, lanes] for k in range(nk)],
                               axis=0)

    u_ev, u_od, u_cat = [], [], []
    for s in range(S5_LANE_TILES):
        lanes = slice(s * LANES, (s + 1) * LANES)
        u_ev.append(rows_of(0, lanes))
        u_od.append(rows_of(1, lanes))
        u_cat.append(jnp.concatenate([u_ev[s], u_od[s]], axis=-1).astype(BF16))

    prev = []
    for j in range(S5_PAIRS):
        z = _dot(u_cat[j // 4], wz_ref[j])
        lr = lam2_ref[0, j][:1]
        li = lam2_ref[1, j][:1]
        sre = h0re_ref[:, j * LANES:(j + 1) * LANES]
        sim = h0im_ref[:, j * LANES:(j + 1) * LANES]
        pre, pim = [], []
        for k in range(nk):
            pre.append(sre)
            pim.append(sim)
            zre, zim = z[k * nb:(k + 1) * nb, :LANES], z[k * nb:(k + 1) * nb, LANES:]
            sre, sim = lr * sre - li * sim + zre, lr * sim + li * sre + zim
        hre_ref[:, j * LANES:(j + 1) * LANES] = sre
        him_ref[:, j * LANES:(j + 1) * LANES] = sim
        prev.append(jnp.concatenate([jnp.concatenate(pre, axis=0), jnp.concatenate(pim, axis=0)],
                                    axis=-1).astype(BF16))

    y_ev, y_od = [], []
    for jt in range(S5_LANE_TILES):
        acc = _dot(u_cat[jt], wk_ref[jt])
        for jj in range(4):
            acc = acc + _dot_nt(prev[4 * jt + jj], ws_ref[4 * jt + jj])
        d = d_ref[:, jt * LANES:(jt + 1) * LANES]
        y_ev.append(jax.nn.gelu(acc[:, :LANES] + d * u_ev[jt]))
        y_od.append(jax.nn.gelu(acc[:, LANES:] + d * u_od[jt]))
    y_ev = jnp.concatenate(y_ev, axis=-1)
    y_od = jnp.concatenate(y_od, axis=-1)
    y = jnp.concatenate([part[k * nb:(k + 1) * nb] for k in range(nk) for part in (y_ev, y_od)], axis=0)
    glu = _dot(y.astype(BF16), wglu_ref[...])
    s5o_ref[...] = (glu[:, :S5_WIDTH] * jax.nn.sigmoid(glu[:, S5_WIDTH:])).astype(BF16)


def _mix_in_sample(x1, mod3, norm_w, w_u, w_qkvg, two_step, d_skip, w_glu, h0_re, h0_im):
    steps, nb, _ = x1.shape
    rows = steps * nb
    n_state = S5_GROUPS * S5_STATE
    slabs = (4 * RET_HEADS, nb * SUBLANES, LANES)
    lam2_t, wz, ws_t, wk = two_step
    args = (x1, mod3, norm_w.reshape(1, D_MODEL), w_u, w_qkvg, wz, lam2_t, ws_t, wk,
            d_skip.reshape(1, S5_WIDTH), w_glu, h0_re, h0_im)
    in_specs = [pl.BlockSpec((steps, nb, D_MODEL), lambda i: (0, 0, 0)),
                pl.BlockSpec((1, nb, 3 * D_MODEL), lambda i: (0, 0, 1))]
    in_specs += [_resident(a.shape) for a in args[2:]]
    return pl.pallas_call(
        functools.partial(_mix_in_sample_kernel, steps=steps, nb=nb),
        out_shape=(jax.ShapeDtypeStruct(slabs, F32),
                   jax.ShapeDtypeStruct((rows, S5_WIDTH), BF16),
                   jax.ShapeDtypeStruct((nb, n_state), F32),
                   jax.ShapeDtypeStruct((nb, n_state), F32)),
        grid=(1,),
        in_specs=in_specs,
        out_specs=(pl.BlockSpec(slabs, lambda i: (0, 0, 0)),
                   pl.BlockSpec((rows, S5_WIDTH), lambda i: (0, 0)),
                   pl.BlockSpec((nb, n_state), lambda i: (0, 0)),
                   pl.BlockSpec((nb, n_state), lambda i: (0, 0))),
        compiler_params=_params(("arbitrary",)),
        name="mix_in_sample",
    )(*args)


def _ret_sample_kernel(qkvg_ref, s_ref, cs_ref, sn_ref, intra_ref, qd_ref, kd_ref, cd_ref,
                       o_ref, sout_ref, *, bb):
    cs, sn = cs_ref[...], sn_ref[...]
    pairs = [(b, hd) for b in range(bb) for hd in range(RET_HEADS)]

    def tile(kind, b, hd):
        return qkvg_ref[kind * RET_HEADS + hd, b * SUBLANES:(b + 1) * SUBLANES, :]

    scores, cross_lhs, vbs = {}, {}, {}
    for b, hd in pairs:
        q = _rotary(tile(0, b, hd), cs, sn)
        k = _rotary(tile(1, b, hd), cs, sn) * (RET_DK ** -0.5)
        vb = tile(2, b, hd).astype(BF16)
        scores[b, hd] = _dot_nt(q.astype(BF16), k.astype(BF16)) * intra_ref[hd]
        cross_lhs[b, hd] = (q * qd_ref[hd]).astype(BF16)
        vbs[b, hd] = vb
        sout_ref[b, hd] = s_ref[b, hd] * cd_ref[hd] + _dot(jnp.transpose(k * kd_ref[hd]).astype(BF16), vb)
    for b, hd in pairs:
        o = _dot(scores[b, hd].astype(BF16), vbs[b, hd]) + _dot(cross_lhs[b, hd], s_ref[b, hd].astype(BF16))
        o_ref[hd, b * SUBLANES:(b + 1) * SUBLANES, :] = _head_norm_gate(o, tile(3, b, hd))


def _ret_sample(qkvg, s0, cs, sn, decay, bb=16):
    nb = s0.shape[0]
    intra, q_dec, k_dec, c_dec = decay
    st = pl.BlockSpec((bb, RET_HEADS, RET_DK, RET_DV), lambda i: (i, 0, 0, 0))
    return pl.pallas_call(
        functools.partial(_ret_sample_kernel, bb=bb),
        out_shape=(jax.ShapeDtypeStruct((RET_HEADS, nb * SUBLANES, LANES), F32),
                   jax.ShapeDtypeStruct(s0.shape, F32)),
        grid=(nb // bb,),
        in_specs=[pl.BlockSpec((4 * RET_HEADS, bb * SUBLANES, LANES), lambda i: (0, i, 0)),
                  st, _resident(cs.shape), _resident(sn.shape),
                  _resident(intra.shape), _resident(q_dec.shape), _resident(k_dec.shape), _resident(c_dec.shape)],
        out_specs=(pl.BlockSpec((RET_HEADS, bb * SUBLANES, LANES), lambda i: (0, i, 0)), st),
        compiler_params=_params(("arbitrary",)),
        name="ret_sample",
    )(qkvg, s0, cs, sn, intra, q_dec, k_dec, c_dec)


def _mix_out_sample_kernel(x_ref, mod_ref, s5o_ref, ret_ref, wout_ref, o_ref, *, steps, nb):
    x = x_ref[...]
    gate = mod_ref[...][..., 2 * D_MODEL:]
    ret = jnp.concatenate(
        [jnp.concatenate([ret_ref[hd, pl.ds(t, nb, stride=SUBLANES), :] for hd in range(RET_HEADS)], axis=-1)
         for t in range(steps)], axis=0)
    mix = jnp.concatenate([s5o_ref[...], ret.astype(BF16)], axis=-1)
    o_ref[...] = x + gate * _dot(mix, wout_ref[...]).reshape(steps, nb, D_MODEL)


def _mix_out_sample(x1, mod3, s5o, ret, w_out):
    steps, nb, _ = x1.shape
    rows = steps * nb
    return pl.pallas_call(
        functools.partial(_mix_out_sample_kernel, steps=steps, nb=nb),
        out_shape=jax.ShapeDtypeStruct(x1.shape, F32),
        grid=(1,),
        in_specs=[pl.BlockSpec((steps, nb, D_MODEL), lambda i: (0, 0, 0)),
                  pl.BlockSpec((1, nb, 3 * D_MODEL), lambda i: (0, 0, 1)),
                  pl.BlockSpec((rows, S5_WIDTH), lambda i: (0, 0)),
                  pl.BlockSpec(ret.shape, lambda i: (0, 0, 0)),
                  _resident((D_MODEL, D_MODEL))],
        out_specs=pl.BlockSpec((steps, nb, D_MODEL), lambda i: (0, 0, 0)),
        compiler_params=_params(("arbitrary",)),
        name="mix_out_sample",
    )(x1, mod3, s5o, ret, w_out)


def kernel(x_prompt, x_sample, state_ssm_re, state_ssm_im, state_ret, c_prompt, c_sample,
           w_ada, b_ada, norm_ffn1, ffn1_w_in, ffn1_w_out, norm_mix, w_in_mix,
           s5_lambda_re, s5_lambda_im, s5_log_dt, s5_b_re, s5_b_im, s5_c_re, s5_c_im, s5_d, s5_w_glu,
           w_out_mix, norm_ffn2, ffn2_w_in, ffn2_w_out, w_ada_final, b_ada_final, norm_final):
    depth = w_ada.shape[0]
    bp, seq, _ = x_prompt.shape
    bs, steps, _ = x_sample.shape
    assert seq % RET_CHUNK == 0 and steps % 2 == 0 and steps <= SUBLANES

    c_rows = bs + bp

    cs_p, sn_p = _rotary_tables(np.arange(seq))
    decay_p = _decay_tables(RET_CHUNK, RET_CHUNK)
    rows_s = SUBLANES
    pos_s = np.concatenate([PAST_LEN + np.arange(steps), np.zeros(rows_s - steps)])
    cs_s, sn_s = _rotary_tables(pos_s)
    decay_s = _decay_tables(steps, rows_s)

    xp = x_prompt
    xs = jnp.transpose(x_sample, (1, 0, 2))
    outs = {k: [] for k in ("p_re", "p_im", "p_ret", "s_re", "s_im", "s_ret")}
    for l in range(depth):
        mod = _ada(c_sample, c_prompt, w_ada[l], b_ada[l])
        mod_p = mod[bs:c_rows][:, None, :]
        mod_s = mod[None]
        w_u = w_in_mix[l][:, :S5_WIDTH].astype(BF16)
        w_qkvg = w_in_mix[l][:, S5_WIDTH:].astype(BF16)
        w_glu = s5_w_glu[l].astype(BF16)
        w_out = w_out_mix[l].astype(BF16)
        two_step = _s5_tables(s5_lambda_re[l], s5_lambda_im[l], s5_log_dt[l],
                              s5_b_re[l], s5_b_im[l], s5_c_re[l], s5_c_im[l])
        last = l == depth - 1
        if last:
            fin = _ada(c_sample, c_prompt, w_ada_final, b_ada_final)
            fin_p, fin_s = fin[bs:c_rows][:, None, :], fin[None]
        else:
            fin_p = fin_s = None

        xp, xs = _ffn(xp, mod_p, xs, mod_s, 0, norm_ffn1[l], ffn1_w_in[l], ffn1_w_out[l])

        s5o, hre, him = _s5_prompt(xp, mod_p, norm_mix[l], w_u, two_step, s5_d[l], w_glu)
        xp, sret = _ret_prompt(xp, mod_p, norm_mix[l], w_qkvg, cs_p, sn_p, decay_p, s5o, w_out)
        outs["p_re"].append(hre.reshape(bp, S5_GROUPS, S5_STATE))
        outs["p_im"].append(him.reshape(bp, S5_GROUPS, S5_STATE))
        outs["p_ret"].append(sret)

        qkvg, s5o_s, hre_s, him_s = _mix_in_sample(
            xs, mod_s, norm_mix[l], w_u, w_qkvg, two_step, s5_d[l], w_glu,
            state_ssm_re[l].reshape(bs, -1), state_ssm_im[l].reshape(bs, -1))
        ret_s, sret_s = _ret_sample(qkvg, state_ret[l], cs_s, sn_s, decay_s)
        xs = _mix_out_sample(xs, mod_s, s5o_s, ret_s, w_out)

        xp, xs = _ffn(xp, mod_p, xs, mod_s, 2, norm_ffn2[l], ffn2_w_in[l], ffn2_w_out[l],
                      fin_p, fin_s, norm_final if last else None)
        outs["s_re"].append(hre_s.reshape(bs, S5_GROUPS, S5_STATE))
        outs["s_im"].append(him_s.reshape(bs, S5_GROUPS, S5_STATE))
        outs["s_ret"].append(sret_s)

    y_prompt = xp
    y_sample = jnp.transpose(xs, (1, 0, 2))
    return (y_prompt, y_sample, jnp.stack(outs["p_re"]), jnp.stack(outs["p_im"]), jnp.stack(outs["p_ret"]),
            jnp.stack(outs["s_re"]), jnp.stack(outs["s_im"]), jnp.stack(outs["s_ret"]))
```

```python
import functools

import numpy as np
import jax
import jax.numpy as jnp
from jax import lax
from jax.experimental import pallas as pl
from jax.experimental.pallas import tpu as pltpu

F32 = jnp.float32
BF16 = jnp.bfloat16

D_MODEL = 1024
D_FF = 2816
N_MOD = 9
S5_WIDTH = 512
S5_GROUP = 16
S5_GROUPS = 32
S5_STATE = 64
RET_HEADS = 4
RET_DK = 128
RET_DV = 128
RET_WIDTH = 512
RET_CHUNK = 128
PAST_LEN = 16384
ROPE_BASE = 10000.0
EPS = 1e-6

LANES = 128
SUBLANES = 8
S5_PAIRS = S5_GROUPS // 2
S5_LANE_TILES = S5_WIDTH // LANES
VMEM_LIMIT_BYTES = 56 * 1024 * 1024


def _dot(a, b):
    return jnp.dot(a, b, preferred_element_type=F32)


def _dot_nt(a, b):
    return lax.dot_general(a, b, (((1,), (1,)), ((), ())), preferred_element_type=F32)


def _params(semantics):
    return pltpu.CompilerParams(dimension_semantics=semantics, vmem_limit_bytes=VMEM_LIMIT_BYTES)


def _resident(shape):
    nd = len(shape)
    return pl.BlockSpec(shape, lambda *_: (0,) * nd, pipeline_mode=pl.Buffered(1))


def _rms_mod(x, norm_w, shift, scale):
    xn = x * lax.rsqrt(jnp.mean(x * x, axis=-1, keepdims=True) + EPS)
    return xn * (norm_w * (1.0 + scale)) + shift


ADA_COLS = 1024
ADA_STREAMS = 4


def _ada_kernel(cs_ref, cp_ref, *refs):
    w_refs, b_ref, o_ref = refs[:ADA_STREAMS], refs[ADA_STREAMS], refs[ADA_STREAMS + 1]
    pad = o_ref.shape[0] - cs_ref.shape[0] - cp_ref.shape[0]
    c = jnp.concatenate([cs_ref[...], cp_ref[...], jnp.zeros((pad, cs_ref.shape[1]), F32)], axis=0)
    s = (c * jax.nn.sigmoid(c)).astype(BF16)
    band = s.shape[1] // ADA_STREAMS
    acc = b_ref[...]
    for n, w_ref in enumerate(w_refs):
        acc = acc + _dot(s[:, n * band:(n + 1) * band], w_ref[...].astype(BF16))
    o_ref[...] = acc


def _ada(c_s, c_p, w, b):
    k, n = w.shape
    m = -(-(c_s.shape[0] + c_p.shape[0]) // 16) * 16
    band = k // ADA_STREAMS
    w_specs = [pl.BlockSpec((band, ADA_COLS), lambda j, r=r: (r, j)) for r in range(ADA_STREAMS)]
    return pl.pallas_call(
        _ada_kernel,
        out_shape=jax.ShapeDtypeStruct((m, n), F32),
        grid=(n // ADA_COLS,),
        in_specs=[pl.BlockSpec(c_s.shape, lambda j: (0, 0)), pl.BlockSpec(c_p.shape, lambda j: (0, 0))]
        + w_specs + [pl.BlockSpec((1, ADA_COLS), lambda j: (0, j))],
        out_specs=pl.BlockSpec((m, ADA_COLS), lambda j: (0, j)),
        compiler_params=_params(("arbitrary",)),
        name="ada_mod",
    )(c_s, c_p, *([w] * ADA_STREAMS), b.reshape(1, n))


FF_CHUNK = 256
FF_CHUNKS = D_FF // FF_CHUNK
FF_GROUPS = ((0, 4), (4, 8), (8, FF_CHUNKS))
FFN_ROWS = 512


def _ffn_in(x, m, norm_w):
    a_dim, r_dim, _ = x.shape
    h = _rms_mod(x, norm_w, m[..., :D_MODEL], m[..., D_MODEL:2 * D_MODEL])
    return h.reshape(a_dim * r_dim, D_MODEL).astype(BF16)


def _ffn_out(x, m, o, fin, norm_f):
    y = x + (0.5 * m[..., 2 * D_MODEL:]) * o.reshape(x.shape)
    if fin is not None:
        y = _rms_mod(y, norm_f, fin[..., :D_MODEL], fin[..., D_MODEL:])
    return y


def _swiglu_part(hb, wa, wb, wo):
    a = _dot(hb, wa)
    b = _dot(hb, wb)
    return _dot((a * jax.nn.sigmoid(a) * b).astype(BF16), wo)


def _ffn_kernel(*refs, n_prompt, final):
    if final:
        (xp_ref, mp_ref, xs_ref, ms_ref, nw_ref, wa_ref, wb_ref, wo_ref, fp_ref, fs_ref, nf_ref,
         op_ref, os_ref, win_scr, wout_scr, hb_scr, acc_scr) = refs
    else:
        (xp_ref, mp_ref, xs_ref, ms_ref, nw_ref, wa_ref, wb_ref, wo_ref,
         op_ref, os_ref, win_scr, wout_scr, hb_scr, acc_scr) = refs
        fp_ref = fs_ref = nf_ref = None
    i = pl.program_id(0)

    def finish(x_ref, m_ref, f_ref, o, o_ref):
        o_ref[...] = _ffn_out(x_ref[...], m_ref[...], o,
                              f_ref[...] if final else None, nf_ref[...] if final else None)

    @pl.when(i < FF_CHUNKS)
    def _():
        wa, wb, wo = wa_ref[...].astype(BF16), wb_ref[...].astype(BF16), wo_ref[...].astype(BF16)
        win_scr[i] = wa
        win_scr[FF_CHUNKS + i] = wb
        wout_scr[i] = wo

        @pl.when(i == 0)
        def _():
            hb_scr[...] = _ffn_in(xp_ref[...], mp_ref[...], nw_ref[...])
            acc_scr[...] = jnp.zeros_like(acc_scr)

        acc_scr[...] += _swiglu_part(hb_scr[...], wa, wb, wo)

        @pl.when(i == FF_CHUNKS - 1)
        def _():
            finish(xp_ref, mp_ref, fp_ref, acc_scr[...], op_ref)

    def run(x_ref, m_ref, f_ref, o_ref):
        hb = _ffn_in(x_ref[...], m_ref[...], nw_ref[...])
        o = None
        for c0, c1 in FF_GROUPS:
            a = jnp.concatenate([_dot(hb, win_scr[c]) for c in range(c0, c1)], axis=-1)
            b = jnp.concatenate([_dot(hb, win_scr[FF_CHUNKS + c]) for c in range(c0, c1)], axis=-1)
            act = (a * jax.nn.sigmoid(a) * b).astype(BF16)
            part = _dot(act, wout_scr[c0:c1].reshape((c1 - c0) * FF_CHUNK, D_MODEL))
            o = part if o is None else o + part
        finish(x_ref, m_ref, f_ref, o, o_ref)

    @pl.when((i >= FF_CHUNKS) & (i < FF_CHUNKS + n_prompt - 1))
    def _():
        run(xp_ref, mp_ref, fp_ref, op_ref)

    @pl.when(i == FF_CHUNKS + n_prompt - 1)
    def _():
        run(xs_ref, ms_ref, fs_ref, os_ref)


def _ffn(xp, mod_p, xs, mod_s, sub, norm_w, w_in, w_out, fin_p=None, fin_s=None, norm_f=None):
    bp, seq, _ = xp.shape
    steps, bs, _ = xs.shape
    tiles = seq // FFN_ROWS
    n_prompt = bp * tiles
    final = fin_p is not None

    def chunk(i):
        return jnp.minimum(i, FF_CHUNKS - 1)

    def tile(i):
        return jnp.clip(i - (FF_CHUNKS - 1), 0, n_prompt - 1)

    tok_p = pl.BlockSpec((1, FFN_ROWS, D_MODEL), lambda i: (tile(i) // tiles, tile(i) % tiles, 0))
    tok_s = pl.BlockSpec((steps, bs, D_MODEL), lambda i: (0, 0, 0))
    in_specs = [tok_p,
                pl.BlockSpec((1, 1, 3 * D_MODEL), lambda i: (tile(i) // tiles, 0, sub)),
                tok_s,
                pl.BlockSpec((1, bs, 3 * D_MODEL), lambda i: (0, 0, sub)),
                _resident((1, D_MODEL)),
                pl.BlockSpec((D_MODEL, FF_CHUNK), lambda i: (0, chunk(i))),
                pl.BlockSpec((D_MODEL, FF_CHUNK), lambda i: (0, FF_CHUNKS + chunk(i))),
                pl.BlockSpec((FF_CHUNK, D_MODEL), lambda i: (chunk(i), 0))]
    args = [xp, mod_p, xs, mod_s, norm_w.reshape(1, D_MODEL), w_in, w_in, w_out]
    if final:
        in_specs += [pl.BlockSpec((1, 1, 2 * D_MODEL), lambda i: (tile(i) // tiles, 0, 0)),
                     pl.BlockSpec((1, bs, 2 * D_MODEL), lambda i: (0, 0, 0)),
                     _resident((1, D_MODEL))]
        args += [fin_p, fin_s, norm_f.reshape(1, D_MODEL)]
    return pl.pallas_call(
        functools.partial(_ffn_kernel, n_prompt=n_prompt, final=final),
        out_shape=(jax.ShapeDtypeStruct(xp.shape, F32), jax.ShapeDtypeStruct(xs.shape, F32)),
        grid=(FF_CHUNKS + n_prompt,),
        in_specs=in_specs,
        out_specs=(tok_p, tok_s),
        scratch_shapes=[pltpu.VMEM((2 * FF_CHUNKS, D_MODEL, FF_CHUNK), BF16),
                        pltpu.VMEM((FF_CHUNKS, FF_CHUNK, D_MODEL), BF16),
                        pltpu.VMEM((FFN_ROWS, D_MODEL), BF16),
                        pltpu.VMEM((FFN_ROWS, D_MODEL), F32)],
        compiler_params=_params(("arbitrary",)),
        name="ffn_final" if final else "ffn",
    )(*args)


def _s5_tables(lam_re, lam_im, log_dt, b_re, b_im, c_re, c_im):
    lr, li = lam_re.astype(F32), lam_im.astype(F32)
    dt = jnp.exp(log_dt.astype(F32))[:, None]
    mag = jnp.exp(lr * dt)
    ar, ai = mag * jnp.cos(li * dt), mag * jnp.sin(li * dt)
    a2r, a2i = ar * ar - ai * ai, 2.0 * ar * ai
    den = lr * lr + li * li
    cr = ((ar - 1.0) * lr + ai * li) / den
    ci = (ai * lr - (ar - 1.0) * li) / den
    br, bi = b_re.astype(F32), b_im.astype(F32)
    bb_re = cr[:, :, None] * br - ci[:, :, None] * bi
    bb_im = cr[:, :, None] * bi + ci[:, :, None] * br
    lb_re = ar[:, :, None] * bb_re - ai[:, :, None] * bb_im
    lb_im = ar[:, :, None] * bb_im + ai[:, :, None] * bb_re
    cre, cim = c_re.astype(F32), c_im.astype(F32)
    c1_re = cre * ar[:, None, :] - cim * ai[:, None, :]
    c1_im = cre * ai[:, None, :] + cim * ar[:, None, :]
    c2_re = cre * a2r[:, None, :] - cim * a2i[:, None, :]
    c2_im = cre * a2i[:, None, :] + cim * a2r[:, None, :]

    lam2_t = jnp.broadcast_to(jnp.stack([a2r, a2i]).reshape(2, S5_PAIRS, 1, LANES),
                              (2, S5_PAIRS, SUBLANES, LANES))

    groups_per_tile = LANES // S5_GROUP
    q_of = 2 * (np.arange(S5_PAIRS) % 4)[:, None] + np.arange(2)[None, :]
    slot = (np.arange(groups_per_tile)[None, :, None] == q_of[:, None, :])
    slot = np.broadcast_to(slot[:, :, None, :, None], (S5_PAIRS, groups_per_tile, 2, 2, S5_STATE))
    slot = jnp.asarray(slot.reshape(S5_PAIRS, 1, groups_per_tile, 1, 2 * LANES), F32)

    def place(parts):
        v = jnp.stack([jnp.stack(p) for p in parts])
        v = v.reshape(2, 2, S5_PAIRS, 2, S5_GROUP, S5_STATE)
        v = jnp.transpose(v, (2, 0, 4, 1, 3, 5)).reshape(S5_PAIRS, 2, 1, S5_GROUP, 2 * LANES)
        return (v * slot).reshape(S5_PAIRS, 2 * LANES, 2 * LANES).astype(BF16)

    def cn(t):
        return jnp.transpose(t, (0, 2, 1))

    wz = place([[cn(lb_re), cn(lb_im)], [cn(bb_re), cn(bb_im)]])
    ws_t = place([[c1_re, -c1_im], [c2_re, -c2_im]])

    c_cat = jnp.concatenate([cre, -cim], axis=-1)
    x_cat = jnp.stack([jnp.concatenate([bb_re, bb_im], axis=1),
                       jnp.concatenate([lb_re, lb_im], axis=1)])
    k = jnp.einsum('gom,xgmi->xgio', c_cat, x_cat, precision=lax.Precision.HIGHEST)
    k = k.reshape(2, S5_LANE_TILES, groups_per_tile, 1, S5_GROUP, S5_GROUP)
    eye = jnp.asarray(np.eye(groups_per_tile)[None, None, :, :, None, None], F32)
    k = jnp.transpose(eye * k, (0, 1, 2, 4, 3, 5)).reshape(2, S5_LANE_TILES, LANES, LANES)
    k0, k1 = k[0], k[1]
    wk = jnp.concatenate([jnp.concatenate([k0, k1], axis=-1),
                          jnp.concatenate([jnp.zeros_like(k0), k0], axis=-1)], axis=1).astype(BF16)
    return lam2_t, wz, ws_t, wk


def _s5_prompt_kernel(x_ref, mod_ref, nw_ref, wu_ref, wz_ref, lam2_ref, ws_ref, wk_ref, d_ref, wglu_ref,
                      o_ref, hre_ref, him_ref,
                      u_slab, sre, sim, st_re, st_im, y_slab, *, tl, nb):
    i = pl.program_id(0)
    nk = tl // 2
    half = nk * nb

    @pl.when(i == 0)
    def _():
        st_re[...] = jnp.zeros_like(st_re)
        st_im[...] = jnp.zeros_like(st_im)

    x = x_ref[...]
    m = mod_ref[...]
    h = _rms_mod(x, nw_ref[...], m[..., :D_MODEL], m[..., D_MODEL:2 * D_MODEL])
    u = _dot(h.reshape(nb * tl, D_MODEL).astype(BF16), wu_ref[...])

    for s in range(S5_LANE_TILES):
        for b in range(nb):
            u_slab[s, pl.ds(b, tl, stride=nb), :] = u[b * tl:(b + 1) * tl, s * LANES:(s + 1) * LANES]

    u_ev, u_od, u_cat = [], [], []
    for s in range(S5_LANE_TILES):
        tiles = u_slab[s].reshape(nk, 2 * nb, LANES)
        ev = tiles[:, :nb, :].reshape(half, LANES)
        od = tiles[:, nb:, :].reshape(half, LANES)
        u_ev.append(ev)
        u_od.append(od)
        u_cat.append(jnp.concatenate([ev, od], axis=-1).astype(BF16))

    for j in range(S5_PAIRS):
        z = _dot(u_cat[j // 4], wz_ref[j])
        sre[j, 0:nb, :] = st_re[j]
        sim[j, 0:nb, :] = st_im[j]
        sre[j, nb:nb + half, :] = z[:, :LANES]
        sim[j, nb:nb + half, :] = z[:, LANES:]

    pairs_per_pass = 8
    for j0 in range(0, S5_PAIRS, pairs_per_pass):
        js = range(j0, j0 + pairs_per_pass)
        lr = [lam2_ref[0, j] for j in js]
        li = [lam2_ref[1, j] for j in js]
        carry = []
        for j in js:
            carry += [st_re[j], st_im[j]]
        for k in range(nk):
            r0 = (k + 1) * nb
            for n, j in enumerate(js):
                re, im = carry[2 * n], carry[2 * n + 1]
                nre = lr[n] * re - li[n] * im + sre[j, r0:r0 + nb, :]
                nim = lr[n] * im + li[n] * re + sim[j, r0:r0 + nb, :]
                if k + 1 < nk:
                    sre[j, r0:r0 + nb, :] = nre
                    sim[j, r0:r0 + nb, :] = nim
                carry[2 * n], carry[2 * n + 1] = nre, nim
        for n, j in enumerate(js):
            st_re[j] = carry[2 * n]
            st_im[j] = carry[2 * n + 1]

    y_ev, y_od = [], []
    for jt in range(S5_LANE_TILES):
        acc = _dot(u_cat[jt], wk_ref[jt])
        for jj in range(4):
            j = 4 * jt + jj
            sp = jnp.concatenate([sre[j, 0:half, :], sim[j, 0:half, :]], axis=-1).astype(BF16)
            acc = acc + _dot_nt(sp, ws_ref[j])
        d = d_ref[:, jt * LANES:(jt + 1) * LANES]
        y_ev.append(jax.nn.gelu(acc[:, :LANES] + d * u_ev[jt]))
        y_od.append(jax.nn.gelu(acc[:, LANES:] + d * u_od[jt]))
    y_ev = jnp.concatenate(y_ev, axis=-1).reshape(nk, nb, S5_WIDTH)
    y_od = jnp.concatenate(y_od, axis=-1).reshape(nk, nb, S5_WIDTH)
    y = jnp.concatenate([y_ev, y_od], axis=1).reshape(tl * nb, S5_WIDTH)
    glu = _dot(y.astype(BF16), wglu_ref[...])
    s5o = glu[:, :S5_WIDTH] * jax.nn.sigmoid(glu[:, S5_WIDTH:])
    for s in range(S5_LANE_TILES):
        y_slab[s] = s5o[:, s * LANES:(s + 1) * LANES]
    for b in range(nb):
        for s in range(S5_LANE_TILES):
            c0 = b * S5_WIDTH + s * LANES
            o_ref[:, c0:c0 + LANES] = y_slab[s, pl.ds(b, tl, stride=nb), :].astype(BF16)

    @pl.when(i == pl.num_programs(0) - 1)
    def _():
        hre_ref[...] = jnp.concatenate([st_re[j] for j in range(S5_PAIRS)], axis=-1)
        him_ref[...] = jnp.concatenate([st_im[j] for j in range(S5_PAIRS)], axis=-1)


def _s5_prompt(x1, mod3, norm_w, w_u, two_step, d_skip, w_glu, tl=128):
    nb, seq, _ = x1.shape
    lam2_t, wz, ws_t, wk = two_step
    rows = nb * tl
    half = rows // 2
    n_state = S5_GROUPS * S5_STATE
    return pl.pallas_call(
        functools.partial(_s5_prompt_kernel, tl=tl, nb=nb),
        out_shape=(jax.ShapeDtypeStruct((seq, nb * S5_WIDTH), BF16),
                   jax.ShapeDtypeStruct((nb, n_state), F32),
                   jax.ShapeDtypeStruct((nb, n_state), F32)),
        grid=(seq // tl,),
        in_specs=[pl.BlockSpec((nb, tl, D_MODEL), lambda i: (0, i, 0)),
                  pl.BlockSpec((nb, 1, 3 * D_MODEL), lambda i: (0, 0, 1)),
                  _resident((1, D_MODEL)),
                  _resident((D_MODEL, S5_WIDTH)),
                  _resident(wz.shape), _resident(lam2_t.shape), _resident(ws_t.shape), _resident(wk.shape),
                  _resident((1, S5_WIDTH)),
                  _resident((S5_WIDTH, 2 * S5_WIDTH))],
        out_specs=(pl.BlockSpec((tl, nb * S5_WIDTH), lambda i: (i, 0)),
                   pl.BlockSpec((nb, n_state), lambda i: (0, 0)),
                   pl.BlockSpec((nb, n_state), lambda i: (0, 0))),
        scratch_shapes=[pltpu.VMEM((S5_LANE_TILES, rows, LANES), F32),
                        pltpu.VMEM((S5_PAIRS, nb + half, LANES), F32),
                        pltpu.VMEM((S5_PAIRS, nb + half, LANES), F32),
                        pltpu.VMEM((S5_PAIRS, nb, LANES), F32),
                        pltpu.VMEM((S5_PAIRS, nb, LANES), F32),
                        pltpu.VMEM((S5_LANE_TILES, rows, LANES), F32)],
        compiler_params=_params(("arbitrary",)),
        name="s5_prompt",
    )(x1, mod3, norm_w.reshape(1, D_MODEL), w_u, wz, lam2_t, ws_t, wk, d_skip.reshape(1, S5_WIDTH), w_glu)


def _rotary_tables(pos):
    half = RET_DK // 2
    inv = ROPE_BASE ** (-np.arange(half, dtype=np.float64) / half)
    ang = np.asarray(pos, np.float64)[:, None] * inv[None, :]
    cos, sin = np.cos(ang), np.sin(ang)
    return (jnp.asarray(np.concatenate([cos, cos], axis=-1), F32),
            jnp.asarray(np.concatenate([-sin, sin], axis=-1), F32))


def _decay_tables(chunk, rows):
    lg = np.log1p(-np.exp2(-5.0 - np.arange(RET_HEADS, dtype=np.float64)))
    idx = np.arange(rows, dtype=np.float64)
    valid = idx < chunk
    diff = idx[:, None] - idx[None, :]
    intra = np.where((diff >= 0) & valid[:, None] & valid[None, :],
                     np.exp(lg[:, None, None] * np.maximum(diff, 0.0)), 0.0)
    q_dec = np.where(valid[None, :], np.exp(lg[:, None] * (idx[None, :] + 1.0)), 0.0)
    k_dec = np.where(valid[None, :], np.exp(lg[:, None] * (chunk - 1.0 - idx)[None, :]), 0.0)
    c_dec = np.exp(lg * chunk)
    q_dec = np.broadcast_to(q_dec[:, :, None], (RET_HEADS, rows, RET_DV))
    k_dec = np.broadcast_to(k_dec[:, :, None], (RET_HEADS, rows, RET_DK))
    c_dec = np.broadcast_to(c_dec[:, None, None], (RET_HEADS, 1, RET_DV))
    return tuple(jnp.asarray(t, F32) for t in (intra, q_dec, k_dec, c_dec))


def _rotary(x, cs, sn):
    return x * cs + pltpu.roll(x, RET_DK // 2, axis=1) * sn


def _head_norm_gate(ret, g):
    mu = jnp.mean(ret, axis=-1, keepdims=True)
    cen = ret - mu
    var = jnp.mean(cen * cen, axis=-1, keepdims=True)
    return (g * jax.nn.sigmoid(g)) * (cen * lax.rsqrt(var + EPS))


def _ret_prompt_kernel(x_ref, mod_ref, nw_ref, wq_ref, cs_ref, sn_ref, intra_ref, qd_ref, kd_ref, cd_ref,
                       s5o_ref, wout_ref, o_ref, sout_ref, s_scr, *, tm):
    t = pl.program_id(1)

    @pl.when(t == 0)
    def _():
        s_scr[...] = jnp.zeros_like(s_scr)

    x = x_ref[0]
    m = mod_ref[0]
    h = _rms_mod(x, nw_ref[...], m[:, :D_MODEL], m[:, D_MODEL:2 * D_MODEL])
    proj = _dot(h.astype(BF16), wq_ref[...])
    cs, sn = cs_ref[...], sn_ref[...]
    n_chunks = tm // RET_CHUNK
    heads = range(RET_HEADS)
    chunks = range(n_chunks)

    lhs, vbs, kvs = {}, {}, {}
    for hd in heads:
        lo = hd * RET_DK
        q = _rotary(proj[:, lo:lo + RET_DK], cs, sn)
        k = _rotary(proj[:, RET_WIDTH + lo:RET_WIDTH + lo + RET_DK], cs, sn) * (RET_DK ** -0.5)
        v = proj[:, 2 * RET_WIDTH + lo:2 * RET_WIDTH + lo + RET_DV]
        for c in chunks:
            rows = slice(c * RET_CHUNK, (c + 1) * RET_CHUNK)
            qc, kc, vb = q[rows], k[rows], v[rows].astype(BF16)
            scores = _dot_nt(qc.astype(BF16), kc.astype(BF16)) * intra_ref[hd]
            lhs[hd, c] = jnp.concatenate([scores.astype(BF16), (qc * qd_ref[hd]).astype(BF16)], axis=-1)
            vbs[hd, c] = vb
            kvs[hd, c] = _dot(jnp.transpose(kc * kd_ref[hd]).astype(BF16), vb)

    states = {}
    for hd in heads:
        s = s_scr[hd]
        for c in chunks:
            states[hd, c] = s
            s = s * cd_ref[hd] + kvs[hd, c]
        s_scr[hd] = s

    parts = [s5o_ref[...]]
    for hd in heads:
        lo = 3 * RET_WIDTH + hd * RET_DV
        outs = [_dot(lhs[hd, c], jnp.concatenate([vbs[hd, c], states[hd, c].astype(BF16)], axis=0))
                for c in chunks]
        parts.append(_head_norm_gate(jnp.concatenate(outs, axis=0), proj[:, lo:lo + RET_DV]).astype(BF16))
    mix = jnp.concatenate(parts, axis=-1)
    o_ref[0] = x + m[:, 2 * D_MODEL:] * _dot(mix, wout_ref[...])

    @pl.when(t == pl.num_programs(1) - 1)
    def _():
        sout_ref[0] = s_scr[...]


def _ret_prompt(x1, mod3, norm_w, w_qkvg, cs, sn, decay, s5o, w_out, tm=1024):
    nb, seq, _ = x1.shape
    intra, q_dec, k_dec, c_dec = decay
    return pl.pallas_call(
        functools.partial(_ret_prompt_kernel, tm=tm),
        out_shape=(jax.ShapeDtypeStruct(x1.shape, F32),
                   jax.ShapeDtypeStruct((nb, RET_HEADS, RET_DK, RET_DV), F32)),
        grid=(nb, seq // tm),
        in_specs=[pl.BlockSpec((1, tm, D_MODEL), lambda b, t: (b, t, 0)),
                  pl.BlockSpec((1, 1, 3 * D_MODEL), lambda b, t: (b, 0, 1)),
                  _resident((1, D_MODEL)),
                  _resident((D_MODEL, 4 * RET_WIDTH)),
                  pl.BlockSpec((tm, RET_DK), lambda b, t: (t, 0)),
                  pl.BlockSpec((tm, RET_DK), lambda b, t: (t, 0)),
                  _resident(intra.shape), _resident(q_dec.shape), _resident(k_dec.shape), _resident(c_dec.shape),
                  pl.BlockSpec((tm, S5_WIDTH), lambda b, t: (t, b)),
                  _resident((D_MODEL, D_MODEL))],
        out_specs=(pl.BlockSpec((1, tm, D_MODEL), lambda b, t: (b, t, 0)),
                   pl.BlockSpec((1, RET_HEADS, RET_DK, RET_DV), lambda b, t: (b, 0, 0, 0))),
        scratch_shapes=[pltpu.VMEM((RET_HEADS, RET_DK, RET_DV), F32)],
        compiler_params=_params(("arbitrary", "arbitrary")),
        name="ret_prompt",
    )(x1, mod3, norm_w.reshape(1, D_MODEL), w_qkvg, cs, sn, intra, q_dec, k_dec, c_dec, s5o, w_out)


def _mix_in_sample_kernel(x_ref, mod_ref, nw_ref, wu_ref, wq_ref, wz_ref, lam2_ref, ws_ref, wk_ref, d_ref,
                          wglu_ref, h0re_ref, h0im_ref, qkvg_ref, s5o_ref, hre_ref, him_ref, *, steps, nb):
    x = x_ref[...]
    m = mod_ref[...]
    h = _rms_mod(x, nw_ref[...], m[..., :D_MODEL], m[..., D_MODEL:2 * D_MODEL])
    hb = h.reshape(steps * nb, D_MODEL).astype(BF16)
    proj = _dot(hb, wq_ref[...])
    qkvg_ref[...] = jnp.zeros_like(qkvg_ref)
    for s in range(4 * RET_HEADS):
        for t in range(steps):
            qkvg_ref[s, pl.ds(t, nb, stride=SUBLANES), :] = proj[t * nb:(t + 1) * nb, s * LANES:(s + 1) * LANES]
    u = _dot(hb, wu_ref[...])
    nk = steps // 2

    def rows_of(parity, lanes):
        return jnp.concatenate([u[(2 * k + parity) * nb:(2 * k + parity + 1) * nb, lanes] for k in range(nk)],
                               axis=0)

    u_ev, u_od, u_cat = [], [], []
    for s in range(S5_LANE_TILES):
        lanes = slice(s * LANES, (s + 1) * LANES)
        u_ev.append(rows_of(0, lanes))
        u_od.append(rows_of(1, lanes))
        u_cat.append(jnp.concatenate([u_ev[s], u_od[s]], axis=-1).astype(BF16))

    prev = []
    for j in range(S5_PAIRS):
        z = _dot(u_cat[j // 4], wz_ref[j])
        lr = lam2_ref[0, j][:1]
        li = lam2_ref[1, j][:1]
        sre = h0re_ref[:, j * LANES:(j + 1) * LANES]
        sim = h0im_ref[:, j * LANES:(j + 1) * LANES]
        pre, pim = [], []
        for k in range(nk):
            pre.append(sre)
            pim.append(sim)
            zre, zim = z[k * nb:(k + 1) * nb, :LANES], z[k * nb:(k + 1) * nb, LANES:]
            sre, sim = lr * sre - li * sim + zre, lr * sim + li * sre + zim
        hre_ref[:, j * LANES:(j + 1) * LANES] = sre
        him_ref[:, j * LANES:(j + 1) * LANES] = sim
        prev.append(jnp.concatenate([jnp.concatenate(pre, axis=0), jnp.concatenate(pim, axis=0)],
                                    axis=-1).astype(BF16))

    y_ev, y_od = [], []
    for jt in range(S5_LANE_TILES):
        acc = _dot(u_cat[jt], wk_ref[jt])
        for jj in range(4):
            acc = acc + _dot_nt(prev[4 * jt + jj], ws_ref[4 * jt + jj])
        d = d_ref[:, jt * LANES:(jt + 1) * LANES]
        y_ev.append(jax.nn.gelu(acc[:, :LANES] + d * u_ev[jt]))
        y_od.append(jax.nn.gelu(acc[:, LANES:] + d * u_od[jt]))
    y_ev = jnp.concatenate(y_ev, axis=-1)
    y_od = jnp.concatenate(y_od, axis=-1)
    y = jnp.concatenate([part[k * nb:(k + 1) * nb] for k in range(nk) for part in (y_ev, y_od)], axis=0)
    glu = _dot(y.astype(BF16), wglu_ref[...])
    s5o_ref[...] = (glu[:, :S5_WIDTH] * jax.nn.sigmoid(glu[:, S5_WIDTH:])).astype(BF16)


def _mix_in_sample(x1, mod3, norm_w, w_u, w_qkvg, two_step, d_skip, w_glu, h0_re, h0_im):
    steps, nb, _ = x1.shape
    rows = steps * nb
    n_state = S5_GROUPS * S5_STATE
    slabs = (4 * RET_HEADS, nb * SUBLANES, LANES)
    lam2_t, wz, ws_t, wk = two_step
    args = (x1, mod3, norm_w.reshape(1, D_MODEL), w_u, w_qkvg, wz, lam2_t, ws_t, wk,
            d_skip.reshape(1, S5_WIDTH), w_glu, h0_re, h0_im)
    in_specs = [pl.BlockSpec((steps, nb, D_MODEL), lambda i: (0, 0, 0)),
                pl.BlockSpec((1, nb, 3 * D_MODEL), lambda i: (0, 0, 1))]
    in_specs += [_resident(a.shape) for a in args[2:]]
    return pl.pallas_call(
        functools.partial(_mix_in_sample_kernel, steps=steps, nb=nb),
        out_shape=(jax.ShapeDtypeStruct(slabs, F32),
                   jax.ShapeDtypeStruct((rows, S5_WIDTH), BF16),
                   jax.ShapeDtypeStruct((nb, n_state), F32),
                   jax.ShapeDtypeStruct((nb, n_state), F32)),
        grid=(1,),
        in_specs=in_specs,
        out_specs=(pl.BlockSpec(slabs, lambda i: (0, 0, 0)),
                   pl.BlockSpec((rows, S5_WIDTH), lambda i: (0, 0)),
                   pl.BlockSpec((nb, n_state), lambda i: (0, 0)),
                   pl.BlockSpec((nb, n_state), lambda i: (0, 0))),
        compiler_params=_params(("arbitrary",)),
        name="mix_in_sample",
    )(*args)


def _ret_sample_kernel(qkvg_ref, s_ref, cs_ref, sn_ref, intra_ref, qd_ref, kd_ref, cd_ref,
                       o_ref, sout_ref, *, bb):
    cs, sn = cs_ref[...], sn_ref[...]
    pairs = [(b, hd) for b in range(bb) for hd in range(RET_HEADS)]

    def tile(kind, b, hd):
        return qkvg_ref[kind * RET_HEADS + hd, b * SUBLANES:(b + 1) * SUBLANES, :]

    scores, cross_lhs, vbs = {}, {}, {}
    for b, hd in pairs:
        q = _rotary(tile(0, b, hd), cs, sn)
        k = _rotary(tile(1, b, hd), cs, sn) * (RET_DK ** -0.5)
        vb = tile(2, b, hd).astype(BF16)
        scores[b, hd] = _dot_nt(q.astype(BF16), k.astype(BF16)) * intra_ref[hd]
        cross_lhs[b, hd] = (q * qd_ref[hd]).astype(BF16)
        vbs[b, hd] = vb
        sout_ref[b, hd] = s_ref[b, hd] * cd_ref[hd] + _dot(jnp.transpose(k * kd_ref[hd]).astype(BF16), vb)
    for b, hd in pairs:
        o = _dot(scores[b, hd].astype(BF16), vbs[b, hd]) + _dot(cross_lhs[b, hd], s_ref[b, hd].astype(BF16))
        o_ref[hd, b * SUBLANES:(b + 1) * SUBLANES, :] = _head_norm_gate(o, tile(3, b, hd))


def _ret_sample(qkvg, s0, cs, sn, decay, bb=16):
    nb = s0.shape[0]
    intra, q_dec, k_dec, c_dec = decay
    st = pl.BlockSpec((bb, RET_HEADS, RET_DK, RET_DV), lambda i: (i, 0, 0, 0))
    return pl.pallas_call(
        functools.partial(_ret_sample_kernel, bb=bb),
        out_shape=(jax.ShapeDtypeStruct((RET_HEADS, nb * SUBLANES, LANES), F32),
                   jax.ShapeDtypeStruct(s0.shape, F32)),
        grid=(nb // bb,),
        in_specs=[pl.BlockSpec((4 * RET_HEADS, bb * SUBLANES, LANES), lambda i: (0, i, 0)),
                  st, _resident(cs.shape), _resident(sn.shape),
                  _resident(intra.shape), _resident(q_dec.shape), _resident(k_dec.shape), _resident(c_dec.shape)],
        out_specs=(pl.BlockSpec((RET_HEADS, bb * SUBLANES, LANES), lambda i: (0, i, 0)), st),
        compiler_params=_params(("arbitrary",)),
        name="ret_sample",
    )(qkvg, s0, cs, sn, intra, q_dec, k_dec, c_dec)


def _mix_out_sample_kernel(x_ref, mod_ref, s5o_ref, ret_ref, wout_ref, o_ref, *, steps, nb):
    x = x_ref[...]
    gate = mod_ref[...][..., 2 * D_MODEL:]
    ret = jnp.concatenate(
        [jnp.concatenate([ret_ref[hd, pl.ds(t, nb, stride=SUBLANES), :] for hd in range(RET_HEADS)], axis=-1)
         for t in range(steps)], axis=0)
    mix = jnp.concatenate([s5o_ref[...], ret.astype(BF16)], axis=-1)
    o_ref[...] = x + gate * _dot(mix, wout_ref[...]).reshape(steps, nb, D_MODEL)


def _mix_out_sample(x1, mod3, s5o, ret, w_out):
    steps, nb, _ = x1.shape
    rows = steps * nb
    return pl.pallas_call(
        functools.partial(_mix_out_sample_kernel, steps=steps, nb=nb),
        out_shape=jax.ShapeDtypeStruct(x1.shape, F32),
        grid=(1,),
        in_specs=[pl.BlockSpec((steps, nb, D_MODEL), lambda i: (0, 0, 0)),
                  pl.BlockSpec((1, nb, 3 * D_MODEL), lambda i: (0, 0, 1)),
                  pl.BlockSpec((rows, S5_WIDTH), lambda i: (0, 0)),
                  pl.BlockSpec(ret.shape, lambda i: (0, 0, 0)),
                  _resident((D_MODEL, D_MODEL))],
        out_specs=pl.BlockSpec((steps, nb, D_MODEL), lambda i: (0, 0, 0)),
        compiler_params=_params(("arbitrary",)),
        name="mix_out_sample",
    )(x1, mod3, s5o, ret, w_out)


def kernel(x_prompt, x_sample, state_ssm_re, state_ssm_im, state_ret, c_prompt, c_sample,
           w_ada, b_ada, norm_ffn1, ffn1_w_in, ffn1_w_out, norm_mix, w_in_mix,
           s5_lambda_re, s5_lambda_im, s5_log_dt, s5_b_re, s5_b_im, s5_c_re, s5_c_im, s5_d, s5_w_glu,
           w_out_mix, norm_ffn2, ffn2_w_in, ffn2_w_out, w_ada_final, b_ada_final, norm_final):
    depth = w_ada.shape[0]
    bp, seq, _ = x_prompt.shape
    bs, steps, _ = x_sample.shape
    assert seq % RET_CHUNK == 0 and steps % 2 == 0 and steps <= SUBLANES

    c_rows = bs + bp

    cs_p, sn_p = _rotary_tables(np.arange(seq))
    decay_p = _decay_tables(RET_CHUNK, RET_CHUNK)
    rows_s = SUBLANES
    pos_s = np.concatenate([PAST_LEN + np.arange(steps), np.zeros(rows_s - steps)])
    cs_s, sn_s = _rotary_tables(pos_s)
    decay_s = _decay_tables(steps, rows_s)

    xp = x_prompt
    xs = jnp.transpose(x_sample, (1, 0, 2))
    outs = {k: [] for k in ("p_re", "p_im", "p_ret", "s_re", "s_im", "s_ret")}
    for l in range(depth):
        mod = _ada(c_sample, c_prompt, w_ada[l], b_ada[l])
        mod_p = mod[bs:c_rows][:, None, :]
        mod_s = mod[None]
        w_u = w_in_mix[l][:, :S5_WIDTH].astype(BF16)
        w_qkvg = w_in_mix[l][:, S5_WIDTH:].astype(BF16)
        w_glu = s5_w_glu[l].astype(BF16)
        w_out = w_out_mix[l].astype(BF16)
        two_step = _s5_tables(s5_lambda_re[l], s5_lambda_im[l], s5_log_dt[l],
                              s5_b_re[l], s5_b_im[l], s5_c_re[l], s5_c_im[l])
        last = l == depth - 1
        if last:
            fin = _ada(c_sample, c_prompt, w_ada_final, b_ada_final)
            fin_p, fin_s = fin[bs:c_rows][:, None, :], fin[None]
        else:
            fin_p = fin_s = None

        xp, xs = _ffn(xp, mod_p, xs, mod_s, 0, norm_ffn1[l], ffn1_w_in[l], ffn1_w_out[l])

        s5o, hre, him = _s5_prompt(xp, mod_p, norm_mix[l], w_u, two_step, s5_d[l], w_glu)
        xp, sret = _ret_prompt(xp, mod_p, norm_mix[l], w_qkvg, cs_p, sn_p, decay_p, s5o, w_out)
        outs["p_re"].append(hre.reshape(bp, S5_GROUPS, S5_STATE))
        outs["p_im"].append(him.reshape(bp, S5_GROUPS, S5_STATE))
        outs["p_ret"].append(sret)

        qkvg, s5o_s, hre_s, him_s = _mix_in_sample(
            xs, mod_s, norm_mix[l], w_u, w_qkvg, two_step, s5_d[l], w_glu,
            state_ssm_re[l].reshape(bs, -1), state_ssm_im[l].reshape(bs, -1))
        ret_s, sret_s = _ret_sample(qkvg, state_ret[l], cs_s, sn_s, decay_s)
        xs = _mix_out_sample(xs, mod_s, s5o_s, ret_s, w_out)

        xp, xs = _ffn(xp, mod_p, xs, mod_s, 2, norm_ffn2[l], ffn2_w_in[l], ffn2_w_out[l],
                      fin_p, fin_s, norm_final if last else None)
        outs["s_re"].append(hre_s.reshape(bs, S5_GROUPS, S5_STATE))
        outs["s_im"].append(him_s.reshape(bs, S5_GROUPS, S5_STATE))
        outs["s_ret"].append(sret_s)

    y_prompt = xp
    y_sample = jnp.transpose(xs, (1, 0, 2))
    return (y_prompt, y_sample, jnp.stack(outs["p_re"]), jnp.stack(outs["p_im"]), jnp.stack(outs["p_ret"]),
            jnp.stack(outs["s_re"]), jnp.stack(outs["s_im"]), jnp.stack(outs["s_ret"]))
```

```python
import functools

import numpy as np
import jax
import jax.numpy as jnp
from jax import lax
from jax.experimental import pallas as pl
from jax.experimental.pallas import tpu as pltpu

F32 = jnp.float32
BF16 = jnp.bfloat16

D_MODEL = 1024
D_FF = 2816
N_MOD = 9
S5_WIDTH = 512
S5_GROUP = 16
S5_GROUPS = 32
S5_STATE = 64
RET_HEADS = 4
RET_DK = 128
RET_DV = 128
RET_WIDTH = 512
RET_CHUNK = 128
PAST_LEN = 16384
ROPE_BASE = 10000.0
EPS = 1e-6

LANES = 128
SUBLANES = 8
S5_PAIRS = S5_GROUPS // 2
S5_LANE_TILES = S5_WIDTH // LANES
VMEM_LIMIT_BYTES = 56 * 1024 * 1024


def _dot(a, b):
    return jnp.dot(a, b, preferred_element_type=F32)


def _dot_nt(a, b):
    return lax.dot_general(a, b, (((1,), (1,)), ((), ())), preferred_element_type=F32)


def _params(semantics):
    return pltpu.CompilerParams(dimension_semantics=semantics, vmem_limit_bytes=VMEM_LIMIT_BYTES)


def _resident(shape):
    nd = len(shape)
    return pl.BlockSpec(shape, lambda *_: (0,) * nd, pipeline_mode=pl.Buffered(1))


def _rms_mod(x, norm_w, shift, scale):
    xn = x * lax.rsqrt(jnp.mean(x * x, axis=-1, keepdims=True) + EPS)
    return xn * (norm_w * (1.0 + scale)) + shift


ADA_COLS = 1024
ADA_STREAMS = 4


def _ada_kernel(cs_ref, cp_ref, *refs):
    w_refs, b_ref, o_ref = refs[:ADA_STREAMS], refs[ADA_STREAMS], refs[ADA_STREAMS + 1]
    pad = o_ref.shape[0] - cs_ref.shape[0] - cp_ref.shape[0]
    c = jnp.concatenate([cs_ref[...], cp_ref[...], jnp.zeros((pad, cs_ref.shape[1]), F32)], axis=0)
    s = (c * jax.nn.sigmoid(c)).astype(BF16)
    band = s.shape[1] // ADA_STREAMS
    acc = b_ref[...]
    for n, w_ref in enumerate(w_refs):
        acc = acc + _dot(s[:, n * band:(n + 1) * band], w_ref[...].astype(BF16))
    o_ref[...] = acc


def _ada(c_s, c_p, w, b):
    k, n = w.shape
    m = -(-(c_s.shape[0] + c_p.shape[0]) // 16) * 16
    band = k // ADA_STREAMS
    w_specs = [pl.BlockSpec((band, ADA_COLS), lambda j, r=r: (r, j)) for r in range(ADA_STREAMS)]
    return pl.pallas_call(
        _ada_kernel,
        out_shape=jax.ShapeDtypeStruct((m, n), F32),
        grid=(n // ADA_COLS,),
        in_specs=[pl.BlockSpec(c_s.shape, lambda j: (0, 0)), pl.BlockSpec(c_p.shape, lambda j: (0, 0))]
        + w_specs + [pl.BlockSpec((1, ADA_COLS), lambda j: (0, j))],
        out_specs=pl.BlockSpec((m, ADA_COLS), lambda j: (0, j)),
        compiler_params=_params(("arbitrary",)),
        name="ada_mod",
    )(c_s, c_p, *([w] * ADA_STREAMS), b.reshape(1, n))


FF_CHUNK = 256
FF_CHUNKS = D_FF // FF_CHUNK
FF_GROUPS = ((0, 6), (6, FF_CHUNKS))
FFN_ROWS = 512


def _ffn_in(x, m, norm_w):
    a_dim, r_dim, _ = x.shape
    h = _rms_mod(x, norm_w, m[..., :D_MODEL], m[..., D_MODEL:2 * D_MODEL])
    return h.reshape(a_dim * r_dim, D_MODEL).astype(BF16)


def _ffn_out(x, m, o, fin, norm_f):
    y = x + (0.5 * m[..., 2 * D_MODEL:]) * o.reshape(x.shape)
    if fin is not None:
        y = _rms_mod(y, norm_f, fin[..., :D_MODEL], fin[..., D_MODEL:])
    return y


def _swiglu_part(hb, wa, wb, wo):
    a = _dot(hb, wa)
    b = _dot(hb, wb)
    return _dot((a * jax.nn.sigmoid(a) * b).astype(BF16), wo)


def _ffn_kernel(*refs, n_prompt, final):
    if final:
        (xp_ref, mp_ref, xs_ref, ms_ref, nw_ref, wa_ref, wb_ref, wo_ref, fp_ref, fs_ref, nf_ref,
         op_ref, os_ref, win_scr, wout_scr, hb_scr, acc_scr) = refs
    else:
        (xp_ref, mp_ref, xs_ref, ms_ref, nw_ref, wa_ref, wb_ref, wo_ref,
         op_ref, os_ref, win_scr, wout_scr, hb_scr, acc_scr) = refs
        fp_ref = fs_ref = nf_ref = None
    i = pl.program_id(0)

    def finish(x_ref, m_ref, f_ref, o, o_ref):
        o_ref[...] = _ffn_out(x_ref[...], m_ref[...], o,
                              f_ref[...] if final else None, nf_ref[...] if final else None)

    @pl.when(i < FF_CHUNKS)
    def _():
        wa, wb, wo = wa_ref[...].astype(BF16), wb_ref[...].astype(BF16), wo_ref[...].astype(BF16)
        win_scr[i] = wa
        win_scr[FF_CHUNKS + i] = wb
        wout_scr[i] = wo

        @pl.when(i == 0)
        def _():
            hb_scr[...] = _ffn_in(xp_ref[...], mp_ref[...], nw_ref[...])
            acc_scr[...] = jnp.zeros_like(acc_scr)

        acc_scr[...] += _swiglu_part(hb_scr[...], wa, wb, wo)

        @pl.when(i == FF_CHUNKS - 1)
        def _():
            finish(xp_ref, mp_ref, fp_ref, acc_scr[...], op_ref)

    def run(x_ref, m_ref, f_ref, o_ref):
        hb = _ffn_in(x_ref[...], m_ref[...], nw_ref[...])
        o = None
        for c0, c1 in FF_GROUPS:
            a = jnp.concatenate([_dot(hb, win_scr[c]) for c in range(c0, c1)], axis=-1)
            b = jnp.concatenate([_dot(hb, win_scr[FF_CHUNKS + c]) for c in range(c0, c1)], axis=-1)
            act = (a * jax.nn.sigmoid(a) * b).astype(BF16)
            part = _dot(act, wout_scr[c0:c1].reshape((c1 - c0) * FF_CHUNK, D_MODEL))
            o = part if o is None else o + part
        finish(x_ref, m_ref, f_ref, o, o_ref)

    @pl.when((i >= FF_CHUNKS) & (i < FF_CHUNKS + n_prompt - 1))
    def _():
        run(xp_ref, mp_ref, fp_ref, op_ref)

    @pl.when(i == FF_CHUNKS + n_prompt - 1)
    def _():
        run(xs_ref, ms_ref, fs_ref, os_ref)


def _ffn(xp, mod_p, xs, mod_s, sub, norm_w, w_in, w_out, fin_p=None, fin_s=None, norm_f=None):
    bp, seq, _ = xp.shape
    steps, bs, _ = xs.shape
    tiles = seq // FFN_ROWS
    n_prompt = bp * tiles
    final = fin_p is not None

    def chunk(i):
        return jnp.minimum(i, FF_CHUNKS - 1)

    def tile(i):
        return jnp.clip(i - (FF_CHUNKS - 1), 0, n_prompt - 1)

    tok_p = pl.BlockSpec((1, FFN_ROWS, D_MODEL), lambda i: (tile(i) // tiles, tile(i) % tiles, 0))
    tok_s = pl.BlockSpec((steps, bs, D_MODEL), lambda i: (0, 0, 0))
    in_specs = [tok_p,
                pl.BlockSpec((1, 1, 3 * D_MODEL), lambda i: (tile(i) // tiles, 0, sub)),
                tok_s,
                pl.BlockSpec((1, bs, 3 * D_MODEL), lambda i: (0, 0, sub)),
                _resident((1, D_MODEL)),
                pl.BlockSpec((D_MODEL, FF_CHUNK), lambda i: (0, chunk(i))),
                pl.BlockSpec((D_MODEL, FF_CHUNK), lambda i: (0, FF_CHUNKS + chunk(i))),
                pl.BlockSpec((FF_CHUNK, D_MODEL), lambda i: (chunk(i), 0))]
    args = [xp, mod_p, xs, mod_s, norm_w.reshape(1, D_MODEL), w_in, w_in, w_out]
    if final:
        in_specs += [pl.BlockSpec((1, 1, 2 * D_MODEL), lambda i: (tile(i) // tiles, 0, 0)),
                     pl.BlockSpec((1, bs, 2 * D_MODEL), lambda i: (0, 0, 0)),
                     _resident((1, D_MODEL))]
        args += [fin_p, fin_s, norm_f.reshape(1, D_MODEL)]
    return pl.pallas_call(
        functools.partial(_ffn_kernel, n_prompt=n_prompt, final=final),
        out_shape=(jax.ShapeDtypeStruct(xp.shape, F32), jax.ShapeDtypeStruct(xs.shape, F32)),
        grid=(FF_CHUNKS + n_prompt,),
        in_specs=in_specs,
        out_specs=(tok_p, tok_s),
        scratch_shapes=[pltpu.VMEM((2 * FF_CHUNKS, D_MODEL, FF_CHUNK), BF16),
                        pltpu.VMEM((FF_CHUNKS, FF_CHUNK, D_MODEL), BF16),
                        pltpu.VMEM((FFN_ROWS, D_MODEL), BF16),
                        pltpu.VMEM((FFN_ROWS, D_MODEL), F32)],
        compiler_params=_params(("arbitrary",)),
        name="ffn_final" if final else "ffn",
    )(*args)


def _s5_tables(lam_re, lam_im, log_dt, b_re, b_im, c_re, c_im):
    lr, li = lam_re.astype(F32), lam_im.astype(F32)
    dt = jnp.exp(log_dt.astype(F32))[:, None]
    mag = jnp.exp(lr * dt)
    ar, ai = mag * jnp.cos(li * dt), mag * jnp.sin(li * dt)
    a2r, a2i = ar * ar - ai * ai, 2.0 * ar * ai
    den = lr * lr + li * li
    cr = ((ar - 1.0) * lr + ai * li) / den
    ci = (ai * lr - (ar - 1.0) * li) / den
    br, bi = b_re.astype(F32), b_im.astype(F32)
    bb_re = cr[:, :, None] * br - ci[:, :, None] * bi
    bb_im = cr[:, :, None] * bi + ci[:, :, None] * br
    lb_re = ar[:, :, None] * bb_re - ai[:, :, None] * bb_im
    lb_im = ar[:, :, None] * bb_im + ai[:, :, None] * bb_re
    cre, cim = c_re.astype(F32), c_im.astype(F32)
    c1_re = cre * ar[:, None, :] - cim * ai[:, None, :]
    c1_im = cre * ai[:, None, :] + cim * ar[:, None, :]
    c2_re = cre * a2r[:, None, :] - cim * a2i[:, None, :]
    c2_im = cre * a2i[:, None, :] + cim * a2r[:, None, :]

    lam2_t = jnp.broadcast_to(jnp.stack([a2r, a2i]).reshape(2, S5_PAIRS, 1, LANES),
                              (2, S5_PAIRS, SUBLANES, LANES))

    groups_per_tile = LANES // S5_GROUP
    q_of = 2 * (np.arange(S5_PAIRS) % 4)[:, None] + np.arange(2)[None, :]
    slot = (np.arange(groups_per_tile)[None, :, None] == q_of[:, None, :])
    slot = np.broadcast_to(slot[:, :, None, :, None], (S5_PAIRS, groups_per_tile, 2, 2, S5_STATE))
    slot = jnp.asarray(slot.reshape(S5_PAIRS, 1, groups_per_tile, 1, 2 * LANES), F32)

    def place(parts):
        v = jnp.stack([jnp.stack(p) for p in parts])
        v = v.reshape(2, 2, S5_PAIRS, 2, S5_GROUP, S5_STATE)
        v = jnp.transpose(v, (2, 0, 4, 1, 3, 5)).reshape(S5_PAIRS, 2, 1, S5_GROUP, 2 * LANES)
        return (v * slot).reshape(S5_PAIRS, 2 * LANES, 2 * LANES).astype(BF16)

    def cn(t):
        return jnp.transpose(t, (0, 2, 1))

    wz = place([[cn(lb_re), cn(lb_im)], [cn(bb_re), cn(bb_im)]])
    ws_t = place([[c1_re, -c1_im], [c2_re, -c2_im]])

    c_cat = jnp.concatenate([cre, -cim], axis=-1)
    x_cat = jnp.stack([jnp.concatenate([bb_re, bb_im], axis=1),
                       jnp.concatenate([lb_re, lb_im], axis=1)])
    k = jnp.einsum('gom,xgmi->xgio', c_cat, x_cat, precision=lax.Precision.HIGHEST)
    k = k.reshape(2, S5_LANE_TILES, groups_per_tile, 1, S5_GROUP, S5_GROUP)
    eye = jnp.asarray(np.eye(groups_per_tile)[None, None, :, :, None, None], F32)
    k = jnp.transpose(eye * k, (0, 1, 2, 4, 3, 5)).reshape(2, S5_LANE_TILES, LANES, LANES)
    k0, k1 = k[0], k[1]
    wk = jnp.concatenate([jnp.concatenate([k0, k1], axis=-1),
                          jnp.concatenate([jnp.zeros_like(k0), k0], axis=-1)], axis=1).astype(BF16)
    return lam2_t, wz, ws_t, wk


def _s5_prompt_kernel(x_ref, mod_ref, nw_ref, wu_ref, wz_ref, lam2_ref, ws_ref, wk_ref, d_ref, wglu_ref,
                      o_ref, hre_ref, him_ref,
                      u_slab, sre, sim, st_re, st_im, y_slab, *, tl, nb):
    i = pl.program_id(0)
    nk = tl // 2
    half = nk * nb

    @pl.when(i == 0)
    def _():
        st_re[...] = jnp.zeros_like(st_re)
        st_im[...] = jnp.zeros_like(st_im)

    x = x_ref[...]
    m = mod_ref[...]
    h = _rms_mod(x, nw_ref[...], m[..., :D_MODEL], m[..., D_MODEL:2 * D_MODEL])
    u = _dot(h.reshape(nb * tl, D_MODEL).astype(BF16), wu_ref[...])

    for s in range(S5_LANE_TILES):
        for b in range(nb):
            u_slab[s, pl.ds(b, tl, stride=nb), :] = u[b * tl:(b + 1) * tl, s * LANES:(s + 1) * LANES]

    u_ev, u_od, u_cat = [], [], []
    for s in range(S5_LANE_TILES):
        tiles = u_slab[s].reshape(nk, 2 * nb, LANES)
        ev = tiles[:, :nb, :].reshape(half, LANES)
        od = tiles[:, nb:, :].reshape(half, LANES)
        u_ev.append(ev)
        u_od.append(od)
        u_cat.append(jnp.concatenate([ev, od], axis=-1).astype(BF16))

    for j in range(S5_PAIRS):
        z = _dot(u_cat[j // 4], wz_ref[j])
        sre[j, 0:nb, :] = st_re[j]
        sim[j, 0:nb, :] = st_im[j]
        sre[j, nb:nb + half, :] = z[:, :LANES]
        sim[j, nb:nb + half, :] = z[:, LANES:]

    pairs_per_pass = 8
    for j0 in range(0, S5_PAIRS, pairs_per_pass):
        js = range(j0, j0 + pairs_per_pass)
        lr = [lam2_ref[0, j] for j in js]
        li = [lam2_ref[1, j] for j in js]
        carry = []
        for j in js:
            carry += [st_re[j], st_im[j]]
        for k in range(nk):
            r0 = (k + 1) * nb
            for n, j in enumerate(js):
                re, im = carry[2 * n], carry[2 * n + 1]
                nre = lr[n] * re - li[n] * im + sre[j, r0:r0 + nb, :]
                nim = lr[n] * im + li[n] * re + sim[j, r0:r0 + nb, :]
                if k + 1 < nk:
                    sre[j, r0:r0 + nb, :] = nre
                    sim[j, r0:r0 + nb, :] = nim
                carry[2 * n], carry[2 * n + 1] = nre, nim
        for n, j in enumerate(js):
            st_re[j] = carry[2 * n]
            st_im[j] = carry[2 * n + 1]

    y_ev, y_od = [], []
    for jt in range(S5_LANE_TILES):
        acc = _dot(u_cat[jt], wk_ref[jt])
        for jj in range(4):
            j = 4 * jt + jj
            sp = jnp.concatenate([sre[j, 0:half, :], sim[j, 0:half, :]], axis=-1).astype(BF16)
            acc = acc + _dot_nt(sp, ws_ref[j])
        d = d_ref[:, jt * LANES:(jt + 1) * LANES]
        y_ev.append(jax.nn.gelu(acc[:, :LANES] + d * u_ev[jt]))
        y_od.append(jax.nn.gelu(acc[:, LANES:] + d * u_od[jt]))
    y_ev = jnp.concatenate(y_ev, axis=-1).reshape(nk, nb, S5_WIDTH)
    y_od = jnp.concatenate(y_od, axis=-1).reshape(nk, nb, S5_WIDTH)
    y = jnp.concatenate([y_ev, y_od], axis=1).reshape(tl * nb, S5_WIDTH)
    glu = _dot(y.astype(BF16), wglu_ref[...])
    s5o = glu[:, :S5_WIDTH] * jax.nn.sigmoid(glu[:, S5_WIDTH:])
    for s in range(S5_LANE_TILES):
        y_slab[s] = s5o[:, s * LANES:(s + 1) * LANES]
    for b in range(nb):
        for s in range(S5_LANE_TILES):
            c0 = b * S5_WIDTH + s * LANES
            o_ref[:, c0:c0 + LANES] = y_slab[s, pl.ds(b, tl, stride=nb), :].astype(BF16)

    @pl.when(i == pl.num_programs(0) - 1)
    def _():
        hre_ref[...] = jnp.concatenate([st_re[j] for j in range(S5_PAIRS)], axis=-1)
        him_ref[...] = jnp.concatenate([st_im[j] for j in range(S5_PAIRS)], axis=-1)


def _s5_prompt(x1, mod3, norm_w, w_u, two_step, d_skip, w_glu, tl=128):
    nb, seq, _ = x1.shape
    lam2_t, wz, ws_t, wk = two_step
    rows = nb * tl
    half = rows // 2
    n_state = S5_GROUPS * S5_STATE
    return pl.pallas_call(
        functools.partial(_s5_prompt_kernel, tl=tl, nb=nb),
        out_shape=(jax.ShapeDtypeStruct((seq, nb * S5_WIDTH), BF16),
                   jax.ShapeDtypeStruct((nb, n_state), F32),
                   jax.ShapeDtypeStruct((nb, n_state), F32)),
        grid=(seq // tl,),
        in_specs=[pl.BlockSpec((nb, tl, D_MODEL), lambda i: (0, i, 0)),
                  pl.BlockSpec((nb, 1, 3 * D_MODEL), lambda i: (0, 0, 1)),
                  _resident((1, D_MODEL)),
                  _resident((D_MODEL, S5_WIDTH)),
                  _resident(wz.shape), _resident(lam2_t.shape), _resident(ws_t.shape), _resident(wk.shape),
                  _resident((1, S5_WIDTH)),
                  _resident((S5_WIDTH, 2 * S5_WIDTH))],
        out_specs=(pl.BlockSpec((tl, nb * S5_WIDTH), lambda i: (i, 0)),
                   pl.BlockSpec((nb, n_state), lambda i: (0, 0)),
                   pl.BlockSpec((nb, n_state), lambda i: (0, 0))),
        scratch_shapes=[pltpu.VMEM((S5_LANE_TILES, rows, LANES), F32),
                        pltpu.VMEM((S5_PAIRS, nb + half, LANES), F32),
                        pltpu.VMEM((S5_PAIRS, nb + half, LANES), F32),
                        pltpu.VMEM((S5_PAIRS, nb, LANES), F32),
                        pltpu.VMEM((S5_PAIRS, nb, LANES), F32),
                        pltpu.VMEM((S5_LANE_TILES, rows, LANES), F32)],
        compiler_params=_params(("arbitrary",)),
        name="s5_prompt",
    )(x1, mod3, norm_w.reshape(1, D_MODEL), w_u, wz, lam2_t, ws_t, wk, d_skip.reshape(1, S5_WIDTH), w_glu)


def _rotary_tables(pos):
    half = RET_DK // 2
    inv = ROPE_BASE ** (-np.arange(half, dtype=np.float64) / half)
    ang = np.asarray(pos, np.float64)[:, None] * inv[None, :]
    cos, sin = np.cos(ang), np.sin(ang)
    return (jnp.asarray(np.concatenate([cos, cos], axis=-1), F32),
            jnp.asarray(np.concatenate([-sin, sin], axis=-1), F32))


def _decay_tables(chunk, rows):
    lg = np.log1p(-np.exp2(-5.0 - np.arange(RET_HEADS, dtype=np.float64)))
    idx = np.arange(rows, dtype=np.float64)
    valid = idx < chunk
    diff = idx[:, None] - idx[None, :]
    intra = np.where((diff >= 0) & valid[:, None] & valid[None, :],
                     np.exp(lg[:, None, None] * np.maximum(diff, 0.0)), 0.0)
    q_dec = np.where(valid[None, :], np.exp(lg[:, None] * (idx[None, :] + 1.0)), 0.0)
    k_dec = np.where(valid[None, :], np.exp(lg[:, None] * (chunk - 1.0 - idx)[None, :]), 0.0)
    c_dec = np.exp(lg * chunk)
    q_dec = np.broadcast_to(q_dec[:, :, None], (RET_HEADS, rows, RET_DV))
    k_dec = np.broadcast_to(k_dec[:, :, None], (RET_HEADS, rows, RET_DK))
    c_dec = np.broadcast_to(c_dec[:, None, None], (RET_HEADS, 1, RET_DV))
    return tuple(jnp.asarray(t, F32) for t in (intra, q_dec, k_dec, c_dec))


def _rotary(x, cs, sn):
    return x * cs + pltpu.roll(x, RET_DK // 2, axis=1) * sn


def _head_norm_gate(ret, g):
    mu = jnp.mean(ret, axis=-1, keepdims=True)
    cen = ret - mu
    var = jnp.mean(cen * cen, axis=-1, keepdims=True)
    return (g * jax.nn.sigmoid(g)) * (cen * lax.rsqrt(var + EPS))


def _ret_prompt_kernel(x_ref, mod_ref, nw_ref, wq_ref, cs_ref, sn_ref, intra_ref, qd_ref, kd_ref, cd_ref,
                       s5o_ref, wout_ref, o_ref, sout_ref, s_scr, *, tm):
    t = pl.program_id(1)

    @pl.when(t == 0)
    def _():
        s_scr[...] = jnp.zeros_like(s_scr)

    x = x_ref[0]
    m = mod_ref[0]
    h = _rms_mod(x, nw_ref[...], m[:, :D_MODEL], m[:, D_MODEL:2 * D_MODEL])
    proj = _dot(h.astype(BF16), wq_ref[...])
    cs, sn = cs_ref[...], sn_ref[...]
    n_chunks = tm // RET_CHUNK
    heads = range(RET_HEADS)
    chunks = range(n_chunks)

    lhs, vbs, kvs = {}, {}, {}
    for hd in heads:
        lo = hd * RET_DK
        q = _rotary(proj[:, lo:lo + RET_DK], cs, sn)
        k = _rotary(proj[:, RET_WIDTH + lo:RET_WIDTH + lo + RET_DK], cs, sn) * (RET_DK ** -0.5)
        v = proj[:, 2 * RET_WIDTH + lo:2 * RET_WIDTH + lo + RET_DV]
        for c in chunks:
            rows = slice(c * RET_CHUNK, (c + 1) * RET_CHUNK)
            qc, kc, vb = q[rows], k[rows], v[rows].astype(BF16)
            scores = _dot_nt(qc.astype(BF16), kc.astype(BF16)) * intra_ref[hd]
            lhs[hd, c] = jnp.concatenate([scores.astype(BF16), (qc * qd_ref[hd]).astype(BF16)], axis=-1)
            vbs[hd, c] = vb
            kvs[hd, c] = _dot(jnp.transpose(kc * kd_ref[hd]).astype(BF16), vb)

    states = {}
    for hd in heads:
        s = s_scr[hd]
        for c in chunks:
            states[hd, c] = s
            s = s * cd_ref[hd] + kvs[hd, c]
        s_scr[hd] = s

    parts = [s5o_ref[...]]
    for hd in heads:
        lo = 3 * RET_WIDTH + hd * RET_DV
        outs = [_dot(lhs[hd, c], jnp.concatenate([vbs[hd, c], states[hd, c].astype(BF16)], axis=0))
                for c in chunks]
        parts.append(_head_norm_gate(jnp.concatenate(outs, axis=0), proj[:, lo:lo + RET_DV]).astype(BF16))
    mix = jnp.concatenate(parts, axis=-1)
    o_ref[0] = x + m[:, 2 * D_MODEL:] * _dot(mix, wout_ref[...])

    @pl.when(t == pl.num_programs(1) - 1)
    def _():
        sout_ref[0] = s_scr[...]


def _ret_prompt(x1, mod3, norm_w, w_qkvg, cs, sn, decay, s5o, w_out, tm=1024):
    nb, seq, _ = x1.shape
    intra, q_dec, k_dec, c_dec = decay
    return pl.pallas_call(
        functools.partial(_ret_prompt_kernel, tm=tm),
        out_shape=(jax.ShapeDtypeStruct(x1.shape, F32),
                   jax.ShapeDtypeStruct((nb, RET_HEADS, RET_DK, RET_DV), F32)),
        grid=(nb, seq // tm),
        in_specs=[pl.BlockSpec((1, tm, D_MODEL), lambda b, t: (b, t, 0)),
                  pl.BlockSpec((1, 1, 3 * D_MODEL), lambda b, t: (b, 0, 1)),
                  _resident((1, D_MODEL)),
                  _resident((D_MODEL, 4 * RET_WIDTH)),
                  pl.BlockSpec((tm, RET_DK), lambda b, t: (t, 0)),
                  pl.BlockSpec((tm, RET_DK), lambda b, t: (t, 0)),
                  _resident(intra.shape), _resident(q_dec.shape), _resident(k_dec.shape), _resident(c_dec.shape),
                  pl.BlockSpec((tm, S5_WIDTH), lambda b, t: (t, b)),
                  _resident((D_MODEL, D_MODEL))],
        out_specs=(pl.BlockSpec((1, tm, D_MODEL), lambda b, t: (b, t, 0)),
                   pl.BlockSpec((1, RET_HEADS, RET_DK, RET_DV), lambda b, t: (b, 0, 0, 0))),
        scratch_shapes=[pltpu.VMEM((RET_HEADS, RET_DK, RET_DV), F32)],
        compiler_params=_params(("arbitrary", "arbitrary")),
        name="ret_prompt",
    )(x1, mod3, norm_w.reshape(1, D_MODEL), w_qkvg, cs, sn, intra, q_dec, k_dec, c_dec, s5o, w_out)


def _mix_in_sample_kernel(x_ref, mod_ref, nw_ref, wu_ref, wq_ref, wz_ref, lam2_ref, ws_ref, wk_ref, d_ref,
                          wglu_ref, h0re_ref, h0im_ref, qkvg_ref, s5o_ref, hre_ref, him_ref, *, steps, nb):
    x = x_ref[...]
    m = mod_ref[...]
    h = _rms_mod(x, nw_ref[...], m[..., :D_MODEL], m[..., D_MODEL:2 * D_MODEL])
    hb = h.reshape(steps * nb, D_MODEL).astype(BF16)
    proj = _dot(hb, wq_ref[...])
    qkvg_ref[...] = jnp.zeros_like(qkvg_ref)
    for s in range(4 * RET_HEADS):
        for t in range(steps):
            qkvg_ref[s, pl.ds(t, nb, stride=SUBLANES), :] = proj[t * nb:(t + 1) * nb, s * LANES:(s + 1) * LANES]
    u = _dot(hb, wu_ref[...])
    nk = steps // 2

    def rows_of(parity, lanes):
        return jnp.concatenate([u[(2 * k + parity) * nb:(2 * k + parity + 1) * nb, lanes] for k in range(nk)],
                               axis=0)

    u_ev, u_od, u_cat = [], [], []
    for s in range(S5_LANE_TILES):
        lanes = slice(s * LANES, (s + 1) * LANES)
        u_ev.append(rows_of(0, lanes))
        u_od.append(rows_of(1, lanes))
        u_cat.append(jnp.concatenate([u_ev[s], u_od[s]], axis=-1).astype(BF16))

    prev = []
    for j in range(S5_PAIRS):
        z = _dot(u_cat[j // 4], wz_ref[j])
        lr = lam2_ref[0, j][:1]
        li = lam2_ref[1, j][:1]
        sre = h0re_ref[:, j * LANES:(j + 1) * LANES]
        sim = h0im_ref[:, j * LANES:(j + 1) * LANES]
        pre, pim = [], []
        for k in range(nk):
            pre.append(sre)
            pim.append(sim)
            zre, zim = z[k * nb:(k + 1) * nb, :LANES], z[k * nb:(k + 1) * nb, LANES:]
            sre, sim = lr * sre - li * sim + zre, lr * sim + li * sre + zim
        hre_ref[:, j * LANES:(j + 1) * LANES] = sre
        him_ref[:, j * LANES:(j + 1) * LANES] = sim
        prev.append(jnp.concatenate([jnp.concatenate(pre, axis=0), jnp.concatenate(pim, axis=0)],
                                    axis=-1).astype(BF16))

    y_ev, y_od = [], []
    for jt in range(S5_LANE_TILES):
        acc = _dot(u_cat[jt], wk_ref[jt])
        for jj in range(4):
            acc = acc + _dot_nt(prev[4 * jt + jj], ws_ref[4 * jt + jj])
        d = d_ref[:, jt * LANES:(jt + 1) * LANES]
        y_ev.append(jax.nn.gelu(acc[:, :LANES] + d * u_ev[jt]))
        y_od.append(jax.nn.gelu(acc[:, LANES:] + d * u_od[jt]))
    y_ev = jnp.concatenate(y_ev, axis=-1)
    y_od = jnp.concatenate(y_od, axis=-1)
    y = jnp.concatenate([part[k * nb:(k + 1) * nb] for k in range(nk) for part in (y_ev, y_od)], axis=0)
    glu = _dot(y.astype(BF16), wglu_ref[...])
    s5o_ref[...] = (glu[:, :S5_WIDTH] * jax.nn.sigmoid(glu[:, S5_WIDTH:])).astype(BF16)


def _mix_in_sample(x1, mod3, norm_w, w_u, w_qkvg, two_step, d_skip, w_glu, h0_re, h0_im):
    steps, nb, _ = x1.shape
    rows = steps * nb
    n_state = S5_GROUPS * S5_STATE
    slabs = (4 * RET_HEADS, nb * SUBLANES, LANES)
    lam2_t, wz, ws_t, wk = two_step
    args = (x1, mod3, norm_w.reshape(1, D_MODEL), w_u, w_qkvg, wz, lam2_t, ws_t, wk,
            d_skip.reshape(1, S5_WIDTH), w_glu, h0_re, h0_im)
    in_specs = [pl.BlockSpec((steps, nb, D_MODEL), lambda i: (0, 0, 0)),
                pl.BlockSpec((1, nb, 3 * D_MODEL), lambda i: (0, 0, 1))]
    in_specs += [_resident(a.shape) for a in args[2:]]
    return pl.pallas_call(
        functools.partial(_mix_in_sample_kernel, steps=steps, nb=nb),
        out_shape=(jax.ShapeDtypeStruct(slabs, F32),
                   jax.ShapeDtypeStruct((rows, S5_WIDTH), BF16),
                   jax.ShapeDtypeStruct((nb, n_state), F32),
                   jax.ShapeDtypeStruct((nb, n_state), F32)),
        grid=(1,),
        in_specs=in_specs,
        out_specs=(pl.BlockSpec(slabs, lambda i: (0, 0, 0)),
                   pl.BlockSpec((rows, S5_WIDTH), lambda i: (0, 0)),
                   pl.BlockSpec((nb, n_state), lambda i: (0, 0)),
                   pl.BlockSpec((nb, n_state), lambda i: (0, 0))),
        compiler_params=_params(("arbitrary",)),
        name="mix_in_sample",
    )(*args)


def _ret_sample_kernel(qkvg_ref, s_ref, cs_ref, sn_ref, intra_ref, qd_ref, kd_ref, cd_ref,
                       o_ref, sout_ref, *, bb):
    cs, sn = cs_ref[...], sn_ref[...]
    pairs = [(b, hd) for b in range(bb) for hd in range(RET_HEADS)]

    def tile(kind, b, hd):
        return qkvg_ref[kind * RET_HEADS + hd, b * SUBLANES:(b + 1) * SUBLANES, :]

    scores, cross_lhs, vbs = {}, {}, {}
    for b, hd in pairs:
        q = _rotary(tile(0, b, hd), cs, sn)
        k = _rotary(tile(1, b, hd), cs, sn) * (RET_DK ** -0.5)
        vb = tile(2, b, hd).astype(BF16)
        scores[b, hd] = _dot_nt(q.astype(BF16), k.astype(BF16)) * intra_ref[hd]
        cross_lhs[b, hd] = (q * qd_ref[hd]).astype(BF16)
        vbs[b, hd] = vb
        sout_ref[b, hd] = s_ref[b, hd] * cd_ref[hd] + _dot(jnp.transpose(k * kd_ref[hd]).astype(BF16), vb)
    for b, hd in pairs:
        o = _dot(scores[b, hd].astype(BF16), vbs[b, hd]) + _dot(cross_lhs[b, hd], s_ref[b, hd].astype(BF16))
        o_ref[hd, b * SUBLANES:(b + 1) * SUBLANES, :] = _head_norm_gate(o, tile(3, b, hd))


def _ret_sample(qkvg, s0, cs, sn, decay, bb=16):
    nb = s0.shape[0]
    intra, q_dec, k_dec, c_dec = decay
    st = pl.BlockSpec((bb, RET_HEADS, RET_DK, RET_DV), lambda i: (i, 0, 0, 0))
    return pl.pallas_call(
        functools.partial(_ret_sample_kernel, bb=bb),
        out_shape=(jax.ShapeDtypeStruct((RET_HEADS, nb * SUBLANES, LANES), F32),
                   jax.ShapeDtypeStruct(s0.shape, F32)),
        grid=(nb // bb,),
        in_specs=[pl.BlockSpec((4 * RET_HEADS, bb * SUBLANES, LANES), lambda i: (0, i, 0)),
                  st, _resident(cs.shape), _resident(sn.shape),
                  _resident(intra.shape), _resident(q_dec.shape), _resident(k_dec.shape), _resident(c_dec.shape)],
        out_specs=(pl.BlockSpec((RET_HEADS, bb * SUBLANES, LANES), lambda i: (0, i, 0)), st),
        compiler_params=_params(("arbitrary",)),
        name="ret_sample",
    )(qkvg, s0, cs, sn, intra, q_dec, k_dec, c_dec)


def _mix_out_sample_kernel(x_ref, mod_ref, s5o_ref, ret_ref, wout_ref, o_ref, *, steps, nb):
    x = x_ref[...]
    gate = mod_ref[...][..., 2 * D_MODEL:]
    ret = jnp.concatenate(
        [jnp.concatenate([ret_ref[hd, pl.ds(t, nb, stride=SUBLANES), :] for hd in range(RET_HEADS)], axis=-1)
         for t in range(steps)], axis=0)
    mix = jnp.concatenate([s5o_ref[...], ret.astype(BF16)], axis=-1)
    o_ref[...] = x + gate * _dot(mix, wout_ref[...]).reshape(steps, nb, D_MODEL)


def _mix_out_sample(x1, mod3, s5o, ret, w_out):
    steps, nb, _ = x1.shape
    rows = steps * nb
    return pl.pallas_call(
        functools.partial(_mix_out_sample_kernel, steps=steps, nb=nb),
        out_shape=jax.ShapeDtypeStruct(x1.shape, F32),
        grid=(1,),
        in_specs=[pl.BlockSpec((steps, nb, D_MODEL), lambda i: (0, 0, 0)),
                  pl.BlockSpec((1, nb, 3 * D_MODEL), lambda i: (0, 0, 1)),
                  pl.BlockSpec((rows, S5_WIDTH), lambda i: (0, 0)),
                  pl.BlockSpec(ret.shape, lambda i: (0, 0, 0)),
                  _resident((D_MODEL, D_MODEL))],
        out_specs=pl.BlockSpec((steps, nb, D_MODEL), lambda i: (0, 0, 0)),
        compiler_params=_params(("arbitrary",)),
        name="mix_out_sample",
    )(x1, mod3, s5o, ret, w_out)


def kernel(x_prompt, x_sample, state_ssm_re, state_ssm_im, state_ret, c_prompt, c_sample,
           w_ada, b_ada, norm_ffn1, ffn1_w_in, ffn1_w_out, norm_mix, w_in_mix,
           s5_lambda_re, s5_lambda_im, s5_log_dt, s5_b_re, s5_b_im, s5_c_re, s5_c_im, s5_d, s5_w_glu,
           w_out_mix, norm_ffn2, ffn2_w_in, ffn2_w_out, w_ada_final, b_ada_final, norm_final):
    depth = w_ada.shape[0]
    bp, seq, _ = x_prompt.shape
    bs, steps, _ = x_sample.shape
    assert seq % RET_CHUNK == 0 and steps % 2 == 0 and steps <= SUBLANES

    c_rows = bs + bp

    cs_p, sn_p = _rotary_tables(np.arange(seq))
    decay_p = _decay_tables(RET_CHUNK, RET_CHUNK)
    rows_s = SUBLANES
    pos_s = np.concatenate([PAST_LEN + np.arange(steps), np.zeros(rows_s - steps)])
    cs_s, sn_s = _rotary_tables(pos_s)
    decay_s = _decay_tables(steps, rows_s)

    xp = x_prompt
    xs = jnp.transpose(x_sample, (1, 0, 2))
    outs = {k: [] for k in ("p_re", "p_im", "p_ret", "s_re", "s_im", "s_ret")}
    for l in range(depth):
        mod = _ada(c_sample, c_prompt, w_ada[l], b_ada[l])
        mod_p = mod[bs:c_rows][:, None, :]
        mod_s = mod[None]
        w_u = w_in_mix[l][:, :S5_WIDTH].astype(BF16)
        w_qkvg = w_in_mix[l][:, S5_WIDTH:].astype(BF16)
        w_glu = s5_w_glu[l].astype(BF16)
        w_out = w_out_mix[l].astype(BF16)
        two_step = _s5_tables(s5_lambda_re[l], s5_lambda_im[l], s5_log_dt[l],
                              s5_b_re[l], s5_b_im[l], s5_c_re[l], s5_c_im[l])
        last = l == depth - 1
        if last:
            fin = _ada(c_sample, c_prompt, w_ada_final, b_ada_final)
            fin_p, fin_s = fin[bs:c_rows][:, None, :], fin[None]
        else:
            fin_p = fin_s = None

        xp, xs = _ffn(xp, mod_p, xs, mod_s, 0, norm_ffn1[l], ffn1_w_in[l], ffn1_w_out[l])

        s5o, hre, him = _s5_prompt(xp, mod_p, norm_mix[l], w_u, two_step, s5_d[l], w_glu)
        xp, sret = _ret_prompt(xp, mod_p, norm_mix[l], w_qkvg, cs_p, sn_p, decay_p, s5o, w_out)
        outs["p_re"].append(hre.reshape(bp, S5_GROUPS, S5_STATE))
        outs["p_im"].append(him.reshape(bp, S5_GROUPS, S5_STATE))
        outs["p_ret"].append(sret)

        qkvg, s5o_s, hre_s, him_s = _mix_in_sample(
            xs, mod_s, norm_mix[l], w_u, w_qkvg, two_step, s5_d[l], w_glu,
            state_ssm_re[l].reshape(bs, -1), state_ssm_im[l].reshape(bs, -1))
        ret_s, sret_s = _ret_sample(qkvg, state_ret[l], cs_s, sn_s, decay_s)
        xs = _mix_out_sample(xs, mod_s, s5o_s, ret_s, w_out)

        xp, xs = _ffn(xp, mod_p, xs, mod_s, 2, norm_ffn2[l], ffn2_w_in[l], ffn2_w_out[l],
                      fin_p, fin_s, norm_final if last else None)
        outs["s_re"].append(hre_s.reshape(bs, S5_GROUPS, S5_STATE))
        outs["s_im"].append(him_s.reshape(bs, S5_GROUPS, S5_STATE))
        outs["s_ret"].append(sret_s)

    y_prompt = xp
    y_sample = jnp.transpose(xs, (1, 0, 2))
    return (y_prompt, y_sample, jnp.stack(outs["p_re"]), jnp.stack(outs["p_im"]), jnp.stack(outs["p_ret"]),
            jnp.stack(outs["s_re"]), jnp.stack(outs["s_im"]), jnp.stack(outs["s_ret"]))
```

```python
import functools

import numpy as np
import jax
import jax.numpy as jnp
from jax import lax
from jax.experimental import pallas as pl
from jax.experimental.pallas import tpu as pltpu

F32 = jnp.float32
BF16 = jnp.bfloat16

D_MODEL = 1024
D_FF = 2816
N_MOD = 9
S5_WIDTH = 512
S5_GROUP = 16
S5_GROUPS = 32
S5_STATE = 64
RET_HEADS = 4
RET_DK = 128
RET_DV = 128
RET_WIDTH = 512
RET_CHUNK = 128
PAST_LEN = 16384
ROPE_BASE = 10000.0
EPS = 1e-6

LANES = 128
SUBLANES = 8
S5_PAIRS = S5_GROUPS // 2
S5_LANE_TILES = S5_WIDTH // LANES
VMEM_LIMIT_BYTES = 56 * 1024 * 1024


def _dot(a, b):
    return jnp.dot(a, b, preferred_element_type=F32)


def _dot_nt(a, b):
    return lax.dot_general(a, b, (((1,), (1,)), ((), ())), preferred_element_type=F32)


def _params(semantics):
    return pltpu.CompilerParams(dimension_semantics=semantics, vmem_limit_bytes=VMEM_LIMIT_BYTES)


def _resident(shape):
    nd = len(shape)
    return pl.BlockSpec(shape, lambda *_: (0,) * nd, pipeline_mode=pl.Buffered(1))


def _rms_mod(x, norm_w, shift, scale):
    xn = x * lax.rsqrt(jnp.mean(x * x, axis=-1, keepdims=True) + EPS)
    return xn * (norm_w * (1.0 + scale)) + shift


ADA_COLS = 1024
ADA_STREAMS = 4


def _ada_kernel(cs_ref, cp_ref, *refs):
    w_refs, b_ref, o_ref = refs[:ADA_STREAMS], refs[ADA_STREAMS], refs[ADA_STREAMS + 1]
    pad = o_ref.shape[0] - cs_ref.shape[0] - cp_ref.shape[0]
    c = jnp.concatenate([cs_ref[...], cp_ref[...], jnp.zeros((pad, cs_ref.shape[1]), F32)], axis=0)
    s = (c * jax.nn.sigmoid(c)).astype(BF16)
    band = s.shape[1] // ADA_STREAMS
    acc = b_ref[...]
    for n, w_ref in enumerate(w_refs):
        acc = acc + _dot(s[:, n * band:(n + 1) * band], w_ref[...].astype(BF16))
    o_ref[...] = acc


def _ada(c_s, c_p, w, b):
    k, n = w.shape
    m = -(-(c_s.shape[0] + c_p.shape[0]) // 16) * 16
    band = k // ADA_STREAMS
    w_specs = [pl.BlockSpec((band, ADA_COLS), lambda j, r=r: (r, j)) for r in range(ADA_STREAMS)]
    return pl.pallas_call(
        _ada_kernel,
        out_shape=jax.ShapeDtypeStruct((m, n), F32),
        grid=(n // ADA_COLS,),
        in_specs=[pl.BlockSpec(c_s.shape, lambda j: (0, 0)), pl.BlockSpec(c_p.shape, lambda j: (0, 0))]
        + w_specs + [pl.BlockSpec((1, ADA_COLS), lambda j: (0, j))],
        out_specs=pl.BlockSpec((m, ADA_COLS), lambda j: (0, j)),
        compiler_params=_params(("arbitrary",)),
        name="ada_mod",
    )(c_s, c_p, *([w] * ADA_STREAMS), b.reshape(1, n))


FF_CHUNK = 256
FF_CHUNKS = D_FF // FF_CHUNK
FF_GROUPS = ((0, 6), (6, FF_CHUNKS))
FFN_ROWS = 512


def _ffn_in(x, m, norm_w):
    a_dim, r_dim, _ = x.shape
    h = _rms_mod(x, norm_w, m[..., :D_MODEL], m[..., D_MODEL:2 * D_MODEL])
    return h.reshape(a_dim * r_dim, D_MODEL).astype(BF16)


def _ffn_out(x, m, o, fin, norm_f):
    y = x + (0.5 * m[..., 2 * D_MODEL:]) * o.reshape(x.shape)
    if fin is not None:
        y = _rms_mod(y, norm_f, fin[..., :D_MODEL], fin[..., D_MODEL:])
    return y


def _swiglu_part(hb, wa, wb, wo):
    a = _dot(hb, wa)
    b = _dot(hb, wb)
    return _dot((a * jax.nn.sigmoid(a) * b).astype(BF16), wo)


def _ffn_kernel(*refs, n_prompt, final):
    if final:
        (xp_ref, mp_ref, xs_ref, ms_ref, nw_ref, wa_ref, wb_ref, wo_ref, fp_ref, fs_ref, nf_ref,
         op_ref, os_ref, win_scr, wout_scr, hb_scr, acc_scr) = refs
    else:
        (xp_ref, mp_ref, xs_ref, ms_ref, nw_ref, wa_ref, wb_ref, wo_ref,
         op_ref, os_ref, win_scr, wout_scr, hb_scr, acc_scr) = refs
        fp_ref = fs_ref = nf_ref = None
    i = pl.program_id(0)

    def finish(x_ref, m_ref, f_ref, o, o_ref):
        o_ref[...] = _ffn_out(x_ref[...], m_ref[...], o,
                              f_ref[...] if final else None, nf_ref[...] if final else None)

    @pl.when(i < FF_CHUNKS)
    def _():
        wa, wb, wo = wa_ref[...].astype(BF16), wb_ref[...].astype(BF16), wo_ref[...].astype(BF16)
        win_scr[i] = wa
        win_scr[FF_CHUNKS + i] = wb
        wout_scr[i] = wo

        @pl.when(i == 0)
        def _():
            hb_scr[...] = _ffn_in(xp_ref[...], mp_ref[...], nw_ref[...])
            acc_scr[...] = jnp.zeros_like(acc_scr)

        acc_scr[...] += _swiglu_part(hb_scr[...], wa, wb, wo)

        @pl.when(i == FF_CHUNKS - 1)
        def _():
            finish(xp_ref, mp_ref, fp_ref, acc_scr[...], op_ref)

    def run(x_ref, m_ref, f_ref, o_ref):
        x, m = x_ref[...], m_ref[...]
        a_dim, r_dim, _ = x.shape
        hb = _ffn_in(x, m, nw_ref[...])
        acts = []
        for c0, c1 in FF_GROUPS:
            a = jnp.concatenate([_dot(hb, win_scr[c]) for c in range(c0, c1)], axis=-1)
            b = jnp.concatenate([_dot(hb, win_scr[FF_CHUNKS + c]) for c in range(c0, c1)], axis=-1)
            acts.append((a * jax.nn.sigmoid(a) * b).astype(BF16))
        n_rows = a_dim * r_dim // 2
        for half in range(2):
            if a_dim == 1:
                sel = (slice(None), slice(half * r_dim // 2, (half + 1) * r_dim // 2))
            else:
                sel = (slice(half * a_dim // 2, (half + 1) * a_dim // 2), slice(None))
            rows = slice(half * n_rows, (half + 1) * n_rows)
            o = None
            for (c0, c1), act in zip(FF_GROUPS, acts):
                part = _dot(act[rows], wout_scr[c0:c1].reshape((c1 - c0) * FF_CHUNK, D_MODEL))
                o = part if o is None else o + part
            o_ref[sel] = _ffn_out(x[sel], m, o, f_ref[...] if final else None, nf_ref[...] if final else None)

    @pl.when((i >= FF_CHUNKS) & (i < FF_CHUNKS + n_prompt - 1))
    def _():
        run(xp_ref, mp_ref, fp_ref, op_ref)

    @pl.when(i == FF_CHUNKS + n_prompt - 1)
    def _():
        run(xs_ref, ms_ref, fs_ref, os_ref)


def _ffn(xp, mod_p, xs, mod_s, sub, norm_w, w_in, w_out, fin_p=None, fin_s=None, norm_f=None):
    bp, seq, _ = xp.shape
    steps, bs, _ = xs.shape
    tiles = seq // FFN_ROWS
    n_prompt = bp * tiles
    final = fin_p is not None

    def chunk(i):
        return jnp.minimum(i, FF_CHUNKS - 1)

    def tile(i):
        return jnp.clip(i - (FF_CHUNKS - 1), 0, n_prompt - 1)

    tok_p = pl.BlockSpec((1, FFN_ROWS, D_MODEL), lambda i: (tile(i) // tiles, tile(i) % tiles, 0))
    tok_s = pl.BlockSpec((steps, bs, D_MODEL), lambda i: (0, 0, 0))
    in_specs = [tok_p,
                pl.BlockSpec((1, 1, 3 * D_MODEL), lambda i: (tile(i) // tiles, 0, sub)),
                tok_s,
                pl.BlockSpec((1, bs, 3 * D_MODEL), lambda i: (0, 0, sub)),
                _resident((1, D_MODEL)),
                pl.BlockSpec((D_MODEL, FF_CHUNK), lambda i: (0, chunk(i))),
                pl.BlockSpec((D_MODEL, FF_CHUNK), lambda i: (0, FF_CHUNKS + chunk(i))),
                pl.BlockSpec((FF_CHUNK, D_MODEL), lambda i: (chunk(i), 0))]
    args = [xp, mod_p, xs, mod_s, norm_w.reshape(1, D_MODEL), w_in, w_in, w_out]
    if final:
        in_specs += [pl.BlockSpec((1, 1, 2 * D_MODEL), lambda i: (tile(i) // tiles, 0, 0)),
                     pl.BlockSpec((1, bs, 2 * D_MODEL), lambda i: (0, 0, 0)),
                     _resident((1, D_MODEL))]
        args += [fin_p, fin_s, norm_f.reshape(1, D_MODEL)]
    return pl.pallas_call(
        functools.partial(_ffn_kernel, n_prompt=n_prompt, final=final),
        out_shape=(jax.ShapeDtypeStruct(xp.shape, F32), jax.ShapeDtypeStruct(xs.shape, F32)),
        grid=(FF_CHUNKS + n_prompt,),
        in_specs=in_specs,
        out_specs=(tok_p, tok_s),
        scratch_shapes=[pltpu.VMEM((2 * FF_CHUNKS, D_MODEL, FF_CHUNK), BF16),
                        pltpu.VMEM((FF_CHUNKS, FF_CHUNK, D_MODEL), BF16),
                        pltpu.VMEM((FFN_ROWS, D_MODEL), BF16),
                        pltpu.VMEM((FFN_ROWS, D_MODEL), F32)],
        compiler_params=_params(("arbitrary",)),
        name="ffn_final" if final else "ffn",
    )(*args)


def _s5_tables(lam_re, lam_im, log_dt, b_re, b_im, c_re, c_im):
    lr, li = lam_re.astype(F32), lam_im.astype(F32)
    dt = jnp.exp(log_dt.astype(F32))[:, None]
    mag = jnp.exp(lr * dt)
    ar, ai = mag * jnp.cos(li * dt), mag * jnp.sin(li * dt)
    a2r, a2i = ar * ar - ai * ai, 2.0 * ar * ai
    den = lr * lr + li * li
    cr = ((ar - 1.0) * lr + ai * li) / den
    ci = (ai * lr - (ar - 1.0) * li) / den
    br, bi = b_re.astype(F32), b_im.astype(F32)
    bb_re = cr[:, :, None] * br - ci[:, :, None] * bi
    bb_im = cr[:, :, None] * bi + ci[:, :, None] * br
    lb_re = ar[:, :, None] * bb_re - ai[:, :, None] * bb_im
    lb_im = ar[:, :, None] * bb_im + ai[:, :, None] * bb_re
    cre, cim = c_re.astype(F32), c_im.astype(F32)
    c1_re = cre * ar[:, None, :] - cim * ai[:, None, :]
    c1_im = cre * ai[:, None, :] + cim * ar[:, None, :]
    c2_re = cre * a2r[:, None, :] - cim * a2i[:, None, :]
    c2_im = cre * a2i[:, None, :] + cim * a2r[:, None, :]

    lam2_t = jnp.broadcast_to(jnp.stack([a2r, a2i]).reshape(2, S5_PAIRS, 1, LANES),
                              (2, S5_PAIRS, SUBLANES, LANES))

    groups_per_tile = LANES // S5_GROUP
    q_of = 2 * (np.arange(S5_PAIRS) % 4)[:, None] + np.arange(2)[None, :]
    slot = (np.arange(groups_per_tile)[None, :, None] == q_of[:, None, :])
    slot = np.broadcast_to(slot[:, :, None, :, None], (S5_PAIRS, groups_per_tile, 2, 2, S5_STATE))
    slot = jnp.asarray(slot.reshape(S5_PAIRS, 1, groups_per_tile, 1, 2 * LANES), F32)

    def place(parts):
        v = jnp.stack([jnp.stack(p) for p in parts])
        v = v.reshape(2, 2, S5_PAIRS, 2, S5_GROUP, S5_STATE)
        v = jnp.transpose(v, (2, 0, 4, 1, 3, 5)).reshape(S5_PAIRS, 2, 1, S5_GROUP, 2 * LANES)
        return (v * slot).reshape(S5_PAIRS, 2 * LANES, 2 * LANES).astype(BF16)

    def cn(t):
        return jnp.transpose(t, (0, 2, 1))

    wz = place([[cn(lb_re), cn(lb_im)], [cn(bb_re), cn(bb_im)]])
    ws_t = place([[c1_re, -c1_im], [c2_re, -c2_im]])

    c_cat = jnp.concatenate([cre, -cim], axis=-1)
    x_cat = jnp.stack([jnp.concatenate([bb_re, bb_im], axis=1),
                       jnp.concatenate([lb_re, lb_im], axis=1)])
    k = jnp.einsum('gom,xgmi->xgio', c_cat, x_cat, precision=lax.Precision.HIGHEST)
    k = k.reshape(2, S5_LANE_TILES, groups_per_tile, 1, S5_GROUP, S5_GROUP)
    eye = jnp.asarray(np.eye(groups_per_tile)[None, None, :, :, None, None], F32)
    k = jnp.transpose(eye * k, (0, 1, 2, 4, 3, 5)).reshape(2, S5_LANE_TILES, LANES, LANES)
    k0, k1 = k[0], k[1]
    wk = jnp.concatenate([jnp.concatenate([k0, k1], axis=-1),
                          jnp.concatenate([jnp.zeros_like(k0), k0], axis=-1)], axis=1).astype(BF16)
    return lam2_t, wz, ws_t, wk


PROJ_ROWS = 512

def _s5_prompt_kernel(x_ref, mod_ref, nw_ref, wu_ref, wz_ref, lam2_ref, ws_ref, wk_ref, d_ref, wglu_ref,
                      o_ref, hre_ref, him_ref,
                      u_slab, sre, sim, st_re, st_im, y_slab, *, tl, nb):
    i = pl.program_id(0)
    nk = tl // 2
    half = nk * nb

    @pl.when(i == 0)
    def _():
        st_re[...] = jnp.zeros_like(st_re)
        st_im[...] = jnp.zeros_like(st_im)

    seqs = max(1, PROJ_ROWS // tl)
    parts = []
    for b0 in range(0, nb, seqs):
        m = mod_ref[b0:b0 + seqs]
        h = _rms_mod(x_ref[b0:b0 + seqs], nw_ref[...], m[..., :D_MODEL], m[..., D_MODEL:2 * D_MODEL])
        parts.append(_dot(h.reshape(seqs * tl, D_MODEL).astype(BF16), wu_ref[...]))
    u = jnp.concatenate(parts, axis=0)

    for s in range(S5_LANE_TILES):
        for b in range(nb):
            u_slab[s, pl.ds(b, tl, stride=nb), :] = u[b * tl:(b + 1) * tl, s * LANES:(s + 1) * LANES]

    u_ev, u_od, u_cat = [], [], []
    for s in range(S5_LANE_TILES):
        tiles = u_slab[s].reshape(nk, 2 * nb, LANES)
        ev = tiles[:, :nb, :].reshape(half, LANES)
        od = tiles[:, nb:, :].reshape(half, LANES)
        u_ev.append(ev)
        u_od.append(od)
        u_cat.append(jnp.concatenate([ev, od], axis=-1).astype(BF16))

    for j in range(S5_PAIRS):
        z = _dot(u_cat[j // 4], wz_ref[j])
        sre[j, 0:nb, :] = st_re[j]
        sim[j, 0:nb, :] = st_im[j]
        sre[j, nb:nb + half, :] = z[:, :LANES]
        sim[j, nb:nb + half, :] = z[:, LANES:]

    pairs_per_pass = 8
    for j0 in range(0, S5_PAIRS, pairs_per_pass):
        js = range(j0, j0 + pairs_per_pass)
        lr = [lam2_ref[0, j] for j in js]
        li = [lam2_ref[1, j] for j in js]
        carry = []
        for j in js:
            carry += [st_re[j], st_im[j]]
        for k in range(nk):
            r0 = (k + 1) * nb
            for n, j in enumerate(js):
                re, im = carry[2 * n], carry[2 * n + 1]
                nre = lr[n] * re - li[n] * im + sre[j, r0:r0 + nb, :]
                nim = lr[n] * im + li[n] * re + sim[j, r0:r0 + nb, :]
                if k + 1 < nk:
                    sre[j, r0:r0 + nb, :] = nre
                    sim[j, r0:r0 + nb, :] = nim
                carry[2 * n], carry[2 * n + 1] = nre, nim
        for n, j in enumerate(js):
            st_re[j] = carry[2 * n]
            st_im[j] = carry[2 * n + 1]

    y_ev, y_od = [], []
    for jt in range(S5_LANE_TILES):
        acc = _dot(u_cat[jt], wk_ref[jt])
        for jj in range(4):
            j = 4 * jt + jj
            sp = jnp.concatenate([sre[j, 0:half, :], sim[j, 0:half, :]], axis=-1).astype(BF16)
            acc = acc + _dot_nt(sp, ws_ref[j])
        d = d_ref[:, jt * LANES:(jt + 1) * LANES]
        y_ev.append(jax.nn.gelu(acc[:, :LANES] + d * u_ev[jt]))
        y_od.append(jax.nn.gelu(acc[:, LANES:] + d * u_od[jt]))
    y_ev = jnp.concatenate(y_ev, axis=-1).reshape(nk, nb, S5_WIDTH)
    y_od = jnp.concatenate(y_od, axis=-1).reshape(nk, nb, S5_WIDTH)
    y = jnp.concatenate([y_ev, y_od], axis=1).reshape(tl * nb, S5_WIDTH)
    glu = _dot(y.astype(BF16), wglu_ref[...])
    s5o = glu[:, :S5_WIDTH] * jax.nn.sigmoid(glu[:, S5_WIDTH:])
    for s in range(S5_LANE_TILES):
        y_slab[s] = s5o[:, s * LANES:(s + 1) * LANES]
    for b in range(nb):
        for s in range(S5_LANE_TILES):
            c0 = b * S5_WIDTH + s * LANES
            o_ref[:, c0:c0 + LANES] = y_slab[s, pl.ds(b, tl, stride=nb), :].astype(BF16)

    @pl.when(i == pl.num_programs(0) - 1)
    def _():
        hre_ref[...] = jnp.concatenate([st_re[j] for j in range(S5_PAIRS)], axis=-1)
        him_ref[...] = jnp.concatenate([st_im[j] for j in range(S5_PAIRS)], axis=-1)


def _s5_prompt(x1, mod3, norm_w, w_u, two_step, d_skip, w_glu, tl=256):
    nb, seq, _ = x1.shape
    lam2_t, wz, ws_t, wk = two_step
    rows = nb * tl
    half = rows // 2
    n_state = S5_GROUPS * S5_STATE
    return pl.pallas_call(
        functools.partial(_s5_prompt_kernel, tl=tl, nb=nb),
        out_shape=(jax.ShapeDtypeStruct((seq, nb * S5_WIDTH), BF16),
                   jax.ShapeDtypeStruct((nb, n_state), F32),
                   jax.ShapeDtypeStruct((nb, n_state), F32)),
        grid=(seq // tl,),
        in_specs=[pl.BlockSpec((nb, tl, D_MODEL), lambda i: (0, i, 0)),
                  pl.BlockSpec((nb, 1, 3 * D_MODEL), lambda i: (0, 0, 1)),
                  _resident((1, D_MODEL)),
                  _resident((D_MODEL, S5_WIDTH)),
                  _resident(wz.shape), _resident(lam2_t.shape), _resident(ws_t.shape), _resident(wk.shape),
                  _resident((1, S5_WIDTH)),
                  _resident((S5_WIDTH, 2 * S5_WIDTH))],
        out_specs=(pl.BlockSpec((tl, nb * S5_WIDTH), lambda i: (i, 0)),
                   pl.BlockSpec((nb, n_state), lambda i: (0, 0)),
                   pl.BlockSpec((nb, n_state), lambda i: (0, 0))),
        scratch_shapes=[pltpu.VMEM((S5_LANE_TILES, rows, LANES), F32),
                        pltpu.VMEM((S5_PAIRS, nb + half, LANES), F32),
                        pltpu.VMEM((S5_PAIRS, nb + half, LANES), F32),
                        pltpu.VMEM((S5_PAIRS, nb, LANES), F32),
                        pltpu.VMEM((S5_PAIRS, nb, LANES), F32),
                        pltpu.VMEM((S5_LANE_TILES, rows, LANES), F32)],
        compiler_params=_params(("arbitrary",)),
        name="s5_prompt",
    )(x1, mod3, norm_w.reshape(1, D_MODEL), w_u, wz, lam2_t, ws_t, wk, d_skip.reshape(1, S5_WIDTH), w_glu)


def _rotary_tables(pos):
    half = RET_DK // 2
    inv = ROPE_BASE ** (-np.arange(half, dtype=np.float64) / half)
    ang = np.asarray(pos, np.float64)[:, None] * inv[None, :]
    cos, sin = np.cos(ang), np.sin(ang)
    return (jnp.asarray(np.concatenate([cos, cos], axis=-1), F32),
            jnp.asarray(np.concatenate([-sin, sin], axis=-1), F32))


def _decay_tables(chunk, rows):
    lg = np.log1p(-np.exp2(-5.0 - np.arange(RET_HEADS, dtype=np.float64)))
    idx = np.arange(rows, dtype=np.float64)
    valid = idx < chunk
    diff = idx[:, None] - idx[None, :]
    intra = np.where((diff >= 0) & valid[:, None] & valid[None, :],
                     np.exp(lg[:, None, None] * np.maximum(diff, 0.0)), 0.0)
    q_dec = np.where(valid[None, :], np.exp(lg[:, None] * (idx[None, :] + 1.0)), 0.0)
    k_dec = np.where(valid[None, :], np.exp(lg[:, None] * (chunk - 1.0 - idx)[None, :]), 0.0)
    c_dec = np.exp(lg * chunk)
    q_dec = np.broadcast_to(q_dec[:, :, None], (RET_HEADS, rows, RET_DV))
    k_dec = np.broadcast_to(k_dec[:, :, None], (RET_HEADS, rows, RET_DK))
    c_dec = np.broadcast_to(c_dec[:, None, None], (RET_HEADS, 1, RET_DV))
    return tuple(jnp.asarray(t, F32) for t in (intra, q_dec, k_dec, c_dec))


def _rotary(x, cs, sn):
    return x * cs + pltpu.roll(x, RET_DK // 2, axis=1) * sn


def _head_norm_gate(ret, g):
    mu = jnp.mean(ret, axis=-1, keepdims=True)
    cen = ret - mu
    var = jnp.mean(cen * cen, axis=-1, keepdims=True)
    return (g * jax.nn.sigmoid(g)) * (cen * lax.rsqrt(var + EPS))


def _ret_prompt_kernel(x_ref, mod_ref, nw_ref, wq_ref, cs_ref, sn_ref, intra_ref, qd_ref, kd_ref, cd_ref,
                       s5o_ref, wout_ref, o_ref, sout_ref, s_scr, *, tm):
    t = pl.program_id(1)

    @pl.when(t == 0)
    def _():
        s_scr[...] = jnp.zeros_like(s_scr)

    x = x_ref[0]
    m = mod_ref[0]
    proj = jnp.concatenate(
        [_dot(_rms_mod(x[r0:r0 + PROJ_ROWS], nw_ref[...], m[:, :D_MODEL], m[:, D_MODEL:2 * D_MODEL]).astype(BF16),
              wq_ref[...]) for r0 in range(0, tm, PROJ_ROWS)], axis=0)
    cs, sn = cs_ref[...], sn_ref[...]
    n_chunks = tm // RET_CHUNK
    heads = range(RET_HEADS)
    chunks = range(n_chunks)

    lhs, vbs, kvs = {}, {}, {}
    for hd in heads:
        lo = hd * RET_DK
        q = _rotary(proj[:, lo:lo + RET_DK], cs, sn)
        k = _rotary(proj[:, RET_WIDTH + lo:RET_WIDTH + lo + RET_DK], cs, sn) * (RET_DK ** -0.5)
        v = proj[:, 2 * RET_WIDTH + lo:2 * RET_WIDTH + lo + RET_DV]
        for c in chunks:
            rows = slice(c * RET_CHUNK, (c + 1) * RET_CHUNK)
            qc, kc, vb = q[rows], k[rows], v[rows].astype(BF16)
            scores = _dot_nt(qc.astype(BF16), kc.astype(BF16)) * intra_ref[hd]
            lhs[hd, c] = jnp.concatenate([scores.astype(BF16), (qc * qd_ref[hd]).astype(BF16)], axis=-1)
            vbs[hd, c] = vb
            kvs[hd, c] = _dot(jnp.transpose(kc * kd_ref[hd]).astype(BF16), vb)

    states = {}
    for hd in heads:
        s = s_scr[hd]
        for c in chunks:
            states[hd, c] = s
            s = s * cd_ref[hd] + kvs[hd, c]
        s_scr[hd] = s

    parts = [s5o_ref[...]]
    for hd in heads:
        lo = 3 * RET_WIDTH + hd * RET_DV
        outs = [_dot(lhs[hd, c], jnp.concatenate([vbs[hd, c], states[hd, c].astype(BF16)], axis=0))
                for c in chunks]
        parts.append(_head_norm_gate(jnp.concatenate(outs, axis=0), proj[:, lo:lo + RET_DV]).astype(BF16))
    mix = jnp.concatenate(parts, axis=-1)
    for r0 in range(0, tm, PROJ_ROWS):
        rows = slice(r0, r0 + PROJ_ROWS)
        o_ref[0, rows, :] = x[rows] + m[:, 2 * D_MODEL:] * _dot(mix[rows], wout_ref[...])

    @pl.when(t == pl.num_programs(1) - 1)
    def _():
        sout_ref[0] = s_scr[...]


def _ret_prompt(x1, mod3, norm_w, w_qkvg, cs, sn, decay, s5o, w_out, tm=1024):
    nb, seq, _ = x1.shape
    intra, q_dec, k_dec, c_dec = decay
    return pl.pallas_call(
        functools.partial(_ret_prompt_kernel, tm=tm),
        out_shape=(jax.ShapeDtypeStruct(x1.shape, F32),
                   jax.ShapeDtypeStruct((nb, RET_HEADS, RET_DK, RET_DV), F32)),
        grid=(nb, seq // tm),
        in_specs=[pl.BlockSpec((1, tm, D_MODEL), lambda b, t: (b, t, 0)),
                  pl.BlockSpec((1, 1, 3 * D_MODEL), lambda b, t: (b, 0, 1)),
                  _resident((1, D_MODEL)),
                  _resident((D_MODEL, 4 * RET_WIDTH)),
                  pl.BlockSpec((tm, RET_DK), lambda b, t: (t, 0)),
                  pl.BlockSpec((tm, RET_DK), lambda b, t: (t, 0)),
                  _resident(intra.shape), _resident(q_dec.shape), _resident(k_dec.shape), _resident(c_dec.shape),
                  pl.BlockSpec((tm, S5_WIDTH), lambda b, t: (t, b)),
                  _resident((D_MODEL, D_MODEL))],
        out_specs=(pl.BlockSpec((1, tm, D_MODEL), lambda b, t: (b, t, 0)),
                   pl.BlockSpec((1, RET_HEADS, RET_DK, RET_DV), lambda b, t: (b, 0, 0, 0))),
        scratch_shapes=[pltpu.VMEM((RET_HEADS, RET_DK, RET_DV), F32)],
        compiler_params=_params(("arbitrary", "arbitrary")),
        name="ret_prompt",
    )(x1, mod3, norm_w.reshape(1, D_MODEL), w_qkvg, cs, sn, intra, q_dec, k_dec, c_dec, s5o, w_out)


def _mix_in_sample_kernel(x_ref, mod_ref, nw_ref, wu_ref, wq_ref, wz_ref, lam2_ref, ws_ref, wk_ref, d_ref,
                          wglu_ref, h0re_ref, h0im_ref, qkvg_ref, s5o_ref, hre_ref, him_ref, *, steps, nb):
    x = x_ref[...]
    m = mod_ref[...]
    h = _rms_mod(x, nw_ref[...], m[..., :D_MODEL], m[..., D_MODEL:2 * D_MODEL])
    hb = h.reshape(steps * nb, D_MODEL).astype(BF16)
    proj = _dot(hb, wq_ref[...])
    qkvg_ref[...] = jnp.zeros_like(qkvg_ref)
    for s in range(4 * RET_HEADS):
        for t in range(steps):
            qkvg_ref[s, pl.ds(t, nb, stride=SUBLANES), :] = proj[t * nb:(t + 1) * nb, s * LANES:(s + 1) * LANES]
    u = _dot(hb, wu_ref[...])
    nk = steps // 2

    def rows_of(parity, lanes):
        return jnp.concatenate([u[(2 * k + parity) * nb:(2 * k + parity + 1) * nb, lanes] for k in range(nk)],
                               axis=0)

    u_ev, u_od, u_cat = [], [], []
    for s in range(S5_LANE_TILES):
        lanes = slice(s * LANES, (s + 1) * LANES)
        u_ev.append(rows_of(0, lanes))
        u_od.append(rows_of(1, lanes))
        u_cat.append(jnp.concatenate([u_ev[s], u_od[s]], axis=-1).astype(BF16))

    prev = []
    for j in range(S5_PAIRS):
        z = _dot(u_cat[j // 4], wz_ref[j])
        lr = lam2_ref[0, j][:1]
        li = lam2_ref[1, j][:1]
        sre = h0re_ref[:, j * LANES:(j + 1) * LANES]
        sim = h0im_ref[:, j * LANES:(j + 1) * LANES]
        pre, pim = [], []
        for k in range(nk):
            pre.append(sre)
            pim.append(sim)
            zre, zim = z[k * nb:(k + 1) * nb, :LANES], z[k * nb:(k + 1) * nb, LANES:]
            sre, sim = lr * sre - li * sim + zre, lr * sim + li * sre + zim
        hre_ref[:, j * LANES:(j + 1) * LANES] = sre
        him_ref[:, j * LANES:(j + 1) * LANES] = sim
        prev.append(jnp.concatenate([jnp.concatenate(pre, axis=0), jnp.concatenate(pim, axis=0)],
                                    axis=-1).astype(BF16))

    y_ev, y_od = [], []
    for jt in range(S5_LANE_TILES):
        acc = _dot(u_cat[jt], wk_ref[jt])
        for jj in range(4):
            acc = acc + _dot_nt(prev[4 * jt + jj], ws_ref[4 * jt + jj])
        d = d_ref[:, jt * LANES:(jt + 1) * LANES]
        y_ev.append(jax.nn.gelu(acc[:, :LANES] + d * u_ev[jt]))
        y_od.append(jax.nn.gelu(acc[:, LANES:] + d * u_od[jt]))
    y_ev = jnp.concatenate(y_ev, axis=-1)
    y_od = jnp.concatenate(y_od, axis=-1)
    y = jnp.concatenate([part[k * nb:(k + 1) * nb] for k in range(nk) for part in (y_ev, y_od)], axis=0)
    glu = _dot(y.astype(BF16), wglu_ref[...])
    s5o_ref[...] = (glu[:, :S5_WIDTH] * jax.nn.sigmoid(glu[:, S5_WIDTH:])).astype(BF16)


def _mix_in_sample(x1, mod3, norm_w, w_u, w_qkvg, two_step, d_skip, w_glu, h0_re, h0_im):
    steps, nb, _ = x1.shape
    rows = steps * nb
    n_state = S5_GROUPS * S5_STATE
    slabs = (4 * RET_HEADS, nb * SUBLANES, LANES)
    lam2_t, wz, ws_t, wk = two_step
    args = (x1, mod3, norm_w.reshape(1, D_MODEL), w_u, w_qkvg, wz, lam2_t, ws_t, wk,
            d_skip.reshape(1, S5_WIDTH), w_glu, h0_re, h0_im)
    in_specs = [pl.BlockSpec((steps, nb, D_MODEL), lambda i: (0, 0, 0)),
                pl.BlockSpec((1, nb, 3 * D_MODEL), lambda i: (0, 0, 1))]
    in_specs += [_resident(a.shape) for a in args[2:]]
    return pl.pallas_call(
        functools.partial(_mix_in_sample_kernel, steps=steps, nb=nb),
        out_shape=(jax.ShapeDtypeStruct(slabs, F32),
                   jax.ShapeDtypeStruct((rows, S5_WIDTH), BF16),
                   jax.ShapeDtypeStruct((nb, n_state), F32),
                   jax.ShapeDtypeStruct((nb, n_state), F32)),
        grid=(1,),
        in_specs=in_specs,
        out_specs=(pl.BlockSpec(slabs, lambda i: (0, 0, 0)),
                   pl.BlockSpec((rows, S5_WIDTH), lambda i: (0, 0)),
                   pl.BlockSpec((nb, n_state), lambda i: (0, 0)),
                   pl.BlockSpec((nb, n_state), lambda i: (0, 0))),
        compiler_params=_params(("arbitrary",)),
        name="mix_in_sample",
    )(*args)


def _ret_sample_kernel(qkvg_ref, s_ref, cs_ref, sn_ref, intra_ref, qd_ref, kd_ref, cd_ref,
                       o_ref, sout_ref, *, bb):
    cs, sn = cs_ref[...], sn_ref[...]
    pairs = [(b, hd) for b in range(bb) for hd in range(RET_HEADS)]

    def tile(kind, b, hd):
        return qkvg_ref[kind * RET_HEADS + hd, b * SUBLANES:(b + 1) * SUBLANES, :]

    scores, cross_lhs, vbs = {}, {}, {}
    for b, hd in pairs:
        q = _rotary(tile(0, b, hd), cs, sn)
        k = _rotary(tile(1, b, hd), cs, sn) * (RET_DK ** -0.5)
        vb = tile(2, b, hd).astype(BF16)
        scores[b, hd] = _dot_nt(q.astype(BF16), k.astype(BF16)) * intra_ref[hd]
        cross_lhs[b, hd] = (q * qd_ref[hd]).astype(BF16)
        vbs[b, hd] = vb
        sout_ref[b, hd] = s_ref[b, hd] * cd_ref[hd] + _dot(jnp.transpose(k * kd_ref[hd]).astype(BF16), vb)
    for b, hd in pairs:
        o = _dot(scores[b, hd].astype(BF16), vbs[b, hd]) + _dot(cross_lhs[b, hd], s_ref[b, hd].astype(BF16))
        o_ref[hd, b * SUBLANES:(b + 1) * SUBLANES, :] = _head_norm_gate(o, tile(3, b, hd))


def _ret_sample(qkvg, s0, cs, sn, decay, bb=16):
    nb = s0.shape[0]
    intra, q_dec, k_dec, c_dec = decay
    st = pl.BlockSpec((bb, RET_HEADS, RET_DK, RET_DV), lambda i: (i, 0, 0, 0))
    return pl.pallas_call(
        functools.partial(_ret_sample_kernel, bb=bb),
        out_shape=(jax.ShapeDtypeStruct((RET_HEADS, nb * SUBLANES, LANES), F32),
                   jax.ShapeDtypeStruct(s0.shape, F32)),
        grid=(nb // bb,),
        in_specs=[pl.BlockSpec((4 * RET_HEADS, bb * SUBLANES, LANES), lambda i: (0, i, 0)),
                  st, _resident(cs.shape), _resident(sn.shape),
                  _resident(intra.shape), _resident(q_dec.shape), _resident(k_dec.shape), _resident(c_dec.shape)],
        out_specs=(pl.BlockSpec((RET_HEADS, bb * SUBLANES, LANES), lambda i: (0, i, 0)), st),
        compiler_params=_params(("arbitrary",)),
        name="ret_sample",
    )(qkvg, s0, cs, sn, intra, q_dec, k_dec, c_dec)


def _mix_out_sample_kernel(x_ref, mod_ref, s5o_ref, ret_ref, wout_ref, o_ref, *, steps, nb):
    x = x_ref[...]
    gate = mod_ref[...][..., 2 * D_MODEL:]
    ret = jnp.concatenate(
        [jnp.concatenate([ret_ref[hd, pl.ds(t, nb, stride=SUBLANES), :] for hd in range(RET_HEADS)], axis=-1)
         for t in range(steps)], axis=0)
    mix = jnp.concatenate([s5o_ref[...], ret.astype(BF16)], axis=-1)
    o_ref[...] = x + gate * _dot(mix, wout_ref[...]).reshape(steps, nb, D_MODEL)


def _mix_out_sample(x1, mod3, s5o, ret, w_out):
    steps, nb, _ = x1.shape
    rows = steps * nb
    return pl.pallas_call(
        functools.partial(_mix_out_sample_kernel, steps=steps, nb=nb),
        out_shape=jax.ShapeDtypeStruct(x1.shape, F32),
        grid=(1,),
        in_specs=[pl.BlockSpec((steps, nb, D_MODEL), lambda i: (0, 0, 0)),
                  pl.BlockSpec((1, nb, 3 * D_MODEL), lambda i: (0, 0, 1)),
                  pl.BlockSpec((rows, S5_WIDTH), lambda i: (0, 0)),
                  pl.BlockSpec(ret.shape, lambda i: (0, 0, 0)),
                  _resident((D_MODEL, D_MODEL))],
        out_specs=pl.BlockSpec((steps, nb, D_MODEL), lambda i: (0, 0, 0)),
        compiler_params=_params(("arbitrary",)),
        name="mix_out_sample",
    )(x1, mod3, s5o, ret, w_out)


def kernel(x_prompt, x_sample, state_ssm_re, state_ssm_im, state_ret, c_prompt, c_sample,
           w_ada, b_ada, norm_ffn1, ffn1_w_in, ffn1_w_out, norm_mix, w_in_mix,
           s5_lambda_re, s5_lambda_im, s5_log_dt, s5_b_re, s5_b_im, s5_c_re, s5_c_im, s5_d, s5_w_glu,
           w_out_mix, norm_ffn2, ffn2_w_in, ffn2_w_out, w_ada_final, b_ada_final, norm_final):
    depth = w_ada.shape[0]
    bp, seq, _ = x_prompt.shape
    bs, steps, _ = x_sample.shape
    assert seq % RET_CHUNK == 0 and steps % 2 == 0 and steps <= SUBLANES

    c_rows = bs + bp

    cs_p, sn_p = _rotary_tables(np.arange(seq))
    decay_p = _decay_tables(RET_CHUNK, RET_CHUNK)
    rows_s = SUBLANES
    pos_s = np.concatenate([PAST_LEN + np.arange(steps), np.zeros(rows_s - steps)])
    cs_s, sn_s = _rotary_tables(pos_s)
    decay_s = _decay_tables(steps, rows_s)

    xp = x_prompt
    xs = jnp.transpose(x_sample, (1, 0, 2))
    outs = {k: [] for k in ("p_re", "p_im", "p_ret", "s_re", "s_im", "s_ret")}
    for l in range(depth):
        mod = _ada(c_sample, c_prompt, w_ada[l], b_ada[l])
        mod_p = mod[bs:c_rows][:, None, :]
        mod_s = mod[None]
        w_u = w_in_mix[l][:, :S5_WIDTH].astype(BF16)
        w_qkvg = w_in_mix[l][:, S5_WIDTH:].astype(BF16)
        w_glu = s5_w_glu[l].astype(BF16)
        w_out = w_out_mix[l].astype(BF16)
        two_step = _s5_tables(s5_lambda_re[l], s5_lambda_im[l], s5_log_dt[l],
                              s5_b_re[l], s5_b_im[l], s5_c_re[l], s5_c_im[l])
        last = l == depth - 1
        if last:
            fin = _ada(c_sample, c_prompt, w_ada_final, b_ada_final)
            fin_p, fin_s = fin[bs:c_rows][:, None, :], fin[None]
        else:
            fin_p = fin_s = None

        xp, xs = _ffn(xp, mod_p, xs, mod_s, 0, norm_ffn1[l], ffn1_w_in[l], ffn1_w_out[l])

        s5o, hre, him = _s5_prompt(xp, mod_p, norm_mix[l], w_u, two_step, s5_d[l], w_glu)
        xp, sret = _ret_prompt(xp, mod_p, norm_mix[l], w_qkvg, cs_p, sn_p, decay_p, s5o, w_out)
        outs["p_re"].append(hre.reshape(bp, S5_GROUPS, S5_STATE))
        outs["p_im"].append(him.reshape(bp, S5_GROUPS, S5_STATE))
        outs["p_ret"].append(sret)

        qkvg, s5o_s, hre_s, him_s = _mix_in_sample(
            xs, mod_s, norm_mix[l], w_u, w_qkvg, two_step, s5_d[l], w_glu,
            state_ssm_re[l].reshape(bs, -1), state_ssm_im[l].reshape(bs, -1))
        ret_s, sret_s = _ret_sample(qkvg, state_ret[l], cs_s, sn_s, decay_s)
        xs = _mix_out_sample(xs, mod_s, s5o_s, ret_s, w_out)

        xp, xs = _ffn(xp, mod_p, xs, mod_s, 2, norm_ffn2[l], ffn2_w_in[l], ffn2_w_out[l],
                      fin_p, fin_s, norm_final if last else None)
        outs["s_re"].append(hre_s.reshape(bs, S5_GROUPS, S5_STATE))
        outs["s_im"].append(him_s.reshape(bs, S5_GROUPS, S5_STATE))
        outs["s_ret"].append(sret_s)

    y_prompt = xp
    y_sample = jnp.transpose(xs, (1, 0, 2))
    return (y_prompt, y_sample, jnp.stack(outs["p_re"]), jnp.stack(outs["p_im"]), jnp.stack(outs["p_ret"]),
            jnp.stack(outs["s_re"]), jnp.stack(outs["s_im"]), jnp.stack(outs["s_ret"]))
```

```python
import functools

import numpy as np
import jax
import jax.numpy as jnp
from jax import lax
from jax.experimental import pallas as pl
from jax.experimental.pallas import tpu as pltpu

F32 = jnp.float32
BF16 = jnp.bfloat16

D_MODEL = 1024
D_FF = 2816
N_MOD = 9
S5_WIDTH = 512
S5_GROUP = 16
S5_GROUPS = 32
S5_STATE = 64
RET_HEADS = 4
RET_DK = 128
RET_DV = 128
RET_WIDTH = 512
RET_CHUNK = 128
PAST_LEN = 16384
ROPE_BASE = 10000.0
EPS = 1e-6

LANES = 128
SUBLANES = 8
S5_PAIRS = S5_GROUPS // 2
S5_LANE_TILES = S5_WIDTH // LANES
VMEM_LIMIT_BYTES = 56 * 1024 * 1024


def _dot(a, b):
    return jnp.dot(a, b, preferred_element_type=F32)


def _dot_nt(a, b):
    return lax.dot_general(a, b, (((1,), (1,)), ((), ())), preferred_element_type=F32)


def _params(semantics):
    return pltpu.CompilerParams(dimension_semantics=semantics, vmem_limit_bytes=VMEM_LIMIT_BYTES)


def _resident(shape):
    nd = len(shape)
    return pl.BlockSpec(shape, lambda *_: (0,) * nd, pipeline_mode=pl.Buffered(1))


def _rms_mod(x, norm_w, shift, scale):
    xn = x * lax.rsqrt(jnp.mean(x * x, axis=-1, keepdims=True) + EPS)
    return xn * (norm_w * (1.0 + scale)) + shift


ADA_COLS = 1024
ADA_STREAMS = 4


def _ada_kernel(cs_ref, cp_ref, *refs):
    w_refs, b_ref, o_ref = refs[:ADA_STREAMS], refs[ADA_STREAMS], refs[ADA_STREAMS + 1]
    pad = o_ref.shape[0] - cs_ref.shape[0] - cp_ref.shape[0]
    c = jnp.concatenate([cs_ref[...], cp_ref[...], jnp.zeros((pad, cs_ref.shape[1]), F32)], axis=0)
    s = (c * jax.nn.sigmoid(c)).astype(BF16)
    band = s.shape[1] // ADA_STREAMS
    acc = b_ref[...]
    for n, w_ref in enumerate(w_refs):
        acc = acc + _dot(s[:, n * band:(n + 1) * band], w_ref[...].astype(BF16))
    o_ref[...] = acc


def _ada(c_s, c_p, w, b):
    k, n = w.shape
    m = -(-(c_s.shape[0] + c_p.shape[0]) // 16) * 16
    band = k // ADA_STREAMS
    w_specs = [pl.BlockSpec((band, ADA_COLS), lambda j, r=r: (r, j)) for r in range(ADA_STREAMS)]
    return pl.pallas_call(
        _ada_kernel,
        out_shape=jax.ShapeDtypeStruct((m, n), F32),
        grid=(n // ADA_COLS,),
        in_specs=[pl.BlockSpec(c_s.shape, lambda j: (0, 0)), pl.BlockSpec(c_p.shape, lambda j: (0, 0))]
        + w_specs + [pl.BlockSpec((1, ADA_COLS), lambda j: (0, j))],
        out_specs=pl.BlockSpec((m, ADA_COLS), lambda j: (0, j)),
        compiler_params=_params(("arbitrary",)),
        name="ada_mod",
    )(c_s, c_p, *([w] * ADA_STREAMS), b.reshape(1, n))


FF_CHUNK = 256
FF_CHUNKS = D_FF // FF_CHUNK
FF_GROUPS = ((0, 6), (6, FF_CHUNKS))
FFN_ROWS = 512


def _ffn_in(x, m, norm_w):
    a_dim, r_dim, _ = x.shape
    h = _rms_mod(x, norm_w, m[..., :D_MODEL], m[..., D_MODEL:2 * D_MODEL])
    return h.reshape(a_dim * r_dim, D_MODEL).astype(BF16)


def _ffn_out(x, m, o, fin, norm_f):
    y = x + (0.5 * m[..., 2 * D_MODEL:]) * o.reshape(x.shape)
    if fin is not None:
        y = _rms_mod(y, norm_f, fin[..., :D_MODEL], fin[..., D_MODEL:])
    return y


def _swiglu_part(hb, wa, wb, wo):
    a = _dot(hb, wa)
    b = _dot(hb, wb)
    return _dot((a * jax.nn.sigmoid(a) * b).astype(BF16), wo)


def _ffn_kernel(*refs, n_prompt, final):
    if final:
        (xp_ref, mp_ref, xs_ref, ms_ref, nw_ref, wa_ref, wb_ref, wo_ref, fp_ref, fs_ref, nf_ref,
         op_ref, os_ref, win_scr, wout_scr, hb_scr, acc_scr) = refs
    else:
        (xp_ref, mp_ref, xs_ref, ms_ref, nw_ref, wa_ref, wb_ref, wo_ref,
         op_ref, os_ref, win_scr, wout_scr, hb_scr, acc_scr) = refs
        fp_ref = fs_ref = nf_ref = None
    i = pl.program_id(0)

    def finish(x_ref, m_ref, f_ref, o, o_ref):
        o_ref[...] = _ffn_out(x_ref[...], m_ref[...], o,
                              f_ref[...] if final else None, nf_ref[...] if final else None)

    @pl.when(i < FF_CHUNKS)
    def _():
        wa, wb, wo = wa_ref[...].astype(BF16), wb_ref[...].astype(BF16), wo_ref[...].astype(BF16)
        win_scr[i] = wa
        win_scr[FF_CHUNKS + i] = wb
        wout_scr[i] = wo

        @pl.when(i == 0)
        def _():
            hb_scr[...] = _ffn_in(xp_ref[...], mp_ref[...], nw_ref[...])
            acc_scr[...] = jnp.zeros_like(acc_scr)

        acc_scr[...] += _swiglu_part(hb_scr[...], wa, wb, wo)

        @pl.when(i == FF_CHUNKS - 1)
        def _():
            finish(xp_ref, mp_ref, fp_ref, acc_scr[...], op_ref)

    def run(x_ref, m_ref, f_ref, o_ref):
        hb = _ffn_in(x_ref[...], m_ref[...], nw_ref[...])
        o = None
        for c0, c1 in FF_GROUPS:
            a = jnp.concatenate([_dot(hb, win_scr[c]) for c in range(c0, c1)], axis=-1)
            b = jnp.concatenate([_dot(hb, win_scr[FF_CHUNKS + c]) for c in range(c0, c1)], axis=-1)
            act = (a * jax.nn.sigmoid(a) * b).astype(BF16)
            part = _dot(act, wout_scr[c0:c1].reshape((c1 - c0) * FF_CHUNK, D_MODEL))
            o = part if o is None else o + part
        finish(x_ref, m_ref, f_ref, o, o_ref)

    @pl.when((i >= FF_CHUNKS) & (i < FF_CHUNKS + n_prompt - 1))
    def _():
        run(xp_ref, mp_ref, fp_ref, op_ref)

    @pl.when(i == FF_CHUNKS + n_prompt - 1)
    def _():
        run(xs_ref, ms_ref, fs_ref, os_ref)


def _ffn(xp, mod_p, xs, mod_s, sub, norm_w, w_in, w_out, fin_p=None, fin_s=None, norm_f=None):
    bp, seq, _ = xp.shape
    steps, bs, _ = xs.shape
    tiles = seq // FFN_ROWS
    n_prompt = bp * tiles
    final = fin_p is not None

    def chunk(i):
        return jnp.minimum(i, FF_CHUNKS - 1)

    def tile(i):
        return jnp.clip(i - (FF_CHUNKS - 1), 0, n_prompt - 1)

    tok_p = pl.BlockSpec((1, FFN_ROWS, D_MODEL), lambda i: (tile(i) // tiles, tile(i) % tiles, 0))
    tok_s = pl.BlockSpec((steps, bs, D_MODEL), lambda i: (0, 0, 0))
    in_specs = [tok_p,
                pl.BlockSpec((1, 1, 3 * D_MODEL), lambda i: (tile(i) // tiles, 0, sub)),
                tok_s,
                pl.BlockSpec((1, bs, 3 * D_MODEL), lambda i: (0, 0, sub)),
                _resident((1, D_MODEL)),
                pl.BlockSpec((D_MODEL, FF_CHUNK), lambda i: (0, chunk(i))),
                pl.BlockSpec((D_MODEL, FF_CHUNK), lambda i: (0, FF_CHUNKS + chunk(i))),
                pl.BlockSpec((FF_CHUNK, D_MODEL), lambda i: (chunk(i), 0))]
    args = [xp, mod_p, xs, mod_s, norm_w.reshape(1, D_MODEL), w_in, w_in, w_out]
    if final:
        in_specs += [pl.BlockSpec((1, 1, 2 * D_MODEL), lambda i: (tile(i) // tiles, 0, 0)),
                     pl.BlockSpec((1, bs, 2 * D_MODEL), lambda i: (0, 0, 0)),
                     _resident((1, D_MODEL))]
        args += [fin_p, fin_s, norm_f.reshape(1, D_MODEL)]
    return pl.pallas_call(
        functools.partial(_ffn_kernel, n_prompt=n_prompt, final=final),
        out_shape=(jax.ShapeDtypeStruct(xp.shape, F32), jax.ShapeDtypeStruct(xs.shape, F32)),
        grid=(FF_CHUNKS + n_prompt,),
        in_specs=in_specs,
        out_specs=(tok_p, tok_s),
        scratch_shapes=[pltpu.VMEM((2 * FF_CHUNKS, D_MODEL, FF_CHUNK), BF16),
                        pltpu.VMEM((FF_CHUNKS, FF_CHUNK, D_MODEL), BF16),
                        pltpu.VMEM((FFN_ROWS, D_MODEL), BF16),
                        pltpu.VMEM((FFN_ROWS, D_MODEL), F32)],
        compiler_params=_params(("arbitrary",)),
        name="ffn_final" if final else "ffn",
    )(*args)


def _s5_tables(lam_re, lam_im, log_dt, b_re, b_im, c_re, c_im):
    lr, li = lam_re.astype(F32), lam_im.astype(F32)
    dt = jnp.exp(log_dt.astype(F32))[:, None]
    mag = jnp.exp(lr * dt)
    ar, ai = mag * jnp.cos(li * dt), mag * jnp.sin(li * dt)
    a2r, a2i = ar * ar - ai * ai, 2.0 * ar * ai
    den = lr * lr + li * li
    cr = ((ar - 1.0) * lr + ai * li) / den
    ci = (ai * lr - (ar - 1.0) * li) / den
    br, bi = b_re.astype(F32), b_im.astype(F32)
    bb_re = cr[:, :, None] * br - ci[:, :, None] * bi
    bb_im = cr[:, :, None] * bi + ci[:, :, None] * br
    lb_re = ar[:, :, None] * bb_re - ai[:, :, None] * bb_im
    lb_im = ar[:, :, None] * bb_im + ai[:, :, None] * bb_re
    cre, cim = c_re.astype(F32), c_im.astype(F32)
    c1_re = cre * ar[:, None, :] - cim * ai[:, None, :]
    c1_im = cre * ai[:, None, :] + cim * ar[:, None, :]
    c2_re = cre * a2r[:, None, :] - cim * a2i[:, None, :]
    c2_im = cre * a2i[:, None, :] + cim * a2r[:, None, :]

    lam2_t = jnp.broadcast_to(jnp.stack([a2r, a2i]).reshape(2, S5_PAIRS, 1, LANES),
                              (2, S5_PAIRS, SUBLANES, LANES))

    groups_per_tile = LANES // S5_GROUP
    q_of = 2 * (np.arange(S5_PAIRS) % 4)[:, None] + np.arange(2)[None, :]
    slot = (np.arange(groups_per_tile)[None, :, None] == q_of[:, None, :])
    slot = np.broadcast_to(slot[:, :, None, :, None], (S5_PAIRS, groups_per_tile, 2, 2, S5_STATE))
    slot = jnp.asarray(slot.reshape(S5_PAIRS, 1, groups_per_tile, 1, 2 * LANES), F32)

    def place(parts):
        v = jnp.stack([jnp.stack(p) for p in parts])
        v = v.reshape(2, 2, S5_PAIRS, 2, S5_GROUP, S5_STATE)
        v = jnp.transpose(v, (2, 0, 4, 1, 3, 5)).reshape(S5_PAIRS, 2, 1, S5_GROUP, 2 * LANES)
        return (v * slot).reshape(S5_PAIRS, 2 * LANES, 2 * LANES).astype(BF16)

    def cn(t):
        return jnp.transpose(t, (0, 2, 1))

    wz = place([[cn(lb_re), cn(lb_im)], [cn(bb_re), cn(bb_im)]])
    ws_t = place([[c1_re, -c1_im], [c2_re, -c2_im]])

    c_cat = jnp.concatenate([cre, -cim], axis=-1)
    x_cat = jnp.stack([jnp.concatenate([bb_re, bb_im], axis=1),
                       jnp.concatenate([lb_re, lb_im], axis=1)])
    k = jnp.einsum('gom,xgmi->xgio', c_cat, x_cat, precision=lax.Precision.HIGHEST)
    k = k.reshape(2, S5_LANE_TILES, groups_per_tile, 1, S5_GROUP, S5_GROUP)
    eye = jnp.asarray(np.eye(groups_per_tile)[None, None, :, :, None, None], F32)
    k = jnp.transpose(eye * k, (0, 1, 2, 4, 3, 5)).reshape(2, S5_LANE_TILES, LANES, LANES)
    k0, k1 = k[0], k[1]
    wk = jnp.concatenate([jnp.concatenate([k0, k1], axis=-1),
                          jnp.concatenate([jnp.zeros_like(k0), k0], axis=-1)], axis=1).astype(BF16)
    return lam2_t, wz, ws_t, wk


PROJ_ROWS = 512

def _s5_prompt_kernel(x_ref, mod_ref, nw_ref, wu_ref, wz_ref, lam2_ref, ws_ref, wk_ref, d_ref, wglu_ref,
                      o_ref, hre_ref, him_ref,
                      u_slab, sre, sim, st_re, st_im, y_slab, *, tl, nb):
    i = pl.program_id(0)
    nk = tl // 2
    half = nk * nb

    @pl.when(i == 0)
    def _():
        st_re[...] = jnp.zeros_like(st_re)
        st_im[...] = jnp.zeros_like(st_im)

    seqs = max(1, PROJ_ROWS // tl)
    parts = []
    for b0 in range(0, nb, seqs):
        m = mod_ref[b0:b0 + seqs]
        h = _rms_mod(x_ref[b0:b0 + seqs], nw_ref[...], m[..., :D_MODEL], m[..., D_MODEL:2 * D_MODEL])
        parts.append(_dot(h.reshape(seqs * tl, D_MODEL).astype(BF16), wu_ref[...]))
    u = jnp.concatenate(parts, axis=0)

    for s in range(S5_LANE_TILES):
        for b in range(nb):
            u_slab[s, pl.ds(b, tl, stride=nb), :] = u[b * tl:(b + 1) * tl, s * LANES:(s + 1) * LANES]

    u_ev, u_od, u_cat = [], [], []
    for s in range(S5_LANE_TILES):
        tiles = u_slab[s].reshape(nk, 2 * nb, LANES)
        ev = tiles[:, :nb, :].reshape(half, LANES)
        od = tiles[:, nb:, :].reshape(half, LANES)
        u_ev.append(ev)
        u_od.append(od)
        u_cat.append(jnp.concatenate([ev, od], axis=-1).astype(BF16))

    for j in range(S5_PAIRS):
        z = _dot(u_cat[j // 4], wz_ref[j])
        sre[j, 0:nb, :] = st_re[j]
        sim[j, 0:nb, :] = st_im[j]
        sre[j, nb:nb + half, :] = z[:, :LANES]
        sim[j, nb:nb + half, :] = z[:, LANES:]

    pairs_per_pass = 8
    for j0 in range(0, S5_PAIRS, pairs_per_pass):
        js = range(j0, j0 + pairs_per_pass)
        lr = [lam2_ref[0, j] for j in js]
        li = [lam2_ref[1, j] for j in js]
        carry = []
        for j in js:
            carry += [st_re[j], st_im[j]]
        for k in range(nk):
            r0 = (k + 1) * nb
            for n, j in enumerate(js):
                re, im = carry[2 * n], carry[2 * n + 1]
                nre = lr[n] * re - li[n] * im + sre[j, r0:r0 + nb, :]
                nim = lr[n] * im + li[n] * re + sim[j, r0:r0 + nb, :]
                if k + 1 < nk:
                    sre[j, r0:r0 + nb, :] = nre
                    sim[j, r0:r0 + nb, :] = nim
                carry[2 * n], carry[2 * n + 1] = nre, nim
        for n, j in enumerate(js):
            st_re[j] = carry[2 * n]
            st_im[j] = carry[2 * n + 1]

    y_ev, y_od = [], []
    for jt in range(S5_LANE_TILES):
        acc = _dot(u_cat[jt], wk_ref[jt])
        for jj in range(4):
            j = 4 * jt + jj
            sp = jnp.concatenate([sre[j, 0:half, :], sim[j, 0:half, :]], axis=-1).astype(BF16)
            acc = acc + _dot_nt(sp, ws_ref[j])
        d = d_ref[:, jt * LANES:(jt + 1) * LANES]
        y_ev.append(jax.nn.gelu(acc[:, :LANES] + d * u_ev[jt]))
        y_od.append(jax.nn.gelu(acc[:, LANES:] + d * u_od[jt]))
    y_ev = jnp.concatenate(y_ev, axis=-1).reshape(nk, nb, S5_WIDTH)
    y_od = jnp.concatenate(y_od, axis=-1).reshape(nk, nb, S5_WIDTH)
    y = jnp.concatenate([y_ev, y_od], axis=1).reshape(tl * nb, S5_WIDTH)
    glu = _dot(y.astype(BF16), wglu_ref[...])
    s5o = glu[:, :S5_WIDTH] * jax.nn.sigmoid(glu[:, S5_WIDTH:])
    for s in range(S5_LANE_TILES):
        y_slab[s] = s5o[:, s * LANES:(s + 1) * LANES]
    for b in range(nb):
        for s in range(S5_LANE_TILES):
            c0 = b * S5_WIDTH + s * LANES
            o_ref[:, c0:c0 + LANES] = y_slab[s, pl.ds(b, tl, stride=nb), :].astype(BF16)

    @pl.when(i == pl.num_programs(0) - 1)
    def _():
        hre_ref[...] = jnp.concatenate([st_re[j] for j in range(S5_PAIRS)], axis=-1)
        him_ref[...] = jnp.concatenate([st_im[j] for j in range(S5_PAIRS)], axis=-1)


def _s5_prompt(x1, mod3, norm_w, w_u, two_step, d_skip, w_glu, tl=256):
    nb, seq, _ = x1.shape
    lam2_t, wz, ws_t, wk = two_step
    rows = nb * tl
    half = rows // 2
    n_state = S5_GROUPS * S5_STATE
    return pl.pallas_call(
        functools.partial(_s5_prompt_kernel, tl=tl, nb=nb),
        out_shape=(jax.ShapeDtypeStruct((seq, nb * S5_WIDTH), BF16),
                   jax.ShapeDtypeStruct((nb, n_state), F32),
                   jax.ShapeDtypeStruct((nb, n_state), F32)),
        grid=(seq // tl,),
        in_specs=[pl.BlockSpec((nb, tl, D_MODEL), lambda i: (0, i, 0)),
                  pl.BlockSpec((nb, 1, 3 * D_MODEL), lambda i: (0, 0, 1)),
                  _resident((1, D_MODEL)),
                  _resident((D_MODEL, S5_WIDTH)),
                  _resident(wz.shape), _resident(lam2_t.shape), _resident(ws_t.shape), _resident(wk.shape),
                  _resident((1, S5_WIDTH)),
                  _resident((S5_WIDTH, 2 * S5_WIDTH))],
        out_specs=(pl.BlockSpec((tl, nb * S5_WIDTH), lambda i: (i, 0)),
                   pl.BlockSpec((nb, n_state), lambda i: (0, 0)),
                   pl.BlockSpec((nb, n_state), lambda i: (0, 0))),
        scratch_shapes=[pltpu.VMEM((S5_LANE_TILES, rows, LANES), F32),
                        pltpu.VMEM((S5_PAIRS, nb + half, LANES), F32),
                        pltpu.VMEM((S5_PAIRS, nb + half, LANES), F32),
                        pltpu.VMEM((S5_PAIRS, nb, LANES), F32),
                        pltpu.VMEM((S5_PAIRS, nb, LANES), F32),
                        pltpu.VMEM((S5_LANE_TILES, rows, LANES), F32)],
        compiler_params=_params(("arbitrary",)),
        name="s5_prompt",
    )(x1, mod3, norm_w.reshape(1, D_MODEL), w_u, wz, lam2_t, ws_t, wk, d_skip.reshape(1, S5_WIDTH), w_glu)


def _rotary_tables(pos):
    half = RET_DK // 2
    inv = ROPE_BASE ** (-np.arange(half, dtype=np.float64) / half)
    ang = np.asarray(pos, np.float64)[:, None] * inv[None, :]
    cos, sin = np.cos(ang), np.sin(ang)
    return (jnp.asarray(np.concatenate([cos, cos], axis=-1), F32),
            jnp.asarray(np.concatenate([-sin, sin], axis=-1), F32))


def _decay_tables(chunk, rows):
    lg = np.log1p(-np.exp2(-5.0 - np.arange(RET_HEADS, dtype=np.float64)))
    idx = np.arange(rows, dtype=np.float64)
    valid = idx < chunk
    diff = idx[:, None] - idx[None, :]
    intra = np.where((diff >= 0) & valid[:, None] & valid[None, :],
                     np.exp(lg[:, None, None] * np.maximum(diff, 0.0)), 0.0)
    q_dec = np.where(valid[None, :], np.exp(lg[:, None] * (idx[None, :] + 1.0)), 0.0)
    k_dec = np.where(valid[None, :], np.exp(lg[:, None] * (chunk - 1.0 - idx)[None, :]), 0.0)
    c_dec = np.exp(lg * chunk)
    q_dec = np.broadcast_to(q_dec[:, :, None], (RET_HEADS, rows, RET_DV))
    k_dec = np.broadcast_to(k_dec[:, :, None], (RET_HEADS, rows, RET_DK))
    c_dec = np.broadcast_to(c_dec[:, None, None], (RET_HEADS, 1, RET_DV))
    return tuple(jnp.asarray(t, F32) for t in (intra, q_dec, k_dec, c_dec))


def _rotary(x, cs, sn):
    return x * cs + pltpu.roll(x, RET_DK // 2, axis=1) * sn


def _head_norm_gate(ret, g):
    mu = jnp.mean(ret, axis=-1, keepdims=True)
    cen = ret - mu
    var = jnp.mean(cen * cen, axis=-1, keepdims=True)
    return (g * jax.nn.sigmoid(g)) * (cen * lax.rsqrt(var + EPS))


def _ret_prompt_kernel(x_ref, mod_ref, nw_ref, wq_ref, cs_ref, sn_ref, intra_ref, qd_ref, kd_ref, cd_ref,
                       s5o_ref, wout_ref, o_ref, sout_ref, s_scr, *, tm):
    t = pl.program_id(1)

    @pl.when(t == 0)
    def _():
        s_scr[...] = jnp.zeros_like(s_scr)

    x = x_ref[0]
    m = mod_ref[0]
    proj = jnp.concatenate(
        [_dot(_rms_mod(x[r0:r0 + PROJ_ROWS], nw_ref[...], m[:, :D_MODEL], m[:, D_MODEL:2 * D_MODEL]).astype(BF16),
              wq_ref[...]) for r0 in range(0, tm, PROJ_ROWS)], axis=0)
    cs, sn = cs_ref[...], sn_ref[...]
    n_chunks = tm // RET_CHUNK
    heads = range(RET_HEADS)
    chunks = range(n_chunks)

    lhs, vbs, kvs = {}, {}, {}
    for hd in heads:
        lo = hd * RET_DK
        q = _rotary(proj[:, lo:lo + RET_DK], cs, sn)
        k = _rotary(proj[:, RET_WIDTH + lo:RET_WIDTH + lo + RET_DK], cs, sn) * (RET_DK ** -0.5)
        v = proj[:, 2 * RET_WIDTH + lo:2 * RET_WIDTH + lo + RET_DV]
        for c in chunks:
            rows = slice(c * RET_CHUNK, (c + 1) * RET_CHUNK)
            qc, kc, vb = q[rows], k[rows], v[rows].astype(BF16)
            scores = _dot_nt(qc.astype(BF16), kc.astype(BF16)) * intra_ref[hd]
            lhs[hd, c] = jnp.concatenate([scores.astype(BF16), (qc * qd_ref[hd]).astype(BF16)], axis=-1)
            vbs[hd, c] = vb
            kvs[hd, c] = _dot(jnp.transpose(kc * kd_ref[hd]).astype(BF16), vb)

    states = {}
    for hd in heads:
        s = s_scr[hd]
        for c in chunks:
            states[hd, c] = s
            s = s * cd_ref[hd] + kvs[hd, c]
        s_scr[hd] = s

    parts = [s5o_ref[...]]
    for hd in heads:
        lo = 3 * RET_WIDTH + hd * RET_DV
        outs = [_dot(lhs[hd, c], jnp.concatenate([vbs[hd, c], states[hd, c].astype(BF16)], axis=0))
                for c in chunks]
        parts.append(_head_norm_gate(jnp.concatenate(outs, axis=0), proj[:, lo:lo + RET_DV]).astype(BF16))
    mix = jnp.concatenate(parts, axis=-1)
    for r0 in range(0, tm, PROJ_ROWS):
        rows = slice(r0, r0 + PROJ_ROWS)
        o_ref[0, rows, :] = x[rows] + m[:, 2 * D_MODEL:] * _dot(mix[rows], wout_ref[...])

    @pl.when(t == pl.num_programs(1) - 1)
    def _():
        sout_ref[0] = s_scr[...]


def _ret_prompt(x1, mod3, norm_w, w_qkvg, cs, sn, decay, s5o, w_out, tm=1024):
    nb, seq, _ = x1.shape
    intra, q_dec, k_dec, c_dec = decay
    return pl.pallas_call(
        functools.partial(_ret_prompt_kernel, tm=tm),
        out_shape=(jax.ShapeDtypeStruct(x1.shape, F32),
                   jax.ShapeDtypeStruct((nb, RET_HEADS, RET_DK, RET_DV), F32)),
        grid=(nb, seq // tm),
        in_specs=[pl.BlockSpec((1, tm, D_MODEL), lambda b, t: (b, t, 0)),
                  pl.BlockSpec((1, 1, 3 * D_MODEL), lambda b, t: (b, 0, 1)),
                  _resident((1, D_MODEL)),
                  _resident((D_MODEL, 4 * RET_WIDTH)),
                  pl.BlockSpec((tm, RET_DK), lambda b, t: (t, 0)),
                  pl.BlockSpec((tm, RET_DK), lambda b, t: (t, 0)),
                  _resident(intra.shape), _resident(q_dec.shape), _resident(k_dec.shape), _resident(c_dec.shape),
                  pl.BlockSpec((tm, S5_WIDTH), lambda b, t: (t, b)),
                  _resident((D_MODEL, D_MODEL))],
        out_specs=(pl.BlockSpec((1, tm, D_MODEL), lambda b, t: (b, t, 0)),
                   pl.BlockSpec((1, RET_HEADS, RET_DK, RET_DV), lambda b, t: (b, 0, 0, 0))),
        scratch_shapes=[pltpu.VMEM((RET_HEADS, RET_DK, RET_DV), F32)],
        compiler_params=_params(("arbitrary", "arbitrary")),
        name="ret_prompt",
    )(x1, mod3, norm_w.reshape(1, D_MODEL), w_qkvg, cs, sn, intra, q_dec, k_dec, c_dec, s5o, w_out)


def _mix_in_sample_kernel(x_ref, mod_ref, nw_ref, wu_ref, wq_ref, wz_ref, lam2_ref, ws_ref, wk_ref, d_ref,
                          wglu_ref, h0re_ref, h0im_ref, qkvg_ref, s5o_ref, hre_ref, him_ref, *, steps, nb):
    x = x_ref[...]
    m = mod_ref[...]
    h = _rms_mod(x, nw_ref[...], m[..., :D_MODEL], m[..., D_MODEL:2 * D_MODEL])
    hb = h.reshape(steps * nb, D_MODEL).astype(BF16)
    proj = _dot(hb, wq_ref[...])
    qkvg_ref[...] = jnp.zeros_like(qkvg_ref)
    for s in range(4 * RET_HEADS):
        for t in range(steps):
            qkvg_ref[s, pl.ds(t, nb, stride=SUBLANES), :] = proj[t * nb:(t + 1) * nb, s * LANES:(s + 1) * LANES]
    u = _dot(hb, wu_ref[...])
    nk = steps // 2

    def rows_of(parity, lanes):
        return jnp.concatenate([u[(2 * k + parity) * nb:(2 * k + parity + 1) * nb, lanes] for k in range(nk)],
                               axis=0)

    u_ev, u_od, u_cat = [], [], []
    for s in range(S5_LANE_TILES):
        lanes = slice(s * LANES, (s + 1) * LANES)
        u_ev.append(rows_of(0, lanes))
        u_od.append(rows_of(1, lanes))
        u_cat.append(jnp.concatenate([u_ev[s], u_od[s]], axis=-1).astype(BF16))

    prev = []
    for j in range(S5_PAIRS):
        z = _dot(u_cat[j // 4], wz_ref[j])
        lr = lam2_ref[0, j][:1]
        li = lam2_ref[1, j][:1]
        sre = h0re_ref[:, j * LANES:(j + 1) * LANES]
        sim = h0im_ref[:, j * LANES:(j + 1) * LANES]
        pre, pim = [], []
        for k in range(nk):
            pre.append(sre)
            pim.append(sim)
            zre, zim = z[k * nb:(k + 1) * nb, :LANES], z[k * nb:(k + 1) * nb, LANES:]
            sre, sim = lr * sre - li * sim + zre, lr * sim + li * sre + zim
        hre_ref[:, j * LANES:(j + 1) * LANES] = sre
        him_ref[:, j * LANES:(j + 1) * LANES] = sim
        prev.append(jnp.concatenate([jnp.concatenate(pre, axis=0), jnp.concatenate(pim, axis=0)],
                                    axis=-1).astype(BF16))

    y_ev, y_od = [], []
    for jt in range(S5_LANE_TILES):
        acc = _dot(u_cat[jt], wk_ref[jt])
        for jj in range(4):
            acc = acc + _dot_nt(prev[4 * jt + jj], ws_ref[4 * jt + jj])
        d = d_ref[:, jt * LANES:(jt + 1) * LANES]
        y_ev.append(jax.nn.gelu(acc[:, :LANES] + d * u_ev[jt]))
        y_od.append(jax.nn.gelu(acc[:, LANES:] + d * u_od[jt]))
    y_ev = jnp.concatenate(y_ev, axis=-1)
    y_od = jnp.concatenate(y_od, axis=-1)
    y = jnp.concatenate([part[k * nb:(k + 1) * nb] for k in range(nk) for part in (y_ev, y_od)], axis=0)
    glu = _dot(y.astype(BF16), wglu_ref[...])
    s5o_ref[...] = (glu[:, :S5_WIDTH] * jax.nn.sigmoid(glu[:, S5_WIDTH:])).astype(BF16)


def _mix_in_sample(x1, mod3, norm_w, w_u, w_qkvg, two_step, d_skip, w_glu, h0_re, h0_im):
    steps, nb, _ = x1.shape
    rows = steps * nb
    n_state = S5_GROUPS * S5_STATE
    slabs = (4 * RET_HEADS, nb * SUBLANES, LANES)
    lam2_t, wz, ws_t, wk = two_step
    args = (x1, mod3, norm_w.reshape(1, D_MODEL), w_u, w_qkvg, wz, lam2_t, ws_t, wk,
            d_skip.reshape(1, S5_WIDTH), w_glu, h0_re, h0_im)
    in_specs = [pl.BlockSpec((steps, nb, D_MODEL), lambda i: (0, 0, 0)),
                pl.BlockSpec((1, nb, 3 * D_MODEL), lambda i: (0, 0, 1))]
    in_specs += [_resident(a.shape) for a in args[2:]]
    return pl.pallas_call(
        functools.partial(_mix_in_sample_kernel, steps=steps, nb=nb),
        out_shape=(jax.ShapeDtypeStruct(slabs, F32),
                   jax.ShapeDtypeStruct((rows, S5_WIDTH), BF16),
                   jax.ShapeDtypeStruct((nb, n_state), F32),
                   jax.ShapeDtypeStruct((nb, n_state), F32)),
        grid=(1,),
        in_specs=in_specs,
        out_specs=(pl.BlockSpec(slabs, lambda i: (0, 0, 0)),
                   pl.BlockSpec((rows, S5_WIDTH), lambda i: (0, 0)),
                   pl.BlockSpec((nb, n_state), lambda i: (0, 0)),
                   pl.BlockSpec((nb, n_state), lambda i: (0, 0))),
        compiler_params=_params(("arbitrary",)),
        name="mix_in_sample",
    )(*args)


def _ret_sample_kernel(qkvg_ref, s_ref, cs_ref, sn_ref, intra_ref, qd_ref, kd_ref, cd_ref,
                       o_ref, sout_ref, *, bb):
    cs, sn = cs_ref[...], sn_ref[...]
    pairs = [(b, hd) for b in range(bb) for hd in range(RET_HEADS)]

    def tile(kind, b, hd):
        return qkvg_ref[kind * RET_HEADS + hd, b * SUBLANES:(b + 1) * SUBLANES, :]

    scores, cross_lhs, vbs = {}, {}, {}
    for b, hd in pairs:
        q = _rotary(tile(0, b, hd), cs, sn)
        k = _rotary(tile(1, b, hd), cs, sn) * (RET_DK ** -0.5)
        vb = tile(2, b, hd).astype(BF16)
        scores[b, hd] = _dot_nt(q.astype(BF16), k.astype(BF16)) * intra_ref[hd]
        cross_lhs[b, hd] = (q * qd_ref[hd]).astype(BF16)
        vbs[b, hd] = vb
        sout_ref[b, hd] = s_ref[b, hd] * cd_ref[hd] + _dot(jnp.transpose(k * kd_ref[hd]).astype(BF16), vb)
    for b, hd in pairs:
        o = _dot(scores[b, hd].astype(BF16), vbs[b, hd]) + _dot(cross_lhs[b, hd], s_ref[b, hd].astype(BF16))
        o_ref[hd, b * SUBLANES:(b + 1) * SUBLANES, :] = _head_norm_gate(o, tile(3, b, hd))


def _ret_sample(qkvg, s0, cs, sn, decay, bb=16):
    nb = s0.shape[0]
    intra, q_dec, k_dec, c_dec = decay
    st = pl.BlockSpec((bb, RET_HEADS, RET_DK, RET_DV), lambda i: (i, 0, 0, 0))
    return pl.pallas_call(
        functools.partial(_ret_sample_kernel, bb=bb),
        out_shape=(jax.ShapeDtypeStruct((RET_HEADS, nb * SUBLANES, LANES), F32),
                   jax.ShapeDtypeStruct(s0.shape, F32)),
        grid=(nb // bb,),
        in_specs=[pl.BlockSpec((4 * RET_HEADS, bb * SUBLANES, LANES), lambda i: (0, i, 0)),
                  st, _resident(cs.shape), _resident(sn.shape),
                  _resident(intra.shape), _resident(q_dec.shape), _resident(k_dec.shape), _resident(c_dec.shape)],
        out_specs=(pl.BlockSpec((RET_HEADS, bb * SUBLANES, LANES), lambda i: (0, i, 0)), st),
        compiler_params=_params(("arbitrary",)),
        name="ret_sample",
    )(qkvg, s0, cs, sn, intra, q_dec, k_dec, c_dec)


def _mix_out_sample_kernel(x_ref, mod_ref, s5o_ref, ret_ref, wout_ref, o_ref, *, steps, nb):
    x = x_ref[...]
    gate = mod_ref[...][..., 2 * D_MODEL:]
    ret = jnp.concatenate(
        [jnp.concatenate([ret_ref[hd, pl.ds(t, nb, stride=SUBLANES), :] for hd in range(RET_HEADS)], axis=-1)
         for t in range(steps)], axis=0)
    mix = jnp.concatenate([s5o_ref[...], ret.astype(BF16)], axis=-1)
    o_ref[...] = x + gate * _dot(mix, wout_ref[...]).reshape(steps, nb, D_MODEL)


def _mix_out_sample(x1, mod3, s5o, ret, w_out):
    steps, nb, _ = x1.shape
    rows = steps * nb
    return pl.pallas_call(
        functools.partial(_mix_out_sample_kernel, steps=steps, nb=nb),
        out_shape=jax.ShapeDtypeStruct(x1.shape, F32),
        grid=(1,),
        in_specs=[pl.BlockSpec((steps, nb, D_MODEL), lambda i: (0, 0, 0)),
                  pl.BlockSpec((1, nb, 3 * D_MODEL), lambda i: (0, 0, 1)),
                  pl.BlockSpec((rows, S5_WIDTH), lambda i: (0, 0)),
                  pl.BlockSpec(ret.shape, lambda i: (0, 0, 0)),
                  _resident((D_MODEL, D_MODEL))],
        out_specs=pl.BlockSpec((steps, nb, D_MODEL), lambda i: (0, 0, 0)),
        compiler_params=_params(("arbitrary",)),
        name="mix_out_sample",
    )(x1, mod3, s5o, ret, w_out)


def kernel(x_prompt, x_sample, state_ssm_re, state_ssm_im, state_ret, c_prompt, c_sample,
           w_ada, b_ada, norm_ffn1, ffn1_w_in, ffn1_w_out, norm_mix, w_in_mix,
           s5_lambda_re, s5_lambda_im, s5_log_dt, s5_b_re, s5_b_im, s5_c_re, s5_c_im, s5_d, s5_w_glu,
           w_out_mix, norm_ffn2, ffn2_w_in, ffn2_w_out, w_ada_final, b_ada_final, norm_final):
    depth = w_ada.shape[0]
    bp, seq, _ = x_prompt.shape
    bs, steps, _ = x_sample.shape
    assert seq % RET_CHUNK == 0 and steps % 2 == 0 and steps <= SUBLANES

    c_rows = bs + bp

    cs_p, sn_p = _rotary_tables(np.arange(seq))
    decay_p = _decay_tables(RET_CHUNK, RET_CHUNK)
    rows_s = SUBLANES
    pos_s = np.concatenate([PAST_LEN + np.arange(steps), np.zeros(rows_s - steps)])
    cs_s, sn_s = _rotary_tables(pos_s)
    decay_s = _decay_tables(steps, rows_s)

    xp = x_prompt
    xs = jnp.transpose(x_sample, (1, 0, 2))
    outs = {k: [] for k in ("p_re", "p_im", "p_ret", "s_re", "s_im", "s_ret")}
    for l in range(depth):
        mod = _ada(c_sample, c_prompt, w_ada[l], b_ada[l])
        mod_p = mod[bs:c_rows][:, None, :]
        mod_s = mod[None]
        w_u = w_in_mix[l][:, :S5_WIDTH].astype(BF16)
        w_qkvg = w_in_mix[l][:, S5_WIDTH:].astype(BF16)
        w_glu = s5_w_glu[l].astype(BF16)
        w_out = w_out_mix[l].astype(BF16)
        two_step = _s5_tables(s5_lambda_re[l], s5_lambda_im[l], s5_log_dt[l],
                              s5_b_re[l], s5_b_im[l], s5_c_re[l], s5_c_im[l])
        last = l == depth - 1
        if last:
            fin = _ada(c_sample, c_prompt, w_ada_final, b_ada_final)
            fin_p, fin_s = fin[bs:c_rows][:, None, :], fin[None]
        else:
            fin_p = fin_s = None

        xp, xs = _ffn(xp, mod_p, xs, mod_s, 0, norm_ffn1[l], ffn1_w_in[l], ffn1_w_out[l])

        s5o, hre, him = _s5_prompt(xp, mod_p, norm_mix[l], w_u, two_step, s5_d[l], w_glu)
        xp, sret = _ret_prompt(xp, mod_p, norm_mix[l], w_qkvg, cs_p, sn_p, decay_p, s5o, w_out)
        outs["p_re"].append(hre.reshape(bp, S5_GROUPS, S5_STATE))
        outs["p_im"].append(him.reshape(bp, S5_GROUPS, S5_STATE))
        outs["p_ret"].append(sret)

        qkvg, s5o_s, hre_s, him_s = _mix_in_sample(
            xs, mod_s, norm_mix[l], w_u, w_qkvg, two_step, s5_d[l], w_glu,
            state_ssm_re[l].reshape(bs, -1), state_ssm_im[l].reshape(bs, -1))
        ret_s, sret_s = _ret_sample(qkvg, state_ret[l], cs_s, sn_s, decay_s)
        xs = _mix_out_sample(xs, mod_s, s5o_s, ret_s, w_out)

        xp, xs = _ffn(xp, mod_p, xs, mod_s, 2, norm_ffn2[l], ffn2_w_in[l], ffn2_w_out[l],
                      fin_p, fin_s, norm_final if last else None)
        outs["s_re"].append(hre_s.reshape(bs, S5_GROUPS, S5_STATE))
        outs["s_im"].append(him_s.reshape(bs, S5_GROUPS, S5_STATE))
        outs["s_ret"].append(sret_s)

    y_prompt = xp
    y_sample = jnp.transpose(xs, (1, 0, 2))
    return (y_prompt, y_sample, jnp.stack(outs["p_re"]), jnp.stack(outs["p_im"]), jnp.stack(outs["p_ret"]),
            jnp.stack(outs["s_re"]), jnp.stack(outs["s_im"]), jnp.stack(outs["s_ret"]))
```

```python
import functools

import numpy as np
import jax
import jax.numpy as jnp
from jax import lax
from jax.experimental import pallas as pl
from jax.experimental.pallas import tpu as pltpu

F32 = jnp.float32
BF16 = jnp.bfloat16

D_MODEL = 1024
D_FF = 2816
N_MOD = 9
S5_WIDTH = 512
S5_GROUP = 16
S5_GROUPS = 32
S5_STATE = 64
RET_HEADS = 4
RET_DK = 128
RET_DV = 128
RET_WIDTH = 512
RET_CHUNK = 128
PAST_LEN = 16384
ROPE_BASE = 10000.0
EPS = 1e-6

LANES = 128
SUBLANES = 8
S5_PAIRS = S5_GROUPS // 2
S5_LANE_TILES = S5_WIDTH // LANES
VMEM_LIMIT_BYTES = 56 * 1024 * 1024


def _dot(a, b):
    return jnp.dot(a, b, preferred_element_type=F32)


def _dot_nt(a, b):
    return lax.dot_general(a, b, (((1,), (1,)), ((), ())), preferred_element_type=F32)


def _params(semantics):
    return pltpu.CompilerParams(dimension_semantics=semantics, vmem_limit_bytes=VMEM_LIMIT_BYTES)


def _resident(shape):
    nd = len(shape)
    return pl.BlockSpec(shape, lambda *_: (0,) * nd, pipeline_mode=pl.Buffered(1))


def _rms_mod(x, norm_w, shift, scale):
    xn = x * lax.rsqrt(jnp.mean(x * x, axis=-1, keepdims=True) + EPS)
    return xn * (norm_w * (1.0 + scale)) + shift


ADA_COLS = 1024
ADA_STREAMS = 4


def _ada_kernel(cs_ref, cp_ref, *refs):
    w_refs, b_ref, o_ref = refs[:ADA_STREAMS], refs[ADA_STREAMS], refs[ADA_STREAMS + 1]
    pad = o_ref.shape[0] - cs_ref.shape[0] - cp_ref.shape[0]
    c = jnp.concatenate([cs_ref[...], cp_ref[...], jnp.zeros((pad, cs_ref.shape[1]), F32)], axis=0)
    s = (c * jax.nn.sigmoid(c)).astype(BF16)
    band = s.shape[1] // ADA_STREAMS
    acc = b_ref[...]
    for n, w_ref in enumerate(w_refs):
        acc = acc + _dot(s[:, n * band:(n + 1) * band], w_ref[...].astype(BF16))
    o_ref[...] = acc


def _ada(c_s, c_p, w, b):
    k, n = w.shape
    m = -(-(c_s.shape[0] + c_p.shape[0]) // 16) * 16
    band = k // ADA_STREAMS
    w_specs = [pl.BlockSpec((band, ADA_COLS), lambda j, r=r: (r, j)) for r in range(ADA_STREAMS)]
    return pl.pallas_call(
        _ada_kernel,
        out_shape=jax.ShapeDtypeStruct((m, n), F32),
        grid=(n // ADA_COLS,),
        in_specs=[pl.BlockSpec(c_s.shape, lambda j: (0, 0)), pl.BlockSpec(c_p.shape, lambda j: (0, 0))]
        + w_specs + [pl.BlockSpec((1, ADA_COLS), lambda j: (0, j))],
        out_specs=pl.BlockSpec((m, ADA_COLS), lambda j: (0, j)),
        compiler_params=_params(("arbitrary",)),
        name="ada_mod",
    )(c_s, c_p, *([w] * ADA_STREAMS), b.reshape(1, n))


FF_CHUNK = 256
FF_CHUNKS = D_FF // FF_CHUNK
FF_GROUPS = ((0, 6), (6, FF_CHUNKS))
FFN_ROWS = 512


def _ffn_in(x, m, norm_w):
    a_dim, r_dim, _ = x.shape
    h = _rms_mod(x, norm_w, m[..., :D_MODEL], m[..., D_MODEL:2 * D_MODEL])
    return h.reshape(a_dim * r_dim, D_MODEL).astype(BF16)


def _ffn_out(x, m, o, fin, norm_f):
    y = x + (0.5 * m[..., 2 * D_MODEL:]) * o.reshape(x.shape)
    if fin is not None:
        y = _rms_mod(y, norm_f, fin[..., :D_MODEL], fin[..., D_MODEL:])
    return y


def _swiglu_part(hb, wa, wb, wo):
    a = _dot(hb, wa)
    b = _dot(hb, wb)
    return _dot((a * jax.nn.sigmoid(a) * b).astype(BF16), wo)


def _ffn_kernel(*refs, n_prompt, final):
    if final:
        (xp_ref, mp_ref, xs_ref, ms_ref, nw_ref, wa_ref, wb_ref, wo_ref, fp_ref, fs_ref, nf_ref,
         op_ref, os_ref, win_scr, wout_scr, hb_scr, acc_scr) = refs
    else:
        (xp_ref, mp_ref, xs_ref, ms_ref, nw_ref, wa_ref, wb_ref, wo_ref,
         op_ref, os_ref, win_scr, wout_scr, hb_scr, acc_scr) = refs
        fp_ref = fs_ref = nf_ref = None
    i = pl.program_id(0)

    def finish(x_ref, m_ref, f_ref, o, o_ref):
        o_ref[...] = _ffn_out(x_ref[...], m_ref[...], o,
                              f_ref[...] if final else None, nf_ref[...] if final else None)

    @pl.when(i < FF_CHUNKS)
    def _():
        wa, wb, wo = wa_ref[...].astype(BF16), wb_ref[...].astype(BF16), wo_ref[...].astype(BF16)
        win_scr[i] = wa
        win_scr[FF_CHUNKS + i] = wb
        wout_scr[i] = wo

        @pl.when(i == 0)
        def _():
            hb_scr[...] = _ffn_in(xp_ref[...], mp_ref[...], nw_ref[...])
            acc_scr[...] = jnp.zeros_like(acc_scr)

        acc_scr[...] += _swiglu_part(hb_scr[...], wa, wb, wo)

        @pl.when(i == FF_CHUNKS - 1)
        def _():
            finish(xp_ref, mp_ref, fp_ref, acc_scr[...], op_ref)

    def run(x_ref, m_ref, f_ref, o_ref):
        hb = _ffn_in(x_ref[...], m_ref[...], nw_ref[...])
        o = None
        for c0, c1 in FF_GROUPS:
            a = jnp.concatenate([_dot(hb, win_scr[c]) for c in range(c0, c1)], axis=-1)
            b = jnp.concatenate([_dot(hb, win_scr[FF_CHUNKS + c]) for c in range(c0, c1)], axis=-1)
            act = (a * jax.nn.sigmoid(a) * b).astype(BF16)
            part = _dot(act, wout_scr[c0:c1].reshape((c1 - c0) * FF_CHUNK, D_MODEL))
            o = part if o is None else o + part
        finish(x_ref, m_ref, f_ref, o, o_ref)

    @pl.when((i >= FF_CHUNKS) & (i < FF_CHUNKS + n_prompt - 1))
    def _():
        run(xp_ref, mp_ref, fp_ref, op_ref)

    @pl.when(i == FF_CHUNKS + n_prompt - 1)
    def _():
        run(xs_ref, ms_ref, fs_ref, os_ref)


def _ffn(xp, mod_p, xs, mod_s, sub, norm_w, w_in, w_out, fin_p=None, fin_s=None, norm_f=None):
    bp, seq, _ = xp.shape
    steps, bs, _ = xs.shape
    tiles = seq // FFN_ROWS
    n_prompt = bp * tiles
    final = fin_p is not None

    def chunk(i):
        return jnp.minimum(i, FF_CHUNKS - 1)

    def tile(i):
        return jnp.clip(i - (FF_CHUNKS - 1), 0, n_prompt - 1)

    tok_p = pl.BlockSpec((1, FFN_ROWS, D_MODEL), lambda i: (tile(i) // tiles, tile(i) % tiles, 0))
    tok_s = pl.BlockSpec((steps, bs, D_MODEL), lambda i: (0, 0, 0))
    in_specs = [tok_p,
                pl.BlockSpec((1, 1, 3 * D_MODEL), lambda i: (tile(i) // tiles, 0, sub)),
                tok_s,
                pl.BlockSpec((1, bs, 3 * D_MODEL), lambda i: (0, 0, sub)),
                _resident((1, D_MODEL)),
                pl.BlockSpec((D_MODEL, FF_CHUNK), lambda i: (0, chunk(i))),
                pl.BlockSpec((D_MODEL, FF_CHUNK), lambda i: (0, FF_CHUNKS + chunk(i))),
                pl.BlockSpec((FF_CHUNK, D_MODEL), lambda i: (chunk(i), 0))]
    args = [xp, mod_p, xs, mod_s, norm_w.reshape(1, D_MODEL), w_in, w_in, w_out]
    if final:
        in_specs += [pl.BlockSpec((1, 1, 2 * D_MODEL), lambda i: (tile(i) // tiles, 0, 0)),
                     pl.BlockSpec((1, bs, 2 * D_MODEL), lambda i: (0, 0, 0)),
                     _resident((1, D_MODEL))]
        args += [fin_p, fin_s, norm_f.reshape(1, D_MODEL)]
    return pl.pallas_call(
        functools.partial(_ffn_kernel, n_prompt=n_prompt, final=final),
        out_shape=(jax.ShapeDtypeStruct(xp.shape, F32), jax.ShapeDtypeStruct(xs.shape, F32)),
        grid=(FF_CHUNKS + n_prompt,),
        in_specs=in_specs,
        out_specs=(tok_p, tok_s),
        scratch_shapes=[pltpu.VMEM((2 * FF_CHUNKS, D_MODEL, FF_CHUNK), BF16),
                        pltpu.VMEM((FF_CHUNKS, FF_CHUNK, D_MODEL), BF16),
                        pltpu.VMEM((FFN_ROWS, D_MODEL), BF16),
                        pltpu.VMEM((FFN_ROWS, D_MODEL), F32)],
        compiler_params=_params(("arbitrary",)),
        name="ffn_final" if final else "ffn",
    )(*args)


def _s5_tables(lam_re, lam_im, log_dt, b_re, b_im, c_re, c_im):
    lr, li = lam_re.astype(F32), lam_im.astype(F32)
    dt = jnp.exp(log_dt.astype(F32))[:, None]
    mag = jnp.exp(lr * dt)
    ar, ai = mag * jnp.cos(li * dt), mag * jnp.sin(li * dt)
    a2r, a2i = ar * ar - ai * ai, 2.0 * ar * ai
    den = lr * lr + li * li
    cr = ((ar - 1.0) * lr + ai * li) / den
    ci = (ai * lr - (ar - 1.0) * li) / den
    br, bi = b_re.astype(F32), b_im.astype(F32)
    bb_re = cr[:, :, None] * br - ci[:, :, None] * bi
    bb_im = cr[:, :, None] * bi + ci[:, :, None] * br
    lb_re = ar[:, :, None] * bb_re - ai[:, :, None] * bb_im
    lb_im = ar[:, :, None] * bb_im + ai[:, :, None] * bb_re
    cre, cim = c_re.astype(F32), c_im.astype(F32)
    c1_re = cre * ar[:, None, :] - cim * ai[:, None, :]
    c1_im = cre * ai[:, None, :] + cim * ar[:, None, :]
    c2_re = cre * a2r[:, None, :] - cim * a2i[:, None, :]
    c2_im = cre * a2i[:, None, :] + cim * a2r[:, None, :]

    lam2_t = jnp.broadcast_to(jnp.stack([a2r, a2i]).reshape(2, S5_PAIRS, 1, LANES),
                              (2, S5_PAIRS, SUBLANES, LANES))

    groups_per_tile = LANES // S5_GROUP
    q_of = 2 * (np.arange(S5_PAIRS) % 4)[:, None] + np.arange(2)[None, :]
    slot = (np.arange(groups_per_tile)[None, :, None] == q_of[:, None, :])
    slot = np.broadcast_to(slot[:, :, None, :, None], (S5_PAIRS, groups_per_tile, 2, 2, S5_STATE))
    slot = jnp.asarray(slot.reshape(S5_PAIRS, 1, groups_per_tile, 1, 2 * LANES), F32)

    def place(parts):
        v = jnp.stack([jnp.stack(p) for p in parts])
        v = v.reshape(2, 2, S5_PAIRS, 2, S5_GROUP, S5_STATE)
        v = jnp.transpose(v, (2, 0, 4, 1, 3, 5)).reshape(S5_PAIRS, 2, 1, S5_GROUP, 2 * LANES)
        return (v * slot).reshape(S5_PAIRS, 2 * LANES, 2 * LANES).astype(BF16)

    def cn(t):
        return jnp.transpose(t, (0, 2, 1))

    wz = place([[cn(lb_re), cn(lb_im)], [cn(bb_re), cn(bb_im)]])
    ws_t = place([[c1_re, -c1_im], [c2_re, -c2_im]])

    c_cat = jnp.concatenate([cre, -cim], axis=-1)
    x_cat = jnp.stack([jnp.concatenate([bb_re, bb_im], axis=1),
                       jnp.concatenate([lb_re, lb_im], axis=1)])
    k = jnp.einsum('gom,xgmi->xgio', c_cat, x_cat, precision=lax.Precision.HIGHEST)
    k = k.reshape(2, S5_LANE_TILES, groups_per_tile, 1, S5_GROUP, S5_GROUP)
    eye = jnp.asarray(np.eye(groups_per_tile)[None, None, :, :, None, None], F32)
    k = jnp.transpose(eye * k, (0, 1, 2, 4, 3, 5)).reshape(2, S5_LANE_TILES, LANES, LANES)
    k0, k1 = k[0], k[1]
    wk = jnp.concatenate([jnp.concatenate([k0, k1], axis=-1),
                          jnp.concatenate([jnp.zeros_like(k0), k0], axis=-1)], axis=1).astype(BF16)
    return lam2_t, wz, ws_t, wk


PROJ_ROWS = 512

def _s5_prompt_kernel(x_ref, mod_ref, nw_ref, wu_ref, wz_ref, lam2_ref, ws_ref, wk_ref, d_ref, wglu_ref,
                      o_ref, hre_ref, him_ref,
                      u_slab, sre, sim, st_re, st_im, y_slab, *, tl, nb):
    i = pl.program_id(0)
    nk = tl // 2
    half = nk * nb

    @pl.when(i == 0)
    def _():
        st_re[...] = jnp.zeros_like(st_re)
        st_im[...] = jnp.zeros_like(st_im)

    seqs = max(1, PROJ_ROWS // tl)
    parts = []
    for b0 in range(0, nb, seqs):
        m = mod_ref[b0:b0 + seqs]
        h = _rms_mod(x_ref[b0:b0 + seqs], nw_ref[...], m[..., :D_MODEL], m[..., D_MODEL:2 * D_MODEL])
        parts.append(_dot(h.reshape(seqs * tl, D_MODEL).astype(BF16), wu_ref[...]))
    u = jnp.concatenate(parts, axis=0)

    for s in range(S5_LANE_TILES):
        for b in range(nb):
            u_slab[s, pl.ds(b, tl, stride=nb), :] = u[b * tl:(b + 1) * tl, s * LANES:(s + 1) * LANES]

    u_ev, u_od, u_cat = [], [], []
    for s in range(S5_LANE_TILES):
        tiles = u_slab[s].reshape(nk, 2 * nb, LANES)
        ev = tiles[:, :nb, :].reshape(half, LANES)
        od = tiles[:, nb:, :].reshape(half, LANES)
        u_ev.append(ev)
        u_od.append(od)
        u_cat.append(jnp.concatenate([ev, od], axis=-1).astype(BF16))

    for j in range(S5_PAIRS):
        z = _dot(u_cat[j // 4], wz_ref[j])
        sre[j, 0:nb, :] = st_re[j]
        sim[j, 0:nb, :] = st_im[j]
        sre[j, nb:nb + half, :] = z[:, :LANES]
        sim[j, nb:nb + half, :] = z[:, LANES:]

    pairs_per_pass = 8
    for j0 in range(0, S5_PAIRS, pairs_per_pass):
        js = range(j0, j0 + pairs_per_pass)
        lr = [lam2_ref[0, j] for j in js]
        li = [lam2_ref[1, j] for j in js]
        carry = []
        for j in js:
            carry += [st_re[j], st_im[j]]
        for k in range(nk):
            r0 = (k + 1) * nb
            for n, j in enumerate(js):
                re, im = carry[2 * n], carry[2 * n + 1]
                nre = lr[n] * re - li[n] * im + sre[j, r0:r0 + nb, :]
                nim = lr[n] * im + li[n] * re + sim[j, r0:r0 + nb, :]
                if k + 1 < nk:
                    sre[j, r0:r0 + nb, :] = nre
                    sim[j, r0:r0 + nb, :] = nim
                carry[2 * n], carry[2 * n + 1] = nre, nim
        for n, j in enumerate(js):
            st_re[j] = carry[2 * n]
            st_im[j] = carry[2 * n + 1]

    y_ev, y_od = [], []
    for jt in range(S5_LANE_TILES):
        acc = _dot(u_cat[jt], wk_ref[jt])
        for jj in range(4):
            j = 4 * jt + jj
            sp = jnp.concatenate([sre[j, 0:half, :], sim[j, 0:half, :]], axis=-1).astype(BF16)
            acc = acc + _dot_nt(sp, ws_ref[j])
        d = d_ref[:, jt * LANES:(jt + 1) * LANES]
        y_ev.append(jax.nn.gelu(acc[:, :LANES] + d * u_ev[jt]))
        y_od.append(jax.nn.gelu(acc[:, LANES:] + d * u_od[jt]))
    y_ev = jnp.concatenate(y_ev, axis=-1).reshape(nk, nb, S5_WIDTH)
    y_od = jnp.concatenate(y_od, axis=-1).reshape(nk, nb, S5_WIDTH)
    y = jnp.concatenate([y_ev, y_od], axis=1).reshape(tl * nb, S5_WIDTH)
    glu = _dot(y.astype(BF16), wglu_ref[...])
    s5o = glu[:, :S5_WIDTH] * jax.nn.sigmoid(glu[:, S5_WIDTH:])
    for s in range(S5_LANE_TILES):
        y_slab[s] = s5o[:, s * LANES:(s + 1) * LANES]
    for b in range(nb):
        for s in range(S5_LANE_TILES):
            c0 = b * S5_WIDTH + s * LANES
            o_ref[:, c0:c0 + LANES] = y_slab[s, pl.ds(b, tl, stride=nb), :].astype(BF16)

    @pl.when(i == pl.num_programs(0) - 1)
    def _():
        hre_ref[...] = jnp.concatenate([st_re[j] for j in range(S5_PAIRS)], axis=-1)
        him_ref[...] = jnp.concatenate([st_im[j] for j in range(S5_PAIRS)], axis=-1)


def _s5_prompt(x1, mod3, norm_w, w_u, two_step, d_skip, w_glu, tl=256):
    nb, seq, _ = x1.shape
    lam2_t, wz, ws_t, wk = two_step
    rows = nb * tl
    half = rows // 2
    n_state = S5_GROUPS * S5_STATE
    return pl.pallas_call(
        functools.partial(_s5_prompt_kernel, tl=tl, nb=nb),
        out_shape=(jax.ShapeDtypeStruct((seq, nb * S5_WIDTH), BF16),
                   jax.ShapeDtypeStruct((nb, n_state), F32),
                   jax.ShapeDtypeStruct((nb, n_state), F32)),
        grid=(seq // tl,),
        in_specs=[pl.BlockSpec((nb, tl, D_MODEL), lambda i: (0, i, 0)),
                  pl.BlockSpec((nb, 1, 3 * D_MODEL), lambda i: (0, 0, 1)),
                  _resident((1, D_MODEL)),
                  _resident((D_MODEL, S5_WIDTH)),
                  _resident(wz.shape), _resident(lam2_t.shape), _resident(ws_t.shape), _resident(wk.shape),
                  _resident((1, S5_WIDTH)),
                  _resident((S5_WIDTH, 2 * S5_WIDTH))],
        out_specs=(pl.BlockSpec((tl, nb * S5_WIDTH), lambda i: (i, 0)),
                   pl.BlockSpec((nb, n_state), lambda i: (0, 0)),
                   pl.BlockSpec((nb, n_state), lambda i: (0, 0))),
        scratch_shapes=[pltpu.VMEM((S5_LANE_TILES, rows, LANES), F32),
                        pltpu.VMEM((S5_PAIRS, nb + half, LANES), F32),
                        pltpu.VMEM((S5_PAIRS, nb + half, LANES), F32),
                        pltpu.VMEM((S5_PAIRS, nb, LANES), F32),
                        pltpu.VMEM((S5_PAIRS, nb, LANES), F32),
                        pltpu.VMEM((S5_LANE_TILES, rows, LANES), F32)],
        compiler_params=_params(("arbitrary",)),
        name="s5_prompt",
    )(x1, mod3, norm_w.reshape(1, D_MODEL), w_u, wz, lam2_t, ws_t, wk, d_skip.reshape(1, S5_WIDTH), w_glu)


def _rotary_tables(pos):
    half = RET_DK // 2
    inv = ROPE_BASE ** (-np.arange(half, dtype=np.float64) / half)
    ang = np.asarray(pos, np.float64)[:, None] * inv[None, :]
    cos, sin = np.cos(ang), np.sin(ang)
    return (jnp.asarray(np.concatenate([cos, cos], axis=-1), F32),
            jnp.asarray(np.concatenate([-sin, sin], axis=-1), F32))


def _decay_tables(chunk, rows):
    lg = np.log1p(-np.exp2(-5.0 - np.arange(RET_HEADS, dtype=np.float64)))
    idx = np.arange(rows, dtype=np.float64)
    valid = idx < chunk
    diff = idx[:, None] - idx[None, :]
    intra = np.where((diff >= 0) & valid[:, None] & valid[None, :],
                     np.exp(lg[:, None, None] * np.maximum(diff, 0.0)), 0.0)
    q_dec = np.where(valid[None, :], np.exp(lg[:, None] * (idx[None, :] + 1.0)), 0.0)
    k_dec = np.where(valid[None, :], np.exp(lg[:, None] * (chunk - 1.0 - idx)[None, :]), 0.0)
    c_dec = np.exp(lg * chunk)
    q_dec = np.broadcast_to(q_dec[:, :, None], (RET_HEADS, rows, RET_DV))
    k_dec = np.broadcast_to(k_dec[:, :, None], (RET_HEADS, rows, RET_DK))
    c_dec = np.broadcast_to(c_dec[:, None, None], (RET_HEADS, 1, RET_DV))
    return tuple(jnp.asarray(t, F32) for t in (intra, q_dec, k_dec, c_dec))


def _rotary(x, cs, sn):
    return x * cs + pltpu.roll(x, RET_DK // 2, axis=1) * sn


def _head_norm_gate(ret, g):
    mu = jnp.mean(ret, axis=-1, keepdims=True)
    cen = ret - mu
    var = jnp.mean(cen * cen, axis=-1, keepdims=True)
    return (g * jax.nn.sigmoid(g)) * (cen * lax.rsqrt(var + EPS))


def _ret_prompt_kernel(x_ref, mod_ref, nw_ref, wq_ref, cs_ref, sn_ref, intra_ref, qd_ref, kd_ref, cd_ref,
                       s5o_ref, wout_ref, o_ref, sout_ref, s_scr, *, tm):
    t = pl.program_id(1)

    @pl.when(t == 0)
    def _():
        s_scr[...] = jnp.zeros_like(s_scr)

    x = x_ref[0]
    m = mod_ref[0]
    proj = jnp.concatenate(
        [_dot(_rms_mod(x[r0:r0 + PROJ_ROWS], nw_ref[...], m[:, :D_MODEL], m[:, D_MODEL:2 * D_MODEL]).astype(BF16),
              wq_ref[:, S5_WIDTH:]) for r0 in range(0, tm, PROJ_ROWS)], axis=0)
    cs, sn = cs_ref[...], sn_ref[...]
    n_chunks = tm // RET_CHUNK
    heads = range(RET_HEADS)
    chunks = range(n_chunks)

    lhs, vbs, kvs = {}, {}, {}
    for hd in heads:
        lo = hd * RET_DK
        q = _rotary(proj[:, lo:lo + RET_DK], cs, sn)
        k = _rotary(proj[:, RET_WIDTH + lo:RET_WIDTH + lo + RET_DK], cs, sn) * (RET_DK ** -0.5)
        v = proj[:, 2 * RET_WIDTH + lo:2 * RET_WIDTH + lo + RET_DV]
        for c in chunks:
            rows = slice(c * RET_CHUNK, (c + 1) * RET_CHUNK)
            qc, kc, vb = q[rows], k[rows], v[rows].astype(BF16)
            scores = _dot_nt(qc.astype(BF16), kc.astype(BF16)) * intra_ref[hd]
            lhs[hd, c] = jnp.concatenate([scores.astype(BF16), (qc * qd_ref[hd]).astype(BF16)], axis=-1)
            vbs[hd, c] = vb
            kvs[hd, c] = _dot(jnp.transpose(kc * kd_ref[hd]).astype(BF16), vb)

    states = {}
    for hd in heads:
        s = s_scr[hd]
        for c in chunks:
            states[hd, c] = s
            s = s * cd_ref[hd] + kvs[hd, c]
        s_scr[hd] = s

    parts = [s5o_ref[...]]
    for hd in heads:
        lo = 3 * RET_WIDTH + hd * RET_DV
        outs = [_dot(lhs[hd, c], jnp.concatenate([vbs[hd, c], states[hd, c].astype(BF16)], axis=0))
                for c in chunks]
        parts.append(_head_norm_gate(jnp.concatenate(outs, axis=0), proj[:, lo:lo + RET_DV]).astype(BF16))
    mix = jnp.concatenate(parts, axis=-1)
    for r0 in range(0, tm, PROJ_ROWS):
        rows = slice(r0, r0 + PROJ_ROWS)
        o_ref[0, rows, :] = x[rows] + m[:, 2 * D_MODEL:] * _dot(mix[rows], wout_ref[...])

    @pl.when(t == pl.num_programs(1) - 1)
    def _():
        sout_ref[0] = s_scr[...]


def _ret_prompt(x1, mod3, norm_w, w_mix, cs, sn, decay, s5o, w_out, tm=1024):
    nb, seq, _ = x1.shape
    intra, q_dec, k_dec, c_dec = decay
    return pl.pallas_call(
        functools.partial(_ret_prompt_kernel, tm=tm),
        out_shape=(jax.ShapeDtypeStruct(x1.shape, F32),
                   jax.ShapeDtypeStruct((nb, RET_HEADS, RET_DK, RET_DV), F32)),
        grid=(nb, seq // tm),
        in_specs=[pl.BlockSpec((1, tm, D_MODEL), lambda b, t: (b, t, 0)),
                  pl.BlockSpec((1, 1, 3 * D_MODEL), lambda b, t: (b, 0, 1)),
                  _resident((1, D_MODEL)),
                  _resident(w_mix.shape),
                  pl.BlockSpec((tm, RET_DK), lambda b, t: (t, 0)),
                  pl.BlockSpec((tm, RET_DK), lambda b, t: (t, 0)),
                  _resident(intra.shape), _resident(q_dec.shape), _resident(k_dec.shape), _resident(c_dec.shape),
                  pl.BlockSpec((tm, S5_WIDTH), lambda b, t: (t, b)),
                  _resident((D_MODEL, D_MODEL))],
        out_specs=(pl.BlockSpec((1, tm, D_MODEL), lambda b, t: (b, t, 0)),
                   pl.BlockSpec((1, RET_HEADS, RET_DK, RET_DV), lambda b, t: (b, 0, 0, 0))),
        scratch_shapes=[pltpu.VMEM((RET_HEADS, RET_DK, RET_DV), F32)],
        compiler_params=_params(("arbitrary", "arbitrary")),
        name="ret_prompt",
    )(x1, mod3, norm_w.reshape(1, D_MODEL), w_mix, cs, sn, intra, q_dec, k_dec, c_dec, s5o, w_out)


def _mix_in_sample_kernel(x_ref, mod_ref, nw_ref, wmix_ref, wz_ref, lam2_ref, ws_ref, wk_ref, d_ref,
                          wglu_ref, h0re_ref, h0im_ref, qkvg_ref, s5o_ref, hre_ref, him_ref, *, steps, nb):
    x = x_ref[...]
    m = mod_ref[...]
    h = _rms_mod(x, nw_ref[...], m[..., :D_MODEL], m[..., D_MODEL:2 * D_MODEL])
    hb = h.reshape(steps * nb, D_MODEL).astype(BF16)
    proj = _dot(hb, wmix_ref[:, S5_WIDTH:])
    qkvg_ref[...] = jnp.zeros_like(qkvg_ref)
    for s in range(4 * RET_HEADS):
        for t in range(steps):
            qkvg_ref[s, pl.ds(t, nb, stride=SUBLANES), :] = proj[t * nb:(t + 1) * nb, s * LANES:(s + 1) * LANES]
    u = _dot(hb, wmix_ref[:, :S5_WIDTH])
    nk = steps // 2

    def rows_of(parity, lanes):
        return jnp.concatenate([u[(2 * k + parity) * nb:(2 * k + parity + 1) * nb, lanes] for k in range(nk)],
                               axis=0)

    u_ev, u_od, u_cat = [], [], []
    for s in range(S5_LANE_TILES):
        lanes = slice(s * LANES, (s + 1) * LANES)
        u_ev.append(rows_of(0, lanes))
        u_od.append(rows_of(1, lanes))
        u_cat.append(jnp.concatenate([u_ev[s], u_od[s]], axis=-1).astype(BF16))

    prev = []
    for j in range(S5_PAIRS):
        z = _dot(u_cat[j // 4], wz_ref[j])
        lr = lam2_ref[0, j][:1]
        li = lam2_ref[1, j][:1]
        sre = h0re_ref[:, j * LANES:(j + 1) * LANES]
        sim = h0im_ref[:, j * LANES:(j + 1) * LANES]
        pre, pim = [], []
        for k in range(nk):
            pre.append(sre)
            pim.append(sim)
            zre, zim = z[k * nb:(k + 1) * nb, :LANES], z[k * nb:(k + 1) * nb, LANES:]
            sre, sim = lr * sre - li * sim + zre, lr * sim + li * sre + zim
        hre_ref[:, j * LANES:(j + 1) * LANES] = sre
        him_ref[:, j * LANES:(j + 1) * LANES] = sim
        prev.append(jnp.concatenate([jnp.concatenate(pre, axis=0), jnp.concatenate(pim, axis=0)],
                                    axis=-1).astype(BF16))

    y_ev, y_od = [], []
    for jt in range(S5_LANE_TILES):
        acc = _dot(u_cat[jt], wk_ref[jt])
        for jj in range(4):
            acc = acc + _dot_nt(prev[4 * jt + jj], ws_ref[4 * jt + jj])
        d = d_ref[:, jt * LANES:(jt + 1) * LANES]
        y_ev.append(jax.nn.gelu(acc[:, :LANES] + d * u_ev[jt]))
        y_od.append(jax.nn.gelu(acc[:, LANES:] + d * u_od[jt]))
    y_ev = jnp.concatenate(y_ev, axis=-1)
    y_od = jnp.concatenate(y_od, axis=-1)
    y = jnp.concatenate([part[k * nb:(k + 1) * nb] for k in range(nk) for part in (y_ev, y_od)], axis=0)
    glu = _dot(y.astype(BF16), wglu_ref[...])
    s5o_ref[...] = (glu[:, :S5_WIDTH] * jax.nn.sigmoid(glu[:, S5_WIDTH:])).astype(BF16)


def _mix_in_sample(x1, mod3, norm_w, w_mix, two_step, d_skip, w_glu, h0_re, h0_im):
    steps, nb, _ = x1.shape
    rows = steps * nb
    n_state = S5_GROUPS * S5_STATE
    slabs = (4 * RET_HEADS, nb * SUBLANES, LANES)
    lam2_t, wz, ws_t, wk = two_step
    args = (x1, mod3, norm_w.reshape(1, D_MODEL), w_mix, wz, lam2_t, ws_t, wk,
            d_skip.reshape(1, S5_WIDTH), w_glu, h0_re, h0_im)
    in_specs = [pl.BlockSpec((steps, nb, D_MODEL), lambda i: (0, 0, 0)),
                pl.BlockSpec((1, nb, 3 * D_MODEL), lambda i: (0, 0, 1))]
    in_specs += [_resident(a.shape) for a in args[2:]]
    return pl.pallas_call(
        functools.partial(_mix_in_sample_kernel, steps=steps, nb=nb),
        out_shape=(jax.ShapeDtypeStruct(slabs, F32),
                   jax.ShapeDtypeStruct((rows, S5_WIDTH), BF16),
                   jax.ShapeDtypeStruct((nb, n_state), F32),
                   jax.ShapeDtypeStruct((nb, n_state), F32)),
        grid=(1,),
        in_specs=in_specs,
        out_specs=(pl.BlockSpec(slabs, lambda i: (0, 0, 0)),
                   pl.BlockSpec((rows, S5_WIDTH), lambda i: (0, 0)),
                   pl.BlockSpec((nb, n_state), lambda i: (0, 0)),
                   pl.BlockSpec((nb, n_state), lambda i: (0, 0))),
        compiler_params=_params(("arbitrary",)),
        name="mix_in_sample",
    )(*args)


def _ret_sample_kernel(qkvg_ref, s_ref, cs_ref, sn_ref, intra_ref, qd_ref, kd_ref, cd_ref,
                       o_ref, sout_ref, *, bb):
    cs, sn = cs_ref[...], sn_ref[...]
    pairs = [(b, hd) for b in range(bb) for hd in range(RET_HEADS)]

    def tile(kind, b, hd):
        return qkvg_ref[kind * RET_HEADS + hd, b * SUBLANES:(b + 1) * SUBLANES, :]

    scores, cross_lhs, vbs = {}, {}, {}
    for b, hd in pairs:
        q = _rotary(tile(0, b, hd), cs, sn)
        k = _rotary(tile(1, b, hd), cs, sn) * (RET_DK ** -0.5)
        vb = tile(2, b, hd).astype(BF16)
        scores[b, hd] = _dot_nt(q.astype(BF16), k.astype(BF16)) * intra_ref[hd]
        cross_lhs[b, hd] = (q * qd_ref[hd]).astype(BF16)
        vbs[b, hd] = vb
        sout_ref[b, hd] = s_ref[b, hd] * cd_ref[hd] + _dot(jnp.transpose(k * kd_ref[hd]).astype(BF16), vb)
    for b, hd in pairs:
        o = _dot(scores[b, hd].astype(BF16), vbs[b, hd]) + _dot(cross_lhs[b, hd], s_ref[b, hd].astype(BF16))
        o_ref[hd, b * SUBLANES:(b + 1) * SUBLANES, :] = _head_norm_gate(o, tile(3, b, hd))


def _ret_sample(qkvg, s0, cs, sn, decay, bb=16):
    nb = s0.shape[0]
    intra, q_dec, k_dec, c_dec = decay
    st = pl.BlockSpec((bb, RET_HEADS, RET_DK, RET_DV), lambda i: (i, 0, 0, 0))
    return pl.pallas_call(
        functools.partial(_ret_sample_kernel, bb=bb),
        out_shape=(jax.ShapeDtypeStruct((RET_HEADS, nb * SUBLANES, LANES), F32),
                   jax.ShapeDtypeStruct(s0.shape, F32)),
        grid=(nb // bb,),
        in_specs=[pl.BlockSpec((4 * RET_HEADS, bb * SUBLANES, LANES), lambda i: (0, i, 0)),
                  st, _resident(cs.shape), _resident(sn.shape),
                  _resident(intra.shape), _resident(q_dec.shape), _resident(k_dec.shape), _resident(c_dec.shape)],
        out_specs=(pl.BlockSpec((RET_HEADS, bb * SUBLANES, LANES), lambda i: (0, i, 0)), st),
        compiler_params=_params(("arbitrary",)),
        name="ret_sample",
    )(qkvg, s0, cs, sn, intra, q_dec, k_dec, c_dec)


def _mix_out_sample_kernel(x_ref, mod_ref, s5o_ref, ret_ref, wout_ref, o_ref, *, steps, nb):
    x = x_ref[...]
    gate = mod_ref[...][..., 2 * D_MODEL:]
    ret = jnp.concatenate(
        [jnp.concatenate([ret_ref[hd, pl.ds(t, nb, stride=SUBLANES), :] for hd in range(RET_HEADS)], axis=-1)
         for t in range(steps)], axis=0)
    mix = jnp.concatenate([s5o_ref[...], ret.astype(BF16)], axis=-1)
    o_ref[...] = x + gate * _dot(mix, wout_ref[...]).reshape(steps, nb, D_MODEL)


def _mix_out_sample(x1, mod3, s5o, ret, w_out):
    steps, nb, _ = x1.shape
    rows = steps * nb
    return pl.pallas_call(
        functools.partial(_mix_out_sample_kernel, steps=steps, nb=nb),
        out_shape=jax.ShapeDtypeStruct(x1.shape, F32),
        grid=(1,),
        in_specs=[pl.BlockSpec((steps, nb, D_MODEL), lambda i: (0, 0, 0)),
                  pl.BlockSpec((1, nb, 3 * D_MODEL), lambda i: (0, 0, 1)),
                  pl.BlockSpec((rows, S5_WIDTH), lambda i: (0, 0)),
                  pl.BlockSpec(ret.shape, lambda i: (0, 0, 0)),
                  _resident((D_MODEL, D_MODEL))],
        out_specs=pl.BlockSpec((steps, nb, D_MODEL), lambda i: (0, 0, 0)),
        compiler_params=_params(("arbitrary",)),
        name="mix_out_sample",
    )(x1, mod3, s5o, ret, w_out)


def kernel(x_prompt, x_sample, state_ssm_re, state_ssm_im, state_ret, c_prompt, c_sample,
           w_ada, b_ada, norm_ffn1, ffn1_w_in, ffn1_w_out, norm_mix, w_in_mix,
           s5_lambda_re, s5_lambda_im, s5_log_dt, s5_b_re, s5_b_im, s5_c_re, s5_c_im, s5_d, s5_w_glu,
           w_out_mix, norm_ffn2, ffn2_w_in, ffn2_w_out, w_ada_final, b_ada_final, norm_final):
    depth = w_ada.shape[0]
    bp, seq, _ = x_prompt.shape
    bs, steps, _ = x_sample.shape
    assert seq % RET_CHUNK == 0 and steps % 2 == 0 and steps <= SUBLANES

    c_rows = bs + bp

    cs_p, sn_p = _rotary_tables(np.arange(seq))
    decay_p = _decay_tables(RET_CHUNK, RET_CHUNK)
    rows_s = SUBLANES
    pos_s = np.concatenate([PAST_LEN + np.arange(steps), np.zeros(rows_s - steps)])
    cs_s, sn_s = _rotary_tables(pos_s)
    decay_s = _decay_tables(steps, rows_s)

    xp = x_prompt
    xs = jnp.transpose(x_sample, (1, 0, 2))
    outs = {k: [] for k in ("p_re", "p_im", "p_ret", "s_re", "s_im", "s_ret")}
    for l in range(depth):
        mod = _ada(c_sample, c_prompt, w_ada[l], b_ada[l])
        mod_p = mod[bs:c_rows][:, None, :]
        mod_s = mod[None]
        w_mix = w_in_mix[l].astype(BF16)
        w_glu = s5_w_glu[l].astype(BF16)
        w_out = w_out_mix[l].astype(BF16)
        two_step = _s5_tables(s5_lambda_re[l], s5_lambda_im[l], s5_log_dt[l],
                              s5_b_re[l], s5_b_im[l], s5_c_re[l], s5_c_im[l])
        last = l == depth - 1
        if last:
            fin = _ada(c_sample, c_prompt, w_ada_final, b_ada_final)
            fin_p, fin_s = fin[bs:c_rows][:, None, :], fin[None]
        else:
            fin_p = fin_s = None

        xp, xs = _ffn(xp, mod_p, xs, mod_s, 0, norm_ffn1[l], ffn1_w_in[l], ffn1_w_out[l])

        s5o, hre, him = _s5_prompt(xp, mod_p, norm_mix[l], w_mix, two_step, s5_d[l], w_glu)
        xp, sret = _ret_prompt(xp, mod_p, norm_mix[l], w_mix, cs_p, sn_p, decay_p, s5o, w_out)
        outs["p_re"].append(hre.reshape(bp, S5_GROUPS, S5_STATE))
        outs["p_im"].append(him.reshape(bp, S5_GROUPS, S5_STATE))
        outs["p_ret"].append(sret)

        qkvg, s5o_s, hre_s, him_s = _mix_in_sample(
            xs, mod_s, norm_mix[l], w_mix, two_step, s5_d[l], w_glu,
            state_ssm_re[l].reshape(bs, -1), state_ssm_im[l].reshape(bs, -1))
        ret_s, sret_s = _ret_sample(qkvg, state_ret[l], cs_s, sn_s, decay_s)
        xs = _mix_out_sample(xs, mod_s, s5o_s, ret_s, w_out)

        xp, xs = _ffn(xp, mod_p, xs, mod_s, 2, norm_ffn2[l], ffn2_w_in[l], ffn2_w_out[l],
                      fin_p, fin_s, norm_final if last else None)
        outs["s_re"].append(hre_s.reshape(bs, S5_GROUPS, S5_STATE))
        outs["s_im"].append(him_s.reshape(bs, S5_GROUPS, S5_STATE))
        outs["s_ret"].append(sret_s)

    y_prompt = xp
    y_sample = jnp.transpose(xs, (1, 0, 2))
    return (y_prompt, y_sample, jnp.stack(outs["p_re"]), jnp.stack(outs["p_im"]), jnp.stack(outs["p_ret"]),
            jnp.stack(outs["s_re"]), jnp.stack(outs["s_im"]), jnp.stack(outs["s_ret"]))
```
